```python
import jax, jax.numpy as jnp
from jax import lax
import numpy as np

D_MODEL = 1024
BATCH = 16
SEQ = 2048
DEPTH = 2

CTX_LEN = 256
GRID_W = 64
N_MOD = 6
EPS = 1e-6
D_RNN = 1024
RNN_BLOCKS = 16
RNN_BW = D_RNN // RNN_BLOCKS
CONV_W = 4
CONV_PAD = (2, 1)
LRU_C = 8.0
NA_HEADS = 16
NA_HEAD_DIM = 64
D_ATT = NA_HEADS * NA_HEAD_DIM
WIN_H = 8
WIN_W = 16
Q_COL_BLOCK = WIN_W
K_COL_BLOCK = 2 * WIN_W
IN_SIZES = (D_RNN, D_ATT, D_ATT, D_RNN, D_ATT, D_MODEL, D_MODEL)
IN_COLS = 2 * D_RNN + 3 * D_ATT + 2 * D_MODEL
CTX_STATE_COLS = D_RNN + 2 * D_ATT
D_FF = 2816
N_EXPERTS = 8
TOP_K = 2
D_FF_EXPERT = 3584

kernel_name = 'hybrid_rglru_natten_moe_dit_block'


def rms_norm(x):
    xf = x.astype(jnp.float32)
    return (xf * lax.rsqrt(jnp.mean(xf * xf, axis=-1, keepdims=True) + EPS)).astype(x.dtype)


def head_rms_norm(x, gain):
    return rms_norm(x) * gain


def modulate(x, shift, scale):
    return rms_norm(x) * (1 + scale) + shift


def adaln(cond, w_mod, b_mod):
    return jnp.split(jax.nn.silu(cond) @ w_mod + b_mod, N_MOD, axis=-1)


def split_columns(z):
    parts, off = [], 0
    for size in IN_SIZES:
        if off >= z.shape[-1]:
            break
        parts.append(z[..., off:off + size])
        off += size
    return parts


def heads(t):
    return t.reshape(*t.shape[:-1], NA_HEADS, NA_HEAD_DIM)


def dw_conv(x, w, b):
    y = lax.conv_general_dilated(x, w[:, None, :].astype(x.dtype), window_strides=(1,), padding=(CONV_PAD,),
                                 dimension_numbers=('NWC', 'WIO', 'NWC'), feature_group_count=x.shape[-1])
    return y + b.astype(x.dtype)


def rglru_coeffs(xc, lam, w_gates, b_gates):
    xb = xc.reshape(*xc.shape[:-1], RNN_BLOCKS, RNN_BW)
    g = jnp.einsum('blnc,gncd->gblnd', xb, w_gates).reshape(2, *xc.shape) + b_gates[:, None, None, :]
    r = jax.nn.sigmoid(g[0].astype(jnp.float32))
    i = jax.nn.sigmoid(g[1].astype(jnp.float32))
    log_a = LRU_C * r * jax.nn.log_sigmoid(lam.astype(jnp.float32))
    a = jnp.exp(log_a)
    u = jnp.sqrt(-jnp.expm1(2.0 * log_a)) * (i * xc.astype(jnp.float32))
    return a, u


def linear_scan(a, u, h0, reverse, return_seq):
    def step(h, au):
        h = au[0] * h + au[1]
        return h, (h if return_seq else None)
    h_last, hs = lax.scan(step, h0, (jnp.swapaxes(a, 0, 1), jnp.swapaxes(u, 0, 1)), reverse=reverse)
    return (jnp.swapaxes(hs, 0, 1) if return_seq else None), h_last


def bidirectional_rglru(xc_ctx, xc_lat, rg_lambda, rg_w, rg_b, ctx_out):
    h0 = jnp.zeros((xc_lat.shape[0], D_RNN), jnp.float32)
    ys_ctx, ys_lat = [], []
    for d, reverse in enumerate((False, True)):
        a_c, u_c = rglru_coeffs(xc_ctx, rg_lambda[d], rg_w[d], rg_b[d])
        hs_c, h_c = linear_scan(a_c, u_c, h0, reverse, ctx_out)
        a_l, u_l = rglru_coeffs(xc_lat, rg_lambda[d], rg_w[d], rg_b[d])
        hs_l, _ = linear_scan(a_l, u_l, h_c, reverse, True)
        ys_ctx.append(hs_c)
        ys_lat.append(hs_l)
    y_lat = (ys_lat[0] + ys_lat[1]).astype(xc_lat.dtype)
    y_ctx = (ys_ctx[0] + ys_ctx[1]).astype(xc_ctx.dtype) if ctx_out else None
    return y_ctx, y_lat


def na_tables(rows):
    kr = min(WIN_H, rows)
    n_cb = GRID_W // Q_COL_BLOCK
    qcol = np.arange(GRID_W).reshape(n_cb, Q_COL_BLOCK)
    kstart = np.clip(np.arange(n_cb) * Q_COL_BLOCK - WIN_W // 2, 0, GRID_W - K_COL_BLOCK)
    kcol = kstart[:, None] + np.arange(K_COL_BLOCK)[None, :]
    wstart = np.clip(qcol - WIN_W // 2, 0, GRID_W - WIN_W)[..., None]
    kc = kcol[:, None, :]
    col_valid = (kc >= wstart) & (kc < wstart + WIN_W)
    col_off = np.clip(kc - qcol[..., None] + WIN_W - 1, 0, 2 * WIN_W - 2)
    return kr, kcol, col_valid, col_off


def neighbourhood_attention(q, k, v, k_ctx, v_ctx, rpb):
    b, s, h, dh = q.shape
    rows = s // GRID_W
    kr, kcol, col_valid, col_off = na_tables(rows)
    n_cb = kcol.shape[0]
    n_lat = kr * K_COL_BLOCK
    scale = dh ** -0.5
    qg = q.reshape(b, rows, n_cb, Q_COL_BLOCK, h, dh)
    kg = k.reshape(b, rows, GRID_W, h, dh)
    vg = v.reshape(b, rows, GRID_W, h, dh)
    col_bias = rpb[:, :, col_off].astype(jnp.float32)
    mask = jnp.asarray(col_valid)[:, :, None, :]

    def row_block(r):
        start = jnp.clip(r - kr // 2, 0, rows - kr)
        k_blk = lax.dynamic_slice_in_dim(kg, start, kr, axis=1)[:, :, kcol]
        v_blk = lax.dynamic_slice_in_dim(vg, start, kr, axis=1)[:, :, kcol]
        q_blk = lax.dynamic_index_in_dim(qg, r, axis=1, keepdims=False)
        bias = jnp.take(col_bias, start + jnp.arange(kr) - r + WIN_H - 1, axis=1)
        s_lat = (jnp.einsum('bjqhd,brjkhd->bhjqrk', q_blk, k_blk).astype(jnp.float32) * scale
                 + jnp.transpose(bias, (0, 2, 3, 1, 4)))
        s_lat = jnp.where(mask, s_lat, -jnp.inf).reshape(b, h, n_cb, Q_COL_BLOCK, n_lat)
        s_ctx = jnp.einsum('bjqhd,bchd->bhjqc', q_blk, k_ctx).astype(jnp.float32) * scale
        p = jax.nn.softmax(jnp.concatenate([s_lat, s_ctx], axis=-1), axis=-1).astype(v.dtype)
        p_lat = p[..., :n_lat].reshape(b, h, n_cb, Q_COL_BLOCK, kr, K_COL_BLOCK)
        o = (jnp.einsum('bhjqrk,brjkhd->bjqhd', p_lat, v_blk)
             + jnp.einsum('bhjqc,bchd->bjqhd', p[..., n_lat:], v_ctx))
        return o.reshape(b, GRID_W, h, dh)

    out = lax.map(row_block, jnp.arange(rows))
    return jnp.moveaxis(out, 0, 1).reshape(b, s, h * dh)


def context_attention(q, k, v):
    s = jnp.einsum('bqhd,bkhd->bhqk', q, k).astype(jnp.float32) * q.shape[-1] ** -0.5
    p = jax.nn.softmax(s, axis=-1).astype(v.dtype)
    o = jnp.einsum('bhqk,bkhd->bqhd', p, v)
    return o.reshape(*o.shape[:2], -1)


def merge_branches(y_rnn, y_na, g_rnn, g_na, w_rnn_o, w_na_o, w_out):
    return (jax.nn.sigmoid(g_rnn) * (y_rnn @ w_rnn_o) + jax.nn.sigmoid(g_na) * (y_na @ w_na_o)) @ w_out


def token_mixer(h_ctx, h_lat, w_in, conv_w, conv_b, rg_lambda, rg_w, rg_b, q_gain, k_gain, rpb,
                w_rnn_o, w_na_o, w_out, ctx_out):
    x_l, k_l, v_l, y_l, q_l, gr_l, gn_l = split_columns(h_lat @ w_in)
    z_c = split_columns(h_ctx @ (w_in if ctx_out else w_in[:, :CTX_STATE_COLS]))
    x_c, k_c, v_c = z_c[:3]
    rnn_ctx, rnn_lat = bidirectional_rglru(dw_conv(x_c, conv_w, conv_b), dw_conv(x_l, conv_w, conv_b),
                                           rg_lambda, rg_w, rg_b, ctx_out)
    k_c = head_rms_norm(heads(k_c), k_gain)
    v_c = heads(v_c)
    na_lat = neighbourhood_attention(head_rms_norm(heads(q_l), q_gain), head_rms_norm(heads(k_l), k_gain),
                                     heads(v_l), k_c, v_c, rpb)
    out_lat = merge_branches(rnn_lat * jax.nn.gelu(y_l), na_lat, gr_l, gn_l, w_rnn_o, w_na_o, w_out)
    out_ctx = None
    if ctx_out:
        y_c, q_c, gr_c, gn_c = z_c[3:]
        na_ctx = context_attention(head_rms_norm(heads(q_c), q_gain), k_c, v_c)
        out_ctx = merge_branches(rnn_ctx * jax.nn.gelu(y_c), na_ctx, gr_c, gn_c, w_rnn_o, w_na_o, w_out)
    return out_ctx, out_lat


def swiglu(h, w1, w3, w2):
    return (jax.nn.silu(h @ w1) * (h @ w3)) @ w2


def moe_swiglu(h, router, w1, w3, w2):
    logits = (h @ router).astype(jnp.float32)
    top_val, top_idx = lax.top_k(logits, TOP_K)
    top_w = jax.nn.softmax(top_val, axis=-1)
    gates = jnp.sum(jax.nn.one_hot(top_idx, N_EXPERTS, dtype=jnp.float32) * top_w[..., None], axis=-2).astype(h.dtype)
    out = jnp.zeros_like(h)
    for e in range(N_EXPERTS):
        out = out + gates[..., e:e + 1] * swiglu(h, w1[e], w3[e], w2[e])
    return out


def setup_inputs(seed: int = 0) -> dict:
    key = jax.random.key(seed)
    ks = jax.random.split(key, 25)
    f32 = jnp.float32
    n_dense = (DEPTH + 1) // 2
    n_moe = DEPTH // 2

    def nrm(k, shape, scale):
        return jax.random.normal(k, shape, f32) * scale

    a0 = jax.random.uniform(ks[9], (DEPTH, 2, D_RNN), f32, 0.9, 0.999) ** (1.0 / LRU_C)
    return {
        'x': nrm(ks[0], (BATCH, SEQ, D_MODEL), 1.0),
        'c': nrm(ks[1], (BATCH, D_MODEL), 1.0),
        'ctx': nrm(ks[2], (BATCH, CTX_LEN, D_MODEL), 1.0),
        'c_ctx': nrm(ks[3], (D_MODEL,), 1.0),
        'w_mod': nrm(ks[4], (DEPTH, D_MODEL, N_MOD * D_MODEL), D_MODEL ** -0.5),
        'b_mod': nrm(ks[5], (DEPTH, N_MOD * D_MODEL), 0.02),
        'w_in': nrm(ks[6], (DEPTH, D_MODEL, IN_COLS), D_MODEL ** -0.5),
        'conv_w': nrm(ks[7], (DEPTH, CONV_W, D_RNN), CONV_W ** -0.5),
        'conv_b': nrm(ks[8], (DEPTH, D_RNN), 0.02),
        'rg_lambda': jnp.log(a0) - jnp.log1p(-a0),
        'rg_w': nrm(ks[10], (DEPTH, 2, 2, RNN_BLOCKS, RNN_BW, RNN_BW), RNN_BW ** -0.5),
        'rg_b': nrm(ks[11], (DEPTH, 2, 2, D_RNN), 0.02),
        'q_gain': 1.0 + nrm(ks[12], (DEPTH, NA_HEAD_DIM), 0.02),
        'k_gain': 1.0 + nrm(ks[13], (DEPTH, NA_HEAD_DIM), 0.02),
        'rpb': nrm(ks[14], (DEPTH, NA_HEADS, 2 * WIN_H - 1, 2 * WIN_W - 1), 0.1),
        'w_rnn_o': nrm(ks[15], (DEPTH, D_RNN, D_MODEL), D_RNN ** -0.5),
        'w_na_o': nrm(ks[16], (DEPTH, D_ATT, D_MODEL), D_ATT ** -0.5),
        'w_out': nrm(ks[17], (DEPTH, D_MODEL, D_MODEL), D_MODEL ** -0.5),
        'ffn_w1': nrm(ks[18], (n_dense, D_MODEL, D_FF), D_MODEL ** -0.5),
        'ffn_w3': nrm(ks[19], (n_dense, D_MODEL, D_FF), D_MODEL ** -0.5),
        'ffn_w2': nrm(ks[20], (n_dense, D_FF, D_MODEL), D_FF ** -0.5),
        'router': nrm(ks[21], (n_moe, D_MODEL, N_EXPERTS), D_MODEL ** -0.5),
        'moe_w1': nrm(ks[22], (n_moe, N_EXPERTS, D_MODEL, D_FF_EXPERT), D_MODEL ** -0.5),
        'moe_w3': nrm(ks[23], (n_moe, N_EXPERTS, D_MODEL, D_FF_EXPERT), D_MODEL ** -0.5),
        'moe_w2': nrm(ks[24], (n_moe, N_EXPERTS, D_FF_EXPERT, D_MODEL), D_FF_EXPERT ** -0.5),
    }


def reference(x, c, ctx, c_ctx, w_mod, b_mod, w_in, conv_w, conv_b, rg_lambda, rg_w, rg_b, q_gain, k_gain,
              rpb, w_rnn_o, w_na_o, w_out, ffn_w1, ffn_w3, ffn_w2, router, moe_w1, moe_w3, moe_w2):
    xc = ctx
    for l in range(DEPTH):
        ctx_out = l < DEPTH - 1
        sh1, sc1, g1, sh2, sc2, g2 = [m[:, None, :] for m in adaln(c, w_mod[l], b_mod[l])]
        csh1, csc1, cg1, csh2, csc2, cg2 = adaln(c_ctx, w_mod[l], b_mod[l])
        mix_ctx, mix_lat = token_mixer(modulate(xc, csh1, csc1), modulate(x, sh1, sc1), w_in[l], conv_w[l],
                                       conv_b[l], rg_lambda[l], rg_w[l], rg_b[l], q_gain[l], k_gain[l], rpb[l],
                                       w_rnn_o[l], w_na_o[l], w_out[l], ctx_out)
        x = x + g1 * mix_lat
        if ctx_out:
            xc = xc + cg1 * mix_ctx
        if l % 2 == 0:
            ffn = lambda h, j=l // 2: swiglu(h, ffn_w1[j], ffn_w3[j], ffn_w2[j])
        else:
            ffn = lambda h, j=l // 2: moe_swiglu(h, router[j], moe_w1[j], moe_w3[j], moe_w2[j])
        x = x + g2 * ffn(modulate(x, sh2, sc2))
        if ctx_out:
            xc = xc + cg2 * ffn(modulate(xc, csh2, csc2))
    return x
```

```python
import functools

import numpy as np
import jax
import jax.numpy as jnp
from jax import lax
from jax.experimental import pallas as pl
from jax.experimental.pallas import tpu as pltpu

F32 = jnp.float32
BF16 = jnp.bfloat16

EPS = 1e-6
N_MOD = 6
GRID_W = 64
WIN_H = 8
WIN_W = 16
HEAD_DIM = 64
RNN_BW = 64
CONV_TAPS = 4
CONV_LEFT = 2
LRU_C = 8.0
TOP_K = 2
MASK_VALUE = -1e30

LANES = 128
MXU_DIM = 256
VMEM_LIMIT_BYTES = 56 * 1024 * 1024

NA_GROUP_ROWS = 4
NA_KEY_ROWS = 12


def _params(*sem):
    return pltpu.CompilerParams(dimension_semantics=sem, vmem_limit_bytes=VMEM_LIMIT_BYTES)


def _sigmoid(x):
    return 0.5 * (jnp.tanh(0.5 * x) + 1.0)


def _silu(x):
    return x * _sigmoid(x)


def _gelu_tanh(x):
    return 0.5 * x * (1.0 + jnp.tanh(np.sqrt(2.0 / np.pi) * (x + 0.044715 * (x * x * x))))


def _modulated_norm(x, shift, scale):
    ms = jnp.mean(x * x, axis=-1, keepdims=True)
    return x * lax.rsqrt(ms + EPS) * (1.0 + scale) + shift


def _pick_tile(n, cap, mult):
    best = None
    for t in range(mult, min(n, cap) + 1, mult):
        if n % t == 0:
            best = t
    assert best is not None, (n, cap, mult)
    return best


def _adaln_kernel(c_ref, w_ref, b_ref, o_ref):
    s = _silu(c_ref[...]).astype(BF16)
    o_ref[...] = jnp.dot(s, w_ref[...].astype(BF16), preferred_element_type=F32) + b_ref[...]


def _adaln(cond, w_mod, b_mod):
    depth, d, n = w_mod.shape
    r = cond.shape[0]
    tn = _pick_tile(n, 1536, LANES)
    return pl.pallas_call(
        _adaln_kernel,
        grid=(depth, n // tn),
        in_specs=[
            pl.BlockSpec((r, d), lambda l, j: (0, 0)),
            pl.BlockSpec((None, d, tn), lambda l, j: (l, 0, j)),
            pl.BlockSpec((None, 1, tn), lambda l, j: (l, 0, j)),
        ],
        out_specs=pl.BlockSpec((None, r, tn), lambda l, j: (l, 0, j)),
        out_shape=jax.ShapeDtypeStruct((depth, r, n), F32),
        compiler_params=_params("parallel", "parallel"),
        name="adaln",
    )(cond, w_mod, b_mod.reshape(depth, 1, n))


def _norm_matmul_kernel(x_ref, sh_ref, sc_ref, w_ref, o_ref, h_ref):
    @pl.when(pl.program_id(1) == 0)
    def _():
        h_ref[...] = _modulated_norm(x_ref[...], sh_ref[...], sc_ref[...]).astype(BF16)

    o_ref[...] = jnp.dot(h_ref[...], w_ref[...], preferred_element_type=F32).astype(o_ref.dtype)


def _norm_matmul(x2d, shift, scale, w, rows_per_mod):
    m, d = x2d.shape
    n = w.shape[1]
    tm = _pick_tile(rows_per_mod, 1024, 8)
    tn = _pick_tile(n, 1024, LANES)
    tiles_per_mod = rows_per_mod // tm
    mod_spec = pl.BlockSpec((None, 1, d), lambda i, j: (i // tiles_per_mod, 0, 0))
    return pl.pallas_call(
        _norm_matmul_kernel,
        grid=(m // tm, n // tn),
        in_specs=[
            pl.BlockSpec((tm, d), lambda i, j: (i, 0)),
            mod_spec,
            mod_spec,
            pl.BlockSpec((d, tn), lambda i, j: (0, j)),
        ],
        out_specs=pl.BlockSpec((tm, tn), lambda i, j: (i, j)),
        out_shape=jax.ShapeDtypeStruct((m, n), BF16),
        scratch_shapes=[pltpu.VMEM((tm, d), BF16)],
        compiler_params=_params("parallel", "arbitrary"),
        name="norm_in_proj",
    )(x2d, shift, scale, w)


def _scan_tile_index(i, n_ctx_tiles, n_tiles, reverse):
    if not reverse:
        return i
    return jnp.where(i < n_ctx_tiles, n_ctx_tiles - 1 - i, n_tiles - 1 - (i - n_ctx_tiles))


def _rnn_kernel(xp_ref, x_ref, xn_ref, cw_ref, cb_ref, wg_ref, bg_ref, lam_ref, *rest,
                n_ctx_tiles, n_tiles, reverse, add_prev):
    if add_prev:
        prev_ref, out_ref, a_scr, u_scr, h_scr = rest
    else:
        out_ref, a_scr, u_scr, h_scr = rest
    tl, nb, c = x_ref.shape
    i = pl.program_id(0)
    ti = _scan_tile_index(i, n_ctx_tiles, n_tiles, reverse)

    @pl.when(i == 0)
    def _():
        h_scr[...] = jnp.zeros_like(h_scr)

    at_start = jnp.logical_or(ti == 0, ti == n_ctx_tiles)
    at_end = jnp.logical_or(ti == n_ctx_tiles - 1, ti == n_tiles - 1)
    xp = jnp.where(at_start, 0.0, xp_ref[...].astype(F32))
    xn = jnp.where(at_end, 0.0, xn_ref[...].astype(F32))
    xe = jnp.concatenate([xp, x_ref[...].astype(F32), xn], axis=0)
    xc = cb_ref[...] + xe[0:tl] * cw_ref[0:1, :]
    for k in range(1, CONV_TAPS):
        xc = xc + xe[k:k + tl] * cw_ref[k:k + 1, :]
    xc = xc.reshape(tl * nb, c)

    lam = lam_ref[...]
    log_sig = jnp.minimum(lam, 0.0) - jnp.log(1.0 + jnp.exp(-jnp.abs(lam)))
    c8 = LRU_C * log_sig
    for j in range(c // MXU_DIM):
        sl = slice(MXU_DIM * j, MXU_DIM * (j + 1))
        xj = xc[:, sl]
        g = jnp.dot(xj.astype(BF16), wg_ref[j], preferred_element_type=F32) + bg_ref[j]
        r = _sigmoid(g[:, :MXU_DIM])
        ig = _sigmoid(g[:, MXU_DIM:])
        a = jnp.exp(c8[:, sl] * r)
        a_scr[:, sl] = a
        u_scr[:, sl] = jnp.sqrt(1.0 - a * a) * (ig * xj)

    def step(s, h):
        tt = (tl - 1 - s) if reverse else s
        row = pl.multiple_of(tt * nb, nb)
        h = a_scr[pl.ds(row, nb), :] * h + u_scr[pl.ds(row, nb), :]
        val = h + prev_ref[tt].astype(F32) if add_prev else h
        out_ref[tt] = val.astype(out_ref.dtype)
        return h

    h_scr[...] = lax.fori_loop(0, tl, step, h_scr[...], unroll=4)


def _rnn_scan(xr, conv_w, conv_b, wg, bg, lam, n_ctx, reverse, prev):
    t, nb, c = xr.shape
    tl = _pick_tile(n_ctx, 64, 2)
    assert t % tl == 0 and n_ctx % tl == 0
    n_tiles, n_ctx_tiles = t // tl, n_ctx // tl
    tile = functools.partial(_scan_tile_index, n_ctx_tiles=n_ctx_tiles, n_tiles=n_tiles, reverse=reverse)
    half = tl // CONV_LEFT
    full = lambda shape: pl.BlockSpec(shape, lambda i: (0,) * len(shape))
    in_specs = [
        pl.BlockSpec((CONV_LEFT, nb, c), lambda i: (jnp.maximum(tile(i) * half - 1, 0), 0, 0)),
        pl.BlockSpec((tl, nb, c), lambda i: (tile(i), 0, 0)),
        pl.BlockSpec((1, nb, c), lambda i: (jnp.minimum((tile(i) + 1) * tl, t - 1), 0, 0)),
        full((CONV_TAPS, c)),
        full((1, c)),
        full(wg.shape),
        full(bg.shape),
        full((1, c)),
    ]
    args = [xr, xr, xr, conv_w, conv_b.reshape(1, c), wg, bg, lam.reshape(1, c)]
    if prev is not None:
        in_specs.append(pl.BlockSpec((tl, nb, c), lambda i: (tile(i), 0, 0)))
        args.append(prev)
    kern = functools.partial(_rnn_kernel, n_ctx_tiles=n_ctx_tiles, n_tiles=n_tiles, reverse=reverse,
                             add_prev=prev is not None)
    return pl.pallas_call(
        kern,
        grid=(n_tiles,),
        in_specs=in_specs,
        out_specs=pl.BlockSpec((tl, nb, c), lambda i: (tile(i), 0, 0)),
        out_shape=jax.ShapeDtypeStruct((t, nb, c), BF16),
        scratch_shapes=[pltpu.VMEM((tl * nb, c), F32), pltpu.VMEM((tl * nb, c), F32), pltpu.VMEM((nb, c), F32)],
        compiler_params=_params("arbitrary"),
        name="rglru_scan_rev" if reverse else "rglru_scan_fwd",
    )(*args)


def _gate_weights(rg_w_d, rg_b_d):
    _, nblk, bw, _ = rg_w_d.shape
    per = MXU_DIM // bw
    ngrp = nblk // per
    w = rg_w_d.reshape(2, ngrp, per, bw, bw)
    eye = jnp.eye(per, dtype=rg_w_d.dtype)
    dense = w[:, :, :, :, None, :] * eye[None, None, :, None, :, None]
    dense = dense.reshape(2, ngrp, MXU_DIM, MXU_DIM)
    wg = jnp.concatenate([dense[0], dense[1]], axis=-1).astype(BF16)
    b = rg_b_d.reshape(2, ngrp, 1, MXU_DIM)
    bg = jnp.concatenate([b[0], b[1]], axis=-1)
    return wg, bg


def _head_norm(x, gain, in_a):
    sq = x * x
    sa = jnp.sum(jnp.where(in_a, sq, 0.0), axis=-1, keepdims=True)
    sb = jnp.sum(jnp.where(in_a, 0.0, sq), axis=-1, keepdims=True)
    ms = jnp.where(in_a, sa, sb) * (1.0 / HEAD_DIM)
    return x * lax.rsqrt(ms + EPS) * gain


def _dot_nt(a, b):
    return lax.dot_general(a, b, (((1,), (1,)), ((), ())), preferred_element_type=F32)


def _na_kernel(q_ref, k_ref, v_ref, kc_ref, vc_ref, qg_ref, kg_ref, bias_ref, o_ref,
               qa_scr, qb_scr, kn_scr, kcn_scr, *, rows):
    s = q_ref.shape[0]
    gq = NA_GROUP_ROWS * GRID_W
    gk = NA_KEY_ROWS * GRID_W
    in_a = lax.broadcasted_iota(jnp.int32, (1, LANES), 1) < HEAD_DIM

    def prep(ci, carry):
        r0 = pl.multiple_of(ci * gq, gq)
        qn = _head_norm(q_ref[pl.ds(r0, gq), :].astype(F32), qg_ref[...], in_a) * (HEAD_DIM ** -0.5)
        qa_scr[pl.ds(r0, gq), :] = jnp.where(in_a, qn, 0.0).astype(BF16)
        qb_scr[pl.ds(r0, gq), :] = jnp.where(in_a, 0.0, qn).astype(BF16)
        kn_scr[pl.ds(r0, gq), :] = _head_norm(k_ref[pl.ds(r0, gq), :].astype(F32), kg_ref[...], in_a).astype(BF16)
        return carry

    lax.fori_loop(0, s // gq, prep, 0)
    kcn_scr[...] = _head_norm(kc_ref[...].astype(F32), kg_ref[...], in_a).astype(BF16)

    ngrp = rows // NA_GROUP_ROWS

    def group(g, carry):
        kb = jnp.clip(NA_GROUP_ROWS * g - WIN_H // 2, 0, rows - NA_KEY_ROWS)
        cls = jnp.where(g == 0, 0, jnp.where(g == ngrp - 1, 2, 1))
        q0 = pl.multiple_of(g * gq, gq)
        k0 = pl.multiple_of(kb * GRID_W, GRID_W)
        kblk = kn_scr[pl.ds(k0, gk), :]
        vblk = v_ref[pl.ds(k0, gk), :]
        outs = []
        for hh, q_scr in enumerate((qa_scr, qb_scr)):
            qh = q_scr[pl.ds(q0, gq), :]
            s_lat = _dot_nt(qh, kblk) + bias_ref[hh, cls]
            s_ctx = _dot_nt(qh, kcn_scr[...])
            m = jnp.maximum(jnp.max(s_lat, axis=-1, keepdims=True), jnp.max(s_ctx, axis=-1, keepdims=True))
            p_lat = jnp.exp(s_lat - m)
            p_ctx = jnp.exp(s_ctx - m)
            den = jnp.sum(p_lat, axis=-1, keepdims=True) + jnp.sum(p_ctx, axis=-1, keepdims=True)
            o = (jnp.dot(p_lat.astype(BF16), vblk, preferred_element_type=F32)
                 + jnp.dot(p_ctx.astype(BF16), vc_ref[...], preferred_element_type=F32))
            outs.append(o / den)
        o_ref[pl.ds(q0, gq), :] = jnp.where(in_a, outs[0], outs[1]).astype(o_ref.dtype)
        return carry

    lax.fori_loop(0, ngrp, group, 0)


def _na_bias_table(rpb, rows):
    nh = rpb.shape[0]
    ngrp = rows // NA_GROUP_ROWS
    a = np.arange(NA_GROUP_ROWS)[:, None, None, None]
    qc = np.arange(GRID_W)[None, :, None, None]
    cr = np.arange(NA_KEY_ROWS)[None, None, :, None]
    kc = np.arange(GRID_W)[None, None, None, :]
    shape = (NA_GROUP_ROWS, GRID_W, NA_KEY_ROWS, GRID_W)
    tabs = []
    for g in (0, 1, ngrp - 1):
        kb = int(np.clip(NA_GROUP_ROWS * g - WIN_H // 2, 0, rows - NA_KEY_ROWS))
        r = NA_GROUP_ROWS * g + a
        key_row = kb + cr
        start = np.clip(r - WIN_H // 2, 0, rows - WIN_H)
        wstart = np.clip(qc - WIN_W // 2, 0, GRID_W - WIN_W)
        valid = (key_row >= start) & (key_row < start + WIN_H) & (kc >= wstart) & (kc < wstart + WIN_W)
        ro = np.broadcast_to(np.clip(key_row - r + WIN_H - 1, 0, 2 * WIN_H - 2), shape)
        co = np.broadcast_to(np.clip(kc - qc + WIN_W - 1, 0, 2 * WIN_W - 2), shape)
        tab = jnp.where(np.broadcast_to(valid, shape), rpb[:, ro, co].astype(F32), MASK_VALUE)
        tabs.append(tab.reshape(nh, NA_GROUP_ROWS * GRID_W, NA_KEY_ROWS * GRID_W))
    return jnp.stack(tabs, axis=1)


def _neighbourhood_attention(z_lat, z_ctx, q_gain, k_gain, bias, seq):
    nb, s, _ = z_lat.shape
    cl = z_ctx.shape[1]
    d_att = bias.shape[0] * HEAD_DIM
    npair = d_att // LANES
    cb = d_att // LANES
    rows = s // GRID_W
    assert s % GRID_W == 0 and rows % NA_GROUP_ROWS == 0 and rows >= NA_KEY_ROWS + NA_GROUP_ROWS
    gain2 = lambda g: jnp.concatenate([g, g]).reshape(1, LANES).astype(F32)
    lat = lambda seg: pl.BlockSpec((None, s, LANES), lambda p, b: (b, 0, seg * cb + p))
    ctx = lambda seg: pl.BlockSpec((None, cl, LANES), lambda p, b: (b, 0, seg * cb + p))
    vec = pl.BlockSpec((1, LANES), lambda p, b: (0, 0))
    return pl.pallas_call(
        functools.partial(_na_kernel, rows=rows),
        grid=(npair, nb),
        in_specs=[lat(4), lat(1), lat(2), ctx(1), ctx(2), vec, vec,
                  pl.BlockSpec((2,) + bias.shape[1:], lambda p, b: (p, 0, 0, 0))],
        out_specs=pl.BlockSpec((None, s, LANES), lambda p, b: (b, 0, p)),
        out_shape=jax.ShapeDtypeStruct((nb, s, d_att), BF16),
        scratch_shapes=[pltpu.VMEM((s, LANES), BF16), pltpu.VMEM((s, LANES), BF16),
                        pltpu.VMEM((s, LANES), BF16), pltpu.VMEM((cl, LANES), BF16)],
        compiler_params=_params("parallel", "parallel"),
        name="neighbourhood_attention",
    )(z_lat, z_lat, z_lat, z_ctx, z_ctx, gain2(q_gain), gain2(k_gain), bias)


def _ctx_attn_kernel(q_ref, k_ref, v_ref, qg_ref, kg_ref, o_ref):
    in_a = lax.broadcasted_iota(jnp.int32, (1, LANES), 1) < HEAD_DIM
    qn = _head_norm(q_ref[...].astype(F32), qg_ref[...], in_a) * (HEAD_DIM ** -0.5)
    kn = _head_norm(k_ref[...].astype(F32), kg_ref[...], in_a).astype(BF16)
    outs = []
    for qh in (jnp.where(in_a, qn, 0.0), jnp.where(in_a, 0.0, qn)):
        sc = _dot_nt(qh.astype(BF16), kn)
        p = jnp.exp(sc - jnp.max(sc, axis=-1, keepdims=True))
        den = jnp.sum(p, axis=-1, keepdims=True)
        outs.append(jnp.dot(p.astype(BF16), v_ref[...], preferred_element_type=F32) / den)
    o_ref[...] = jnp.where(in_a, outs[0], outs[1]).astype(o_ref.dtype)


def _context_attention(z_ctx, q_gain, k_gain, d_att):
    nb, cl, _ = z_ctx.shape
    cb = d_att // LANES
    gain2 = lambda g: jnp.concatenate([g, g]).reshape(1, LANES).astype(F32)
    ctx = lambda seg: pl.BlockSpec((None, cl, LANES), lambda p, b: (b, 0, seg * cb + p))
    vec = pl.BlockSpec((1, LANES), lambda p, b: (0, 0))
    return pl.pallas_call(
        _ctx_attn_kernel,
        grid=(cb, nb),
        in_specs=[ctx(4), ctx(1), ctx(2), vec, vec],
        out_specs=pl.BlockSpec((None, cl, LANES), lambda p, b: (b, 0, p)),
        out_shape=jax.ShapeDtypeStruct((nb, cl, d_att), BF16),
        compiler_params=_params("parallel", "parallel"),
        name="context_attention",
    )(z_ctx, z_ctx, z_ctx, gain2(q_gain), gain2(k_gain))


def _merge_kernel(x_ref, g_ref, hs_ref, y_ref, na_ref, gr_ref, gn_ref, wr_ref, wn_ref, wo_ref, o_ref):
    y_rnn = (hs_ref[...].astype(F32) * _gelu_tanh(y_ref[...].astype(F32))).astype(BF16)
    t_rnn = jnp.dot(y_rnn, wr_ref[...], preferred_element_type=F32)
    t_na = jnp.dot(na_ref[...], wn_ref[...], preferred_element_type=F32)
    mix = _sigmoid(gr_ref[...].astype(F32)) * t_rnn + _sigmoid(gn_ref[...].astype(F32)) * t_na
    out = jnp.dot(mix.astype(BF16), wo_ref[...], preferred_element_type=F32)
    o_ref[...] = x_ref[...] + g_ref[...] * out


def _merge(x2d, gate, hs, z, na, w_rnn_o, w_na_o, w_out, rows_per_mod):
    m, d = x2d.shape
    c = hs.shape[1]
    da = na.shape[1]
    tm = _pick_tile(rows_per_mod, 512, 8)
    tiles_per_mod = rows_per_mod // tm
    y_blk = (c + 2 * da) // c
    gr_blk = (2 * c + 3 * da) // d
    gn_blk = gr_blk + 1
    row = lambda width, blk=0: pl.BlockSpec((tm, width), lambda i: (i, blk))
    whole = lambda w: pl.BlockSpec(w.shape, lambda i: (0, 0))
    return pl.pallas_call(
        _merge_kernel,
        grid=(m // tm,),
        in_specs=[row(d), pl.BlockSpec((None, 1, d), lambda i: (i // tiles_per_mod, 0, 0)),
                  row(c), row(c, y_blk), row(da), row(d, gr_blk), row(d, gn_blk),
                  whole(w_rnn_o), whole(w_na_o), whole(w_out)],
        out_specs=row(d),
        out_shape=jax.ShapeDtypeStruct((m, d), F32),
        compiler_params=_params("parallel"),
        name="merge_out_proj",
    )(x2d, gate, hs, z, na, z, z, w_rnn_o, w_na_o, w_out)


def _ffn_kernel(x_ref, sh_ref, sc_ref, g_ref, w1_ref, w3_ref, w2_ref, o_ref, h_scr, acc_scr):
    f = pl.program_id(1)

    @pl.when(f == 0)
    def _():
        h_scr[...] = _modulated_norm(x_ref[...], sh_ref[...], sc_ref[...]).astype(BF16)
        acc_scr[...] = jnp.zeros_like(acc_scr)

    h = h_scr[...]
    a = jnp.dot(h, w1_ref[...], preferred_element_type=F32)
    b = jnp.dot(h, w3_ref[...], preferred_element_type=F32)
    acc_scr[...] += jnp.dot((_silu(a) * b).astype(BF16), w2_ref[...], preferred_element_type=F32)

    @pl.when(f == pl.num_programs(1) - 1)
    def _():
        o_ref[...] = x_ref[...] + g_ref[...] * acc_scr[...]


def _dense_ffn(x2d, shift, scale, gate, w1, w3, w2, rows_per_mod):
    m, d = x2d.shape
    dff = w1.shape[1]
    tm = _pick_tile(rows_per_mod, 512, 8)
    tf = _pick_tile(dff, 1536, LANES)
    tiles_per_mod = rows_per_mod // tm
    mod = pl.BlockSpec((None, 1, d), lambda i, f: (i // tiles_per_mod, 0, 0))
    return pl.pallas_call(
        _ffn_kernel,
        grid=(m // tm, dff // tf),
        in_specs=[pl.BlockSpec((tm, d), lambda i, f: (i, 0)), mod, mod, mod,
                  pl.BlockSpec((d, tf), lambda i, f: (0, f)),
                  pl.BlockSpec((d, tf), lambda i, f: (0, f)),
                  pl.BlockSpec((tf, d), lambda i, f: (f, 0))],
        out_specs=pl.BlockSpec((tm, d), lambda i, f: (i, 0)),
        out_shape=jax.ShapeDtypeStruct((m, d), F32),
        scratch_shapes=[pltpu.VMEM((tm, d), BF16), pltpu.VMEM((tm, d), F32)],
        compiler_params=_params("parallel", "arbitrary"),
        name="dense_swiglu",
    )(x2d, shift, scale, gate, w1, w3, w2)


def _route_kernel(x_ref, sh_ref, sc_ref, wr_ref, wrl_ref, h_ref, r_ref, *, n_experts):
    h = _modulated_norm(x_ref[...], sh_ref[...], sc_ref[...])
    h_ref[...] = h
    h_hi = h.astype(BF16)
    h_lo = (h - h_hi.astype(F32)).astype(BF16)
    logits = (jnp.dot(h_hi, wr_ref[...], preferred_element_type=F32)
              + jnp.dot(h_lo, wr_ref[...], preferred_element_type=F32)
              + jnp.dot(h_hi, wrl_ref[...], preferred_element_type=F32))
    lane = lax.broadcasted_iota(jnp.int32, logits.shape, 1).astype(F32)
    neg = -jnp.inf
    lg = jnp.where(lane < n_experts, logits, neg)
    m1 = jnp.max(lg, axis=-1, keepdims=True)
    i1 = jnp.min(jnp.where(lg == m1, lane, float(LANES)), axis=-1, keepdims=True)
    lg2 = jnp.where(lane == i1, neg, lg)
    m2 = jnp.max(lg2, axis=-1, keepdims=True)
    i2 = jnp.min(jnp.where(lg2 == m2, lane, float(LANES)), axis=-1, keepdims=True)
    e = jnp.exp(m2 - m1)
    w1 = 1.0 / (1.0 + e)
    w2 = e / (1.0 + e)
    r_ref[...] = jnp.where(lane == 0, i1, jnp.where(lane == 1, i2, jnp.where(lane == 2, w1,
                           jnp.where(lane == 3, w2, 0.0))))


def _route(x2d, shift, scale, router, rows_per_mod):
    m, d = x2d.shape
    n_experts = router.shape[1]
    wr32 = jnp.zeros((d, LANES), F32).at[:, :n_experts].set(router)
    wr = wr32.astype(BF16)
    wrl = (wr32 - wr.astype(F32)).astype(BF16)
    tm = _pick_tile(rows_per_mod, 512, 8)
    tiles_per_mod = rows_per_mod // tm
    mod = pl.BlockSpec((None, 1, d), lambda i: (i // tiles_per_mod, 0, 0))
    rspec = pl.BlockSpec((d, LANES), lambda i: (0, 0))
    return pl.pallas_call(
        functools.partial(_route_kernel, n_experts=n_experts),
        grid=(m // tm,),
        in_specs=[pl.BlockSpec((tm, d), lambda i: (i, 0)), mod, mod, rspec, rspec],
        out_specs=[pl.BlockSpec((tm, d), lambda i: (i, 0)), pl.BlockSpec((tm, LANES), lambda i: (i, 0))],
        out_shape=[jax.ShapeDtypeStruct((m, d), F32), jax.ShapeDtypeStruct((m, LANES), F32)],
        compiler_params=_params("parallel"),
        name="moe_route",
    )(x2d, shift, scale, wr, wrl)


GATHER_CHUNK = 2048


def _gather_kernel(idx_ref, src_ref, dst_ref, sem):
    base = pl.program_id(0) * GATHER_CHUNK

    def row_copy(src_row, dst_row):
        return pltpu.make_async_copy(src_ref.at[pl.ds(src_row, 1), :], dst_ref.at[pl.ds(dst_row, 1), :], sem)

    def start(i, carry):
        row_copy(idx_ref[base + i], base + i).start()
        return carry

    lax.fori_loop(0, GATHER_CHUNK, start, 0, unroll=8)

    def wait(i, carry):
        row_copy(0, base + i).wait()
        return carry

    lax.fori_loop(0, GATHER_CHUNK, wait, 0, unroll=8)


def _row_gather(src, idx):
    n = idx.shape[0]
    assert n % GATHER_CHUNK == 0
    return pl.pallas_call(
        _gather_kernel,
        grid_spec=pltpu.PrefetchScalarGridSpec(
            num_scalar_prefetch=1,
            grid=(n // GATHER_CHUNK,),
            in_specs=[pl.BlockSpec(memory_space=pl.ANY)],
            out_specs=pl.BlockSpec(memory_space=pl.ANY),
            scratch_shapes=[pltpu.SemaphoreType.DMA(())],
        ),
        out_shape=jax.ShapeDtypeStruct((n, src.shape[1]), src.dtype),
        compiler_params=pltpu.CompilerParams(dimension_semantics=("arbitrary",), has_side_effects=True),
        name="row_gather",
    )(idx, src)


def _expert_ffn_kernel(te_ref, nv_ref, x_ref, w1_ref, w3_ref, w2_ref, o_ref, h_scr, acc_scr):
    t = pl.program_id(0)
    f = pl.program_id(1)

    @pl.when(t < nv_ref[0])
    def _():
        @pl.when(f == 0)
        def _():
            h_scr[...] = x_ref[...].astype(BF16)
            acc_scr[...] = jnp.zeros_like(acc_scr)

        h = h_scr[...]
        a = jnp.dot(h, w1_ref[...], preferred_element_type=F32)
        b = jnp.dot(h, w3_ref[...], preferred_element_type=F32)
        acc_scr[...] += jnp.dot((_silu(a) * b).astype(BF16), w2_ref[...], preferred_element_type=F32)

        @pl.when(f == pl.num_programs(1) - 1)
        def _():
            o_ref[...] = acc_scr[...]

    @pl.when(jnp.logical_and(t >= nv_ref[0], f == pl.num_programs(1) - 1))
    def _():
        o_ref[...] = jnp.zeros_like(o_ref)


def _expert_ffn(xs, tile_expert, n_valid, w1, w3, w2, tm):
    n, d = xs.shape
    dfe = w1.shape[2]
    tf = _pick_tile(dfe, 512, LANES)
    nf = dfe // tf

    def fsel(t, f, nv):
        return jnp.where(t < nv[0], f, nf - 1)

    return pl.pallas_call(
        _expert_ffn_kernel,
        grid_spec=pltpu.PrefetchScalarGridSpec(
            num_scalar_prefetch=2,
            grid=(n // tm, nf),
            in_specs=[pl.BlockSpec((tm, d), lambda t, f, te, nv: (t, 0)),
                      pl.BlockSpec((None, d, tf), lambda t, f, te, nv: (te[t], 0, fsel(t, f, nv))),
                      pl.BlockSpec((None, d, tf), lambda t, f, te, nv: (te[t], 0, fsel(t, f, nv))),
                      pl.BlockSpec((None, tf, d), lambda t, f, te, nv: (te[t], fsel(t, f, nv), 0))],
            out_specs=pl.BlockSpec((tm, d), lambda t, f, te, nv: (t, 0)),
            scratch_shapes=[pltpu.VMEM((tm, d), BF16), pltpu.VMEM((tm, d), F32)],
        ),
        out_shape=jax.ShapeDtypeStruct((n, d), F32),
        compiler_params=_params("arbitrary", "arbitrary"),
        name="expert_swiglu",
    )(tile_expert, n_valid, xs, w1, w3, w2)


def _combine_kernel(x_ref, g_ref, r_ref, y_ref, o_ref):
    r = r_ref[...]
    mix = r[:, 2:3] * y_ref[0] + r[:, 3:4] * y_ref[1]
    o_ref[...] = x_ref[...] + g_ref[...] * mix


def _combine(x2d, gate, route, yg, rows_per_mod):
    m, d = x2d.shape
    tm = _pick_tile(rows_per_mod, 512, 8)
    tiles_per_mod = rows_per_mod // tm
    return pl.pallas_call(
        _combine_kernel,
        grid=(m // tm,),
        in_specs=[pl.BlockSpec((tm, d), lambda i: (i, 0)),
                  pl.BlockSpec((None, 1, d), lambda i: (i // tiles_per_mod, 0, 0)),
                  pl.BlockSpec((tm, LANES), lambda i: (i, 0)),
                  pl.BlockSpec((TOP_K, tm, d), lambda i: (0, i, 0))],
        out_specs=pl.BlockSpec((tm, d), lambda i: (i, 0)),
        out_shape=jax.ShapeDtypeStruct((m, d), F32),
        compiler_params=_params("parallel"),
        name="moe_combine",
    )(x2d, gate, route, yg)


def _moe_ffn(x2d, shift, scale, gate, router, w1, w3, w2, rows_per_mod):
    m, d = x2d.shape
    n_experts = router.shape[1]
    h, route = _route(x2d, shift, scale, router, rows_per_mod)
    tm = 1024
    expert = route[:, :TOP_K].astype(jnp.int32).T.reshape(-1)
    onehot = (expert[:, None] == jnp.arange(n_experts)[None, :]).astype(jnp.int32)
    csum = jnp.cumsum(onehot, axis=0)
    rank = jnp.sum(onehot * (csum - 1), axis=1)
    tiles = (csum[-1] + tm - 1) // tm
    tile_end = jnp.cumsum(tiles)
    pos = ((tile_end - tiles) * tm)[expert] + rank
    n_tiles = (TOP_K * m) // tm + n_experts
    src = jnp.zeros((n_tiles * tm,), jnp.int32).at[pos].set(jnp.arange(TOP_K * m, dtype=jnp.int32) % m)
    tile_expert = jnp.minimum(jnp.searchsorted(tile_end, jnp.arange(n_tiles), side="right"),
                              n_experts - 1).astype(jnp.int32)
    n_valid = tile_end[-1:].astype(jnp.int32)
    xs = _row_gather(h, src)
    ys = _expert_ffn(xs, tile_expert, n_valid, w1, w3, w2, tm)
    yg = _row_gather(ys, pos.astype(jnp.int32)).reshape(TOP_K, m, d)
    return _combine(x2d, gate, route, yg, rows_per_mod)


def kernel(x, c, ctx, c_ctx, w_mod, b_mod, w_in, conv_w, conv_b, rg_lambda, rg_w, rg_b, q_gain, k_gain, rpb,
           w_rnn_o, w_na_o, w_out, ffn_w1, ffn_w3, ffn_w2, router, moe_w1, moe_w3, moe_w2):
    nb, seq, d = x.shape
    cl = ctx.shape[1]
    depth = w_mod.shape[0]
    c_rnn = conv_w.shape[2]
    d_att = rpb.shape[1] * HEAD_DIM
    ctx_cols = c_rnn + 2 * d_att
    rows = seq // GRID_W

    n_cond = -(-(nb + 1) // 8) * 8
    cond = jnp.zeros((n_cond, d), F32).at[:nb].set(c).at[nb].set(c_ctx)
    mods = _adaln(cond, w_mod, b_mod)

    x2 = x.reshape(nb * seq, d)
    xc2 = ctx.reshape(nb * cl, d)
    for l in range(depth):
        ctx_out = l < depth - 1
        lat = [mods[l, :nb, k * d:(k + 1) * d].reshape(nb, 1, d) for k in range(N_MOD)]
        cmod = [mods[l, nb:nb + 1, k * d:(k + 1) * d].reshape(1, 1, d) for k in range(N_MOD)]
        w_in_l = w_in[l].astype(BF16)

        z_lat = _norm_matmul(x2, lat[0], lat[1], w_in_l, seq)
        z_ctx = _norm_matmul(xc2, cmod[0], cmod[1], w_in_l if ctx_out else w_in_l[:, :ctx_cols], nb * cl)
        ncl, ncc = z_lat.shape[1], z_ctx.shape[1]
        z_lat3 = z_lat.reshape(nb, seq, ncl)
        z_ctx3 = z_ctx.reshape(nb, cl, ncc)

        xr = jnp.concatenate([jnp.transpose(z_ctx3[:, :, :c_rnn], (1, 0, 2)),
                              jnp.transpose(z_lat3[:, :, :c_rnn], (1, 0, 2))], axis=0)
        wg_f, bg_f = _gate_weights(rg_w[l, 0], rg_b[l, 0])
        wg_r, bg_r = _gate_weights(rg_w[l, 1], rg_b[l, 1])
        hs_r = _rnn_scan(xr, conv_w[l], conv_b[l], wg_r, bg_r, rg_lambda[l, 1], cl, True, None)
        hs = _rnn_scan(xr, conv_w[l], conv_b[l], wg_f, bg_f, rg_lambda[l, 0], cl, False, hs_r)
        hs_lat = jnp.transpose(hs[cl:], (1, 0, 2)).reshape(nb * seq, c_rnn)

        bias = _na_bias_table(rpb[l], rows)
        na_lat = _neighbourhood_attention(z_lat3, z_ctx3, q_gain[l], k_gain[l], bias, seq)

        wr, wn, wo = w_rnn_o[l].astype(BF16), w_na_o[l].astype(BF16), w_out[l].astype(BF16)
        x2 = _merge(x2, lat[2], hs_lat, z_lat, na_lat.reshape(nb * seq, d_att), wr, wn, wo, seq)
        if ctx_out:
            hs_ctx = jnp.transpose(hs[:cl], (1, 0, 2)).reshape(nb * cl, c_rnn)
            na_ctx = _context_attention(z_ctx3, q_gain[l], k_gain[l], d_att)
            xc2 = _merge(xc2, cmod[2], hs_ctx, z_ctx, na_ctx.reshape(nb * cl, d_att), wr, wn, wo, nb * cl)

        j = l // 2
        if l % 2 == 0:
            w1, w3, w2 = ffn_w1[j].astype(BF16), ffn_w3[j].astype(BF16), ffn_w2[j].astype(BF16)
            x2 = _dense_ffn(x2, lat[3], lat[4], lat[5], w1, w3, w2, seq)
            if ctx_out:
                xc2 = _dense_ffn(xc2, cmod[3], cmod[4], cmod[5], w1, w3, w2, nb * cl)
        else:
            w1, w3, w2 = moe_w1[j].astype(BF16), moe_w3[j].astype(BF16), moe_w2[j].astype(BF16)
            x2 = _moe_ffn(x2, lat[3], lat[4], lat[5], router[j], w1, w3, w2, seq)
            if ctx_out:
                xc2 = _moe_ffn(xc2, cmod[3], cmod[4], cmod[5], router[j], w1, w3, w2, nb * cl)
    return x2.reshape(nb, seq, d)
```

```python
import functools

import numpy as np
import jax
import jax.numpy as jnp
from jax import lax
from jax.experimental import pallas as pl
from jax.experimental.pallas import tpu as pltpu

F32 = jnp.float32
BF16 = jnp.bfloat16

EPS = 1e-6
N_MOD = 6
GRID_W = 64
WIN_H = 8
WIN_W = 16
HEAD_DIM = 64
RNN_BW = 64
CONV_TAPS = 4
CONV_LEFT = 2
LRU_C = 8.0
TOP_K = 2
MASK_VALUE = -1e30

LANES = 128
MXU_DIM = 256
VMEM_LIMIT_BYTES = 56 * 1024 * 1024

NA_GROUP_ROWS = 4
NA_KEY_ROWS = 12


def _params(*sem):
    return pltpu.CompilerParams(dimension_semantics=sem, vmem_limit_bytes=VMEM_LIMIT_BYTES)


def _sigmoid(x):
    return 0.5 * (jnp.tanh(0.5 * x) + 1.0)


def _silu(x):
    return x * _sigmoid(x)


def _gelu_tanh(x):
    return 0.5 * x * (1.0 + jnp.tanh(np.sqrt(2.0 / np.pi) * (x + 0.044715 * (x * x * x))))


def _modulated_norm(x, shift, scale):
    ms = jnp.mean(x * x, axis=-1, keepdims=True)
    return x * lax.rsqrt(ms + EPS) * (1.0 + scale) + shift


def _pick_tile(n, cap, mult):
    best = None
    for t in range(mult, min(n, cap) + 1, mult):
        if n % t == 0:
            best = t
    assert best is not None, (n, cap, mult)
    return best


def _adaln_kernel(c_ref, w_ref, b_ref, o_ref):
    s = _silu(c_ref[...]).astype(BF16)
    o_ref[...] = jnp.dot(s, w_ref[...].astype(BF16), preferred_element_type=F32) + b_ref[...]


def _adaln(cond, w_mod, b_mod):
    depth, d, n = w_mod.shape
    r = cond.shape[0]
    tn = _pick_tile(n, 1536, LANES)
    return pl.pallas_call(
        _adaln_kernel,
        grid=(depth, n // tn),
        in_specs=[
            pl.BlockSpec((r, d), lambda l, j: (0, 0)),
            pl.BlockSpec((None, d, tn), lambda l, j: (l, 0, j)),
            pl.BlockSpec((None, 1, tn), lambda l, j: (l, 0, j)),
        ],
        out_specs=pl.BlockSpec((None, r, tn), lambda l, j: (l, 0, j)),
        out_shape=jax.ShapeDtypeStruct((depth, r, n), F32),
        compiler_params=_params("parallel", "parallel"),
        name="adaln",
    )(cond, w_mod, b_mod.reshape(depth, 1, n))


def _norm_matmul_kernel(x_ref, sh_ref, sc_ref, w_ref, o_ref, h_ref):
    @pl.when(pl.program_id(1) == 0)
    def _():
        h_ref[...] = _modulated_norm(x_ref[...], sh_ref[...], sc_ref[...]).astype(BF16)

    o_ref[...] = jnp.dot(h_ref[...], w_ref[...], preferred_element_type=F32).astype(o_ref.dtype)


def _norm_matmul(x2d, shift, scale, w, rows_per_mod):
    m, d = x2d.shape
    n = w.shape[1]
    tm = _pick_tile(rows_per_mod, 1024, 8)
    tn = _pick_tile(n, 1024, LANES)
    tiles_per_mod = rows_per_mod // tm
    mod_spec = pl.BlockSpec((None, 1, d), lambda i, j: (i // tiles_per_mod, 0, 0))
    return pl.pallas_call(
        _norm_matmul_kernel,
        grid=(m // tm, n // tn),
        in_specs=[
            pl.BlockSpec((tm, d), lambda i, j: (i, 0)),
            mod_spec,
            mod_spec,
            pl.BlockSpec((d, tn), lambda i, j: (0, j)),
        ],
        out_specs=pl.BlockSpec((tm, tn), lambda i, j: (i, j)),
        out_shape=jax.ShapeDtypeStruct((m, n), BF16),
        scratch_shapes=[pltpu.VMEM((tm, d), BF16)],
        compiler_params=_params("parallel", "arbitrary"),
        name="norm_in_proj",
    )(x2d, shift, scale, w)


def _scan_tile_index(i, n_ctx_tiles, n_tiles, reverse):
    if not reverse:
        return i
    return jnp.where(i < n_ctx_tiles, n_ctx_tiles - 1 - i, n_tiles - 1 - (i - n_ctx_tiles))


def _rnn_kernel(xp_ref, x_ref, xn_ref, cw_ref, cb_ref, wg_ref, bg_ref, lam_ref, *rest,
                n_ctx_tiles, n_tiles, reverse, add_prev):
    if add_prev:
        prev_ref, out_ref, a_scr, u_scr, h_scr = rest
    else:
        out_ref, a_scr, u_scr, h_scr = rest
    tl, nb, c = x_ref.shape
    i = pl.program_id(0)
    ti = _scan_tile_index(i, n_ctx_tiles, n_tiles, reverse)

    @pl.when(i == 0)
    def _():
        h_scr[...] = jnp.zeros_like(h_scr)

    at_start = jnp.logical_or(ti == 0, ti == n_ctx_tiles)
    at_end = jnp.logical_or(ti == n_ctx_tiles - 1, ti == n_tiles - 1)
    xp = jnp.where(at_start, 0.0, xp_ref[...].astype(F32))
    xn = jnp.where(at_end, 0.0, xn_ref[...].astype(F32))
    xe = jnp.concatenate([xp, x_ref[...].astype(F32), xn], axis=0)
    xc = cb_ref[...] + xe[0:tl] * cw_ref[0:1, :]
    for k in range(1, CONV_TAPS):
        xc = xc + xe[k:k + tl] * cw_ref[k:k + 1, :]
    xc = xc.reshape(tl * nb, c)

    lam = lam_ref[...]
    log_sig = jnp.minimum(lam, 0.0) - jnp.log(1.0 + jnp.exp(-jnp.abs(lam)))
    c8 = LRU_C * log_sig
    for j in range(c // MXU_DIM):
        sl = slice(MXU_DIM * j, MXU_DIM * (j + 1))
        xj = xc[:, sl]
        g = jnp.dot(xj.astype(BF16), wg_ref[j], preferred_element_type=F32) + bg_ref[j]
        r = _sigmoid(g[:, :MXU_DIM])
        ig = _sigmoid(g[:, MXU_DIM:])
        a = jnp.exp(c8[:, sl] * r)
        a_scr[:, sl] = a
        u_scr[:, sl] = jnp.sqrt(1.0 - a * a) * (ig * xj)

    def step(s, h):
        tt = (tl - 1 - s) if reverse else s
        row = pl.multiple_of(tt * nb, nb)
        h = a_scr[pl.ds(row, nb), :] * h + u_scr[pl.ds(row, nb), :]
        val = h + prev_ref[tt].astype(F32) if add_prev else h
        out_ref[tt] = val.astype(out_ref.dtype)
        return h

    h_scr[...] = lax.fori_loop(0, tl, step, h_scr[...], unroll=4)


def _rnn_scan(xr, conv_w, conv_b, wg, bg, lam, n_ctx, reverse, prev):
    t, nb, c = xr.shape
    tl = _pick_tile(n_ctx, 64, 2)
    assert t % tl == 0 and n_ctx % tl == 0
    n_tiles, n_ctx_tiles = t // tl, n_ctx // tl
    tile = functools.partial(_scan_tile_index, n_ctx_tiles=n_ctx_tiles, n_tiles=n_tiles, reverse=reverse)
    half = tl // CONV_LEFT
    full = lambda shape: pl.BlockSpec(shape, lambda i: (0,) * len(shape))
    in_specs = [
        pl.BlockSpec((CONV_LEFT, nb, c), lambda i: (jnp.maximum(tile(i) * half - 1, 0), 0, 0)),
        pl.BlockSpec((tl, nb, c), lambda i: (tile(i), 0, 0)),
        pl.BlockSpec((1, nb, c), lambda i: (jnp.minimum((tile(i) + 1) * tl, t - 1), 0, 0)),
        full((CONV_TAPS, c)),
        full((1, c)),
        full(wg.shape),
        full(bg.shape),
        full((1, c)),
    ]
    args = [xr, xr, xr, conv_w, conv_b.reshape(1, c), wg, bg, lam.reshape(1, c)]
    if prev is not None:
        in_specs.append(pl.BlockSpec((tl, nb, c), lambda i: (tile(i), 0, 0)))
        args.append(prev)
    kern = functools.partial(_rnn_kernel, n_ctx_tiles=n_ctx_tiles, n_tiles=n_tiles, reverse=reverse,
                             add_prev=prev is not None)
    return pl.pallas_call(
        kern,
        grid=(n_tiles,),
        in_specs=in_specs,
        out_specs=pl.BlockSpec((tl, nb, c), lambda i: (tile(i), 0, 0)),
        out_shape=jax.ShapeDtypeStruct((t, nb, c), BF16),
        scratch_shapes=[pltpu.VMEM((tl * nb, c), F32), pltpu.VMEM((tl * nb, c), F32), pltpu.VMEM((nb, c), F32)],
        compiler_params=_params("arbitrary"),
        name="rglru_scan_rev" if reverse else "rglru_scan_fwd",
    )(*args)


def _gate_weights(rg_w_d, rg_b_d):
    _, nblk, bw, _ = rg_w_d.shape
    per = MXU_DIM // bw
    ngrp = nblk // per
    w = rg_w_d.reshape(2, ngrp, per, bw, bw)
    eye = jnp.eye(per, dtype=rg_w_d.dtype)
    dense = w[:, :, :, :, None, :] * eye[None, None, :, None, :, None]
    dense = dense.reshape(2, ngrp, MXU_DIM, MXU_DIM)
    wg = jnp.concatenate([dense[0], dense[1]], axis=-1).astype(BF16)
    b = rg_b_d.reshape(2, ngrp, 1, MXU_DIM)
    bg = jnp.concatenate([b[0], b[1]], axis=-1)
    return wg, bg


def _head_norm(x, gain, in_a):
    sq = x * x
    sa = jnp.sum(jnp.where(in_a, sq, 0.0), axis=-1, keepdims=True)
    sb = jnp.sum(jnp.where(in_a, 0.0, sq), axis=-1, keepdims=True)
    ms = jnp.where(in_a, sa, sb) * (1.0 / HEAD_DIM)
    return x * lax.rsqrt(ms + EPS) * gain


def _dot_nt(a, b):
    return lax.dot_general(a, b, (((1,), (1,)), ((), ())), preferred_element_type=F32)


def _na_kernel(q_ref, k_ref, v_ref, kc_ref, vc_ref, qg_ref, kg_ref, bias_ref, o_ref,
               qa_scr, qb_scr, kn_scr, kcn_scr, *, rows):
    s = q_ref.shape[0]
    gq = NA_GROUP_ROWS * GRID_W
    gk = NA_KEY_ROWS * GRID_W
    in_a = lax.broadcasted_iota(jnp.int32, (1, LANES), 1) < HEAD_DIM

    def prep(ci, carry):
        r0 = pl.multiple_of(ci * gq, gq)
        qn = _head_norm(q_ref[pl.ds(r0, gq), :].astype(F32), qg_ref[...], in_a) * (HEAD_DIM ** -0.5)
        qa_scr[pl.ds(r0, gq), :] = jnp.where(in_a, qn, 0.0).astype(BF16)
        qb_scr[pl.ds(r0, gq), :] = jnp.where(in_a, 0.0, qn).astype(BF16)
        kn_scr[pl.ds(r0, gq), :] = _head_norm(k_ref[pl.ds(r0, gq), :].astype(F32), kg_ref[...], in_a).astype(BF16)
        return carry

    lax.fori_loop(0, s // gq, prep, 0)
    kcn_scr[...] = _head_norm(kc_ref[...].astype(F32), kg_ref[...], in_a).astype(BF16)

    ngrp = rows // NA_GROUP_ROWS

    def group(g, carry):
        kb = jnp.clip(NA_GROUP_ROWS * g - WIN_H // 2, 0, rows - NA_KEY_ROWS)
        cls = jnp.where(g == 0, 0, jnp.where(g == ngrp - 1, 2, 1))
        q0 = pl.multiple_of(g * gq, gq)
        k0 = pl.multiple_of(kb * GRID_W, GRID_W)
        kblk = kn_scr[pl.ds(k0, gk), :]
        vblk = v_ref[pl.ds(k0, gk), :]
        outs = []
        for hh, q_scr in enumerate((qa_scr, qb_scr)):
            qh = q_scr[pl.ds(q0, gq), :]
            s_lat = _dot_nt(qh, kblk) + bias_ref[hh, cls]
            s_ctx = _dot_nt(qh, kcn_scr[...])
            m = jnp.maximum(jnp.max(s_lat, axis=-1, keepdims=True), jnp.max(s_ctx, axis=-1, keepdims=True))
            p_lat = jnp.exp(s_lat - m)
            p_ctx = jnp.exp(s_ctx - m)
            den = jnp.sum(p_lat, axis=-1, keepdims=True) + jnp.sum(p_ctx, axis=-1, keepdims=True)
            o = (jnp.dot(p_lat.astype(BF16), vblk, preferred_element_type=F32)
                 + jnp.dot(p_ctx.astype(BF16), vc_ref[...], preferred_element_type=F32))
            outs.append(o / den)
        o_ref[pl.ds(q0, gq), :] = jnp.where(in_a, outs[0], outs[1]).astype(o_ref.dtype)
        return carry

    lax.fori_loop(0, ngrp, group, 0)


def _na_bias_table(rpb, rows):
    nh, n_ro, n_co = rpb.shape
    ngrp = rows // NA_GROUP_ROWS
    qc = np.arange(GRID_W)[:, None]
    kc = np.arange(GRID_W)[None, :]
    wstart = np.clip(qc - WIN_W // 2, 0, GRID_W - WIN_W)
    col_valid = (kc >= wstart) & (kc < wstart + WIN_W)
    col_onehot = ((kc - qc + WIN_W - 1)[None] == np.arange(n_co)[:, None, None])
    a = np.arange(NA_GROUP_ROWS)[:, None]
    cr = np.arange(NA_KEY_ROWS)[None, :]
    row_onehot, row_valid = [], []
    for g in (0, 1, ngrp - 1):
        kb = int(np.clip(NA_GROUP_ROWS * g - WIN_H // 2, 0, rows - NA_KEY_ROWS))
        r = NA_GROUP_ROWS * g + a
        key_row = kb + cr
        start = np.clip(r - WIN_H // 2, 0, rows - WIN_H)
        valid = (key_row >= start) & (key_row < start + WIN_H)
        row_onehot.append(valid[..., None] & ((key_row - r + WIN_H - 1)[..., None] == np.arange(n_ro)))
        row_valid.append(valid)
    row_onehot = np.stack(row_onehot).astype(np.float32)
    valid = np.stack(row_valid)[:, :, None, :, None] & col_valid[None, None, :, None, :]
    by_col = jnp.einsum("hrj,jqk->hrqk", rpb.astype(F32), col_onehot.astype(np.float32),
                        precision=lax.Precision.HIGHEST)
    tab = jnp.einsum("hoqk,caro->hcaqrk", by_col, row_onehot, precision=lax.Precision.HIGHEST)
    tab = jnp.where(valid[None], tab, MASK_VALUE)
    return tab.reshape(nh, 3, NA_GROUP_ROWS * GRID_W, NA_KEY_ROWS * GRID_W)


def _neighbourhood_attention(z_lat, z_ctx, q_gain, k_gain, bias, seq):
    nb, s, _ = z_lat.shape
    cl = z_ctx.shape[1]
    d_att = bias.shape[0] * HEAD_DIM
    npair = d_att // LANES
    cb = d_att // LANES
    rows = s // GRID_W
    assert s % GRID_W == 0 and rows % NA_GROUP_ROWS == 0 and rows >= NA_KEY_ROWS + NA_GROUP_ROWS
    gain2 = lambda g: jnp.concatenate([g, g]).reshape(1, LANES).astype(F32)
    lat = lambda seg: pl.BlockSpec((None, s, LANES), lambda p, b: (b, 0, seg * cb + p))
    ctx = lambda seg: pl.BlockSpec((None, cl, LANES), lambda p, b: (b, 0, seg * cb + p))
    vec = pl.BlockSpec((1, LANES), lambda p, b: (0, 0))
    return pl.pallas_call(
        functools.partial(_na_kernel, rows=rows),
        grid=(npair, nb),
        in_specs=[lat(4), lat(1), lat(2), ctx(1), ctx(2), vec, vec,
                  pl.BlockSpec((2,) + bias.shape[1:], lambda p, b: (p, 0, 0, 0))],
        out_specs=pl.BlockSpec((None, s, LANES), lambda p, b: (b, 0, p)),
        out_shape=jax.ShapeDtypeStruct((nb, s, d_att), BF16),
        scratch_shapes=[pltpu.VMEM((s, LANES), BF16), pltpu.VMEM((s, LANES), BF16),
                        pltpu.VMEM((s, LANES), BF16), pltpu.VMEM((cl, LANES), BF16)],
        compiler_params=_params("parallel", "parallel"),
        name="neighbourhood_attention",
    )(z_lat, z_lat, z_lat, z_ctx, z_ctx, gain2(q_gain), gain2(k_gain), bias)


def _ctx_attn_kernel(q_ref, k_ref, v_ref, qg_ref, kg_ref, o_ref):
    in_a = lax.broadcasted_iota(jnp.int32, (1, LANES), 1) < HEAD_DIM
    qn = _head_norm(q_ref[...].astype(F32), qg_ref[...], in_a) * (HEAD_DIM ** -0.5)
    kn = _head_norm(k_ref[...].astype(F32), kg_ref[...], in_a).astype(BF16)
    outs = []
    for qh in (jnp.where(in_a, qn, 0.0), jnp.where(in_a, 0.0, qn)):
        sc = _dot_nt(qh.astype(BF16), kn)
        p = jnp.exp(sc - jnp.max(sc, axis=-1, keepdims=True))
        den = jnp.sum(p, axis=-1, keepdims=True)
        outs.append(jnp.dot(p.astype(BF16), v_ref[...], preferred_element_type=F32) / den)
    o_ref[...] = jnp.where(in_a, outs[0], outs[1]).astype(o_ref.dtype)


def _context_attention(z_ctx, q_gain, k_gain, d_att):
    nb, cl, _ = z_ctx.shape
    cb = d_att // LANES
    gain2 = lambda g: jnp.concatenate([g, g]).reshape(1, LANES).astype(F32)
    ctx = lambda seg: pl.BlockSpec((None, cl, LANES), lambda p, b: (b, 0, seg * cb + p))
    vec = pl.BlockSpec((1, LANES), lambda p, b: (0, 0))
    return pl.pallas_call(
        _ctx_attn_kernel,
        grid=(cb, nb),
        in_specs=[ctx(4), ctx(1), ctx(2), vec, vec],
        out_specs=pl.BlockSpec((None, cl, LANES), lambda p, b: (b, 0, p)),
        out_shape=jax.ShapeDtypeStruct((nb, cl, d_att), BF16),
        compiler_params=_params("parallel", "parallel"),
        name="context_attention",
    )(z_ctx, z_ctx, z_ctx, gain2(q_gain), gain2(k_gain))


def _merge_kernel(x_ref, g_ref, hs_ref, y_ref, na_ref, gr_ref, gn_ref, wr_ref, wn_ref, wo_ref, o_ref):
    y_rnn = (hs_ref[...].astype(F32) * _gelu_tanh(y_ref[...].astype(F32))).astype(BF16)
    t_rnn = jnp.dot(y_rnn, wr_ref[...], preferred_element_type=F32)
    t_na = jnp.dot(na_ref[...], wn_ref[...], preferred_element_type=F32)
    mix = _sigmoid(gr_ref[...].astype(F32)) * t_rnn + _sigmoid(gn_ref[...].astype(F32)) * t_na
    out = jnp.dot(mix.astype(BF16), wo_ref[...], preferred_element_type=F32)
    o_ref[...] = x_ref[...] + g_ref[...] * out


def _merge(x2d, gate, hs, z, na, w_rnn_o, w_na_o, w_out, rows_per_mod):
    m, d = x2d.shape
    c = hs.shape[1]
    da = na.shape[1]
    tm = _pick_tile(rows_per_mod, 512, 8)
    tiles_per_mod = rows_per_mod // tm
    y_blk = (c + 2 * da) // c
    gr_blk = (2 * c + 3 * da) // d
    gn_blk = gr_blk + 1
    row = lambda width, blk=0: pl.BlockSpec((tm, width), lambda i: (i, blk))
    whole = lambda w: pl.BlockSpec(w.shape, lambda i: (0, 0))
    return pl.pallas_call(
        _merge_kernel,
        grid=(m // tm,),
        in_specs=[row(d), pl.BlockSpec((None, 1, d), lambda i: (i // tiles_per_mod, 0, 0)),
                  row(c), row(c, y_blk), row(da), row(d, gr_blk), row(d, gn_blk),
                  whole(w_rnn_o), whole(w_na_o), whole(w_out)],
        out_specs=row(d),
        out_shape=jax.ShapeDtypeStruct((m, d), F32),
        compiler_params=_params("parallel"),
        name="merge_out_proj",
    )(x2d, gate, hs, z, na, z, z, w_rnn_o, w_na_o, w_out)


def _ffn_kernel(x_ref, sh_ref, sc_ref, g_ref, w1_ref, w3_ref, w2_ref, o_ref, h_scr, acc_scr):
    f = pl.program_id(1)

    @pl.when(f == 0)
    def _():
        h_scr[...] = _modulated_norm(x_ref[...], sh_ref[...], sc_ref[...]).astype(BF16)
        acc_scr[...] = jnp.zeros_like(acc_scr)

    h = h_scr[...]
    a = jnp.dot(h, w1_ref[...], preferred_element_type=F32)
    b = jnp.dot(h, w3_ref[...], preferred_element_type=F32)
    acc_scr[...] += jnp.dot((_silu(a) * b).astype(BF16), w2_ref[...], preferred_element_type=F32)

    @pl.when(f == pl.num_programs(1) - 1)
    def _():
        o_ref[...] = x_ref[...] + g_ref[...] * acc_scr[...]


def _dense_ffn(x2d, shift, scale, gate, w1, w3, w2, rows_per_mod):
    m, d = x2d.shape
    dff = w1.shape[1]
    tm = _pick_tile(rows_per_mod, 512, 8)
    tf = _pick_tile(dff, 1536, LANES)
    tiles_per_mod = rows_per_mod // tm
    mod = pl.BlockSpec((None, 1, d), lambda i, f: (i // tiles_per_mod, 0, 0))
    return pl.pallas_call(
        _ffn_kernel,
        grid=(m // tm, dff // tf),
        in_specs=[pl.BlockSpec((tm, d), lambda i, f: (i, 0)), mod, mod, mod,
                  pl.BlockSpec((d, tf), lambda i, f: (0, f)),
                  pl.BlockSpec((d, tf), lambda i, f: (0, f)),
                  pl.BlockSpec((tf, d), lambda i, f: (f, 0))],
        out_specs=pl.BlockSpec((tm, d), lambda i, f: (i, 0)),
        out_shape=jax.ShapeDtypeStruct((m, d), F32),
        scratch_shapes=[pltpu.VMEM((tm, d), BF16), pltpu.VMEM((tm, d), F32)],
        compiler_params=_params("parallel", "arbitrary"),
        name="dense_swiglu",
    )(x2d, shift, scale, gate, w1, w3, w2)


def _route_kernel(x_ref, sh_ref, sc_ref, wr_ref, wrl_ref, h_ref, r_ref, *, n_experts):
    h = _modulated_norm(x_ref[...], sh_ref[...], sc_ref[...])
    h_ref[...] = h
    h_hi = h.astype(BF16)
    h_lo = (h - h_hi.astype(F32)).astype(BF16)
    logits = (jnp.dot(h_hi, wr_ref[...], preferred_element_type=F32)
              + jnp.dot(h_lo, wr_ref[...], preferred_element_type=F32)
              + jnp.dot(h_hi, wrl_ref[...], preferred_element_type=F32))
    lane = lax.broadcasted_iota(jnp.int32, logits.shape, 1).astype(F32)
    neg = -jnp.inf
    lg = jnp.where(lane < n_experts, logits, neg)
    m1 = jnp.max(lg, axis=-1, keepdims=True)
    i1 = jnp.min(jnp.where(lg == m1, lane, float(LANES)), axis=-1, keepdims=True)
    lg2 = jnp.where(lane == i1, neg, lg)
    m2 = jnp.max(lg2, axis=-1, keepdims=True)
    i2 = jnp.min(jnp.where(lg2 == m2, lane, float(LANES)), axis=-1, keepdims=True)
    e = jnp.exp(m2 - m1)
    w1 = 1.0 / (1.0 + e)
    w2 = e / (1.0 + e)
    r_ref[...] = jnp.where(lane == 0, i1, jnp.where(lane == 1, i2, jnp.where(lane == 2, w1,
                           jnp.where(lane == 3, w2, 0.0))))


def _route(x2d, shift, scale, router, rows_per_mod):
    m, d = x2d.shape
    n_experts = router.shape[1]
    wr32 = jnp.zeros((d, LANES), F32).at[:, :n_experts].set(router)
    wr = wr32.astype(BF16)
    wrl = (wr32 - wr.astype(F32)).astype(BF16)
    tm = _pick_tile(rows_per_mod, 512, 8)
    tiles_per_mod = rows_per_mod // tm
    mod = pl.BlockSpec((None, 1, d), lambda i: (i // tiles_per_mod, 0, 0))
    rspec = pl.BlockSpec((d, LANES), lambda i: (0, 0))
    return pl.pallas_call(
        functools.partial(_route_kernel, n_experts=n_experts),
        grid=(m // tm,),
        in_specs=[pl.BlockSpec((tm, d), lambda i: (i, 0)), mod, mod, rspec, rspec],
        out_specs=[pl.BlockSpec((tm, d), lambda i: (i, 0)), pl.BlockSpec((tm, LANES), lambda i: (i, 0))],
        out_shape=[jax.ShapeDtypeStruct((m, d), F32), jax.ShapeDtypeStruct((m, LANES), F32)],
        compiler_params=_params("parallel"),
        name="moe_route",
    )(x2d, shift, scale, wr, wrl)


EXPERT_TILE_ROWS = 1024


def _expert_ffn_kernel(te_ref, nv_ref, src_ref, h_hbm, w1_ref, w3_ref, w2_ref, o_ref, xbuf, sem, h_scr, acc_scr):
    t = pl.program_id(0)
    f = pl.program_id(1)
    nv = nv_ref[0]
    tm = xbuf.shape[1]

    def issue_rows(tile, slot):
        base = tile * tm

        def body(i, carry):
            pltpu.make_async_copy(h_hbm.at[pl.ds(src_ref[base + i], 1), :], xbuf.at[slot, pl.ds(i, 1), :],
                                  sem.at[slot]).start()
            return carry

        lax.fori_loop(0, tm, body, 0, unroll=8)

    def wait_rows(slot):
        pltpu.make_async_copy(h_hbm.at[pl.ds(0, tm), :], xbuf.at[slot], sem.at[slot]).wait()

    @pl.when(t < nv)
    def _():
        @pl.when(f == 0)
        def _():
            slot = t % 2

            @pl.when(t == 0)
            def _():
                issue_rows(0, 0)

            wait_rows(slot)
            h_scr[...] = xbuf[slot].astype(BF16)
            acc_scr[...] = jnp.zeros_like(acc_scr)

            @pl.when(t + 1 < nv)
            def _():
                issue_rows(t + 1, 1 - slot)

        h = h_scr[...]
        a = jnp.dot(h, w1_ref[...], preferred_element_type=F32)
        b = jnp.dot(h, w3_ref[...], preferred_element_type=F32)
        acc_scr[...] += jnp.dot((_silu(a) * b).astype(BF16), w2_ref[...], preferred_element_type=F32)

        @pl.when(f == pl.num_programs(1) - 1)
        def _():
            o_ref[...] = acc_scr[...]

    @pl.when(jnp.logical_and(t >= nv, f == pl.num_programs(1) - 1))
    def _():
        o_ref[...] = jnp.zeros_like(o_ref)


def _expert_ffn(h, src, tile_expert, n_valid, w1, w3, w2):
    tm = EXPERT_TILE_ROWS
    n = src.shape[0]
    d = h.shape[1]
    dfe = w1.shape[2]
    tf = _pick_tile(dfe, 512, LANES)
    nf = dfe // tf

    def fsel(t, f, nv):
        return jnp.where(t < nv[0], f, nf - 1)

    return pl.pallas_call(
        _expert_ffn_kernel,
        grid_spec=pltpu.PrefetchScalarGridSpec(
            num_scalar_prefetch=3,
            grid=(n // tm, nf),
            in_specs=[pl.BlockSpec(memory_space=pl.ANY),
                      pl.BlockSpec((None, d, tf), lambda t, f, te, nv, sr: (te[t], 0, fsel(t, f, nv))),
                      pl.BlockSpec((None, d, tf), lambda t, f, te, nv, sr: (te[t], 0, fsel(t, f, nv))),
                      pl.BlockSpec((None, tf, d), lambda t, f, te, nv, sr: (te[t], fsel(t, f, nv), 0))],
            out_specs=pl.BlockSpec((tm, d), lambda t, f, te, nv, sr: (t, 0)),
            scratch_shapes=[pltpu.VMEM((2, tm, d), F32), pltpu.SemaphoreType.DMA((2,)),
                            pltpu.VMEM((tm, d), BF16), pltpu.VMEM((tm, d), F32)],
        ),
        out_shape=jax.ShapeDtypeStruct((n, d), F32),
        compiler_params=_params("arbitrary", "arbitrary"),
        name="expert_swiglu",
    )(tile_expert, n_valid, src, h, w1, w3, w2)


def _combine_kernel(pos_ref, x_ref, g_ref, r_ref, ys_hbm, o_ref, ybuf, sem):
    i = pl.program_id(0)
    n = pl.num_programs(0)
    tm = ybuf.shape[2]
    m = n * tm

    def issue_rows(tile, slot):
        for k in range(TOP_K):
            base = k * m + tile * tm

            def body(j, carry):
                pltpu.make_async_copy(ys_hbm.at[pl.ds(pos_ref[base + j], 1), :],
                                      ybuf.at[slot, k, pl.ds(j, 1), :], sem.at[slot]).start()
                return carry

            lax.fori_loop(0, tm, body, 0, unroll=8)

    slot = i % 2

    @pl.when(i == 0)
    def _():
        issue_rows(0, 0)

    pltpu.make_async_copy(ybuf.at[1 - slot], ybuf.at[slot], sem.at[slot]).wait()

    @pl.when(i + 1 < n)
    def _():
        issue_rows(i + 1, 1 - slot)

    r = r_ref[...]
    mix = r[:, 2:3] * ybuf[slot, 0] + r[:, 3:4] * ybuf[slot, 1]
    o_ref[...] = x_ref[...] + g_ref[...] * mix


def _combine(x2d, gate, route, ys, pos, rows_per_mod):
    m, d = x2d.shape
    tm = _pick_tile(rows_per_mod, 512, 8)
    tiles_per_mod = rows_per_mod // tm
    return pl.pallas_call(
        _combine_kernel,
        grid_spec=pltpu.PrefetchScalarGridSpec(
            num_scalar_prefetch=1,
            grid=(m // tm,),
            in_specs=[pl.BlockSpec((tm, d), lambda i, ps: (i, 0)),
                      pl.BlockSpec((None, 1, d), lambda i, ps: (i // tiles_per_mod, 0, 0)),
                      pl.BlockSpec((tm, LANES), lambda i, ps: (i, 0)),
                      pl.BlockSpec(memory_space=pl.ANY)],
            out_specs=pl.BlockSpec((tm, d), lambda i, ps: (i, 0)),
            scratch_shapes=[pltpu.VMEM((2, TOP_K, tm, d), F32), pltpu.SemaphoreType.DMA((2,))],
        ),
        out_shape=jax.ShapeDtypeStruct((m, d), F32),
        compiler_params=_params("arbitrary"),
        name="moe_combine",
    )(pos, x2d, gate, route, ys)


def _moe_ffn(x2d, shift, scale, gate, router, w1, w3, w2, rows_per_mod):
    m, d = x2d.shape
    n_experts = router.shape[1]
    h, route = _route(x2d, shift, scale, router, rows_per_mod)
    tm = EXPERT_TILE_ROWS
    expert = route[:, :TOP_K].astype(jnp.int32).T.reshape(-1)
    onehot = (expert[:, None] == jnp.arange(n_experts)[None, :]).astype(jnp.int32)
    csum = jnp.cumsum(onehot, axis=0)
    rank = jnp.sum(onehot * (csum - 1), axis=1)
    tiles = (csum[-1] + tm - 1) // tm
    tile_end = jnp.cumsum(tiles)
    pos = (jnp.sum(onehot * ((tile_end - tiles) * tm)[None, :], axis=1) + rank).astype(jnp.int32)
    n_tiles = (TOP_K * m) // tm + n_experts
    src = jnp.zeros((n_tiles * tm,), jnp.int32).at[pos].set(jnp.arange(TOP_K * m, dtype=jnp.int32) % m)
    tile_expert = jnp.minimum(jnp.sum((jnp.arange(n_tiles)[:, None] >= tile_end[None, :]).astype(jnp.int32), axis=1),
                              n_experts - 1).astype(jnp.int32)
    n_valid = tile_end[-1:].astype(jnp.int32)
    ys = _expert_ffn(h, src, tile_expert, n_valid, w1, w3, w2)
    return _combine(x2d, gate, route, ys, pos, rows_per_mod)


def kernel(x, c, ctx, c_ctx, w_mod, b_mod, w_in, conv_w, conv_b, rg_lambda, rg_w, rg_b, q_gain, k_gain, rpb,
           w_rnn_o, w_na_o, w_out, ffn_w1, ffn_w3, ffn_w2, router, moe_w1, moe_w3, moe_w2):
    nb, seq, d = x.shape
    cl = ctx.shape[1]
    depth = w_mod.shape[0]
    c_rnn = conv_w.shape[2]
    d_att = rpb.shape[1] * HEAD_DIM
    ctx_cols = c_rnn + 2 * d_att
    rows = seq // GRID_W

    n_cond = -(-(nb + 1) // 8) * 8
    cond = jnp.zeros((n_cond, d), F32).at[:nb].set(c).at[nb].set(c_ctx)
    mods = _adaln(cond, w_mod, b_mod)

    x2 = x.reshape(nb * seq, d)
    xc2 = ctx.reshape(nb * cl, d)
    for l in range(depth):
        ctx_out = l < depth - 1
        lat = [mods[l, :nb, k * d:(k + 1) * d].reshape(nb, 1, d) for k in range(N_MOD)]
        cmod = [mods[l, nb:nb + 1, k * d:(k + 1) * d].reshape(1, 1, d) for k in range(N_MOD)]
        w_in_l = w_in[l].astype(BF16)

        z_lat = _norm_matmul(x2, lat[0], lat[1], w_in_l, seq)
        z_ctx = _norm_matmul(xc2, cmod[0], cmod[1], w_in_l if ctx_out else w_in_l[:, :ctx_cols], nb * cl)
        ncl, ncc = z_lat.shape[1], z_ctx.shape[1]
        z_lat3 = z_lat.reshape(nb, seq, ncl)
        z_ctx3 = z_ctx.reshape(nb, cl, ncc)

        xr = jnp.concatenate([jnp.transpose(z_ctx3[:, :, :c_rnn], (1, 0, 2)),
                              jnp.transpose(z_lat3[:, :, :c_rnn], (1, 0, 2))], axis=0)
        wg_f, bg_f = _gate_weights(rg_w[l, 0], rg_b[l, 0])
        wg_r, bg_r = _gate_weights(rg_w[l, 1], rg_b[l, 1])
        hs_r = _rnn_scan(xr, conv_w[l], conv_b[l], wg_r, bg_r, rg_lambda[l, 1], cl, True, None)
        hs = _rnn_scan(xr, conv_w[l], conv_b[l], wg_f, bg_f, rg_lambda[l, 0], cl, False, hs_r)
        hs_lat = jnp.transpose(hs[cl:], (1, 0, 2)).reshape(nb * seq, c_rnn)

        bias = _na_bias_table(rpb[l], rows)
        na_lat = _neighbourhood_attention(z_lat3, z_ctx3, q_gain[l], k_gain[l], bias, seq)

        wr, wn, wo = w_rnn_o[l].astype(BF16), w_na_o[l].astype(BF16), w_out[l].astype(BF16)
        x2 = _merge(x2, lat[2], hs_lat, z_lat, na_lat.reshape(nb * seq, d_att), wr, wn, wo, seq)
        if ctx_out:
            hs_ctx = jnp.transpose(hs[:cl], (1, 0, 2)).reshape(nb * cl, c_rnn)
            na_ctx = _context_attention(z_ctx3, q_gain[l], k_gain[l], d_att)
            xc2 = _merge(xc2, cmod[2], hs_ctx, z_ctx, na_ctx.reshape(nb * cl, d_att), wr, wn, wo, nb * cl)

        j = l // 2
        if l % 2 == 0:
            w1, w3, w2 = ffn_w1[j].astype(BF16), ffn_w3[j].astype(BF16), ffn_w2[j].astype(BF16)
            x2 = _dense_ffn(x2, lat[3], lat[4], lat[5], w1, w3, w2, seq)
            if ctx_out:
                xc2 = _dense_ffn(xc2, cmod[3], cmod[4], cmod[5], w1, w3, w2, nb * cl)
        else:
            w1, w3, w2 = moe_w1[j].astype(BF16), moe_w3[j].astype(BF16), moe_w2[j].astype(BF16)
            x2 = _moe_ffn(x2, lat[3], lat[4], lat[5], router[j], w1, w3, w2, seq)
            if ctx_out:
                xc2 = _moe_ffn(xc2, cmod[3], cmod[4], cmod[5], router[j], w1, w3, w2, nb * cl)
    return x2.reshape(nb, seq, d)
```

```python
import functools

import numpy as np
import jax
import jax.numpy as jnp
from jax import lax
from jax.experimental import pallas as pl
from jax.experimental.pallas import tpu as pltpu

F32 = jnp.float32
BF16 = jnp.bfloat16

EPS = 1e-6
N_MOD = 6
GRID_W = 64
WIN_H = 8
WIN_W = 16
HEAD_DIM = 64
RNN_BW = 64
CONV_TAPS = 4
CONV_LEFT = 2
LRU_C = 8.0
TOP_K = 2
MASK_VALUE = -1e30

LANES = 128
MXU_DIM = 256
VMEM_LIMIT_BYTES = 56 * 1024 * 1024

NA_GROUP_ROWS = 4
NA_KEY_ROWS = 12


def _params(*sem):
    return pltpu.CompilerParams(dimension_semantics=sem, vmem_limit_bytes=VMEM_LIMIT_BYTES)


def _sigmoid(x):
    return 0.5 * (jnp.tanh(0.5 * x) + 1.0)


def _silu(x):
    return x * _sigmoid(x)


def _gelu_tanh(x):
    return 0.5 * x * (1.0 + jnp.tanh(np.sqrt(2.0 / np.pi) * (x + 0.044715 * (x * x * x))))


def _modulated_norm(x, shift, scale):
    ms = jnp.mean(x * x, axis=-1, keepdims=True)
    return x * lax.rsqrt(ms + EPS) * (1.0 + scale) + shift


def _pick_tile(n, cap, mult):
    best = None
    for t in range(mult, min(n, cap) + 1, mult):
        if n % t == 0:
            best = t
    assert best is not None, (n, cap, mult)
    return best


def _adaln_kernel(c_ref, w_ref, b_ref, o_ref):
    s = _silu(c_ref[...]).astype(BF16)
    o_ref[...] = jnp.dot(s, w_ref[...].astype(BF16), preferred_element_type=F32) + b_ref[...]


def _adaln(cond, w_mod, b_mod):
    depth, d, n = w_mod.shape
    r = cond.shape[0]
    tn = _pick_tile(n, 1536, LANES)
    return pl.pallas_call(
        _adaln_kernel,
        grid=(depth, n // tn),
        in_specs=[
            pl.BlockSpec((r, d), lambda l, j: (0, 0)),
            pl.BlockSpec((None, d, tn), lambda l, j: (l, 0, j)),
            pl.BlockSpec((None, 1, tn), lambda l, j: (l, 0, j)),
        ],
        out_specs=pl.BlockSpec((None, r, tn), lambda l, j: (l, 0, j)),
        out_shape=jax.ShapeDtypeStruct((depth, r, n), F32),
        compiler_params=_params("parallel", "parallel"),
        name="adaln",
    )(cond, w_mod, b_mod.reshape(depth, 1, n))


def _head_norm(x, gain, in_a):
    sq = x * x
    sa = jnp.sum(jnp.where(in_a, sq, 0.0), axis=-1, keepdims=True)
    sb = jnp.sum(jnp.where(in_a, 0.0, sq), axis=-1, keepdims=True)
    ms = jnp.where(in_a, sa, sb) * (1.0 / HEAD_DIM)
    return x * lax.rsqrt(ms + EPS) * gain


def _norm_matmul_kernel(x_ref, sh_ref, sc_ref, w_ref, hg_ref, o_ref, *, seg, normed):
    h = _modulated_norm(x_ref[...], sh_ref[...], sc_ref[...]).astype(BF16)
    in_a = lax.broadcasted_iota(jnp.int32, (1, LANES), 1) < HEAD_DIM
    for j in range(w_ref.shape[1] // seg):
        acc = jnp.dot(h, w_ref[:, j * seg:(j + 1) * seg], preferred_element_type=F32)
        if j in normed:
            gain = hg_ref[normed.index(j)]
            for sl in range(seg // LANES):
                lanes = slice(sl * LANES, (sl + 1) * LANES)
                o_ref[:, j * seg + sl * LANES:j * seg + (sl + 1) * LANES] = _head_norm(
                    acc[:, lanes], gain[:, lanes], in_a).astype(o_ref.dtype)
        else:
            o_ref[:, j * seg:(j + 1) * seg] = acc.astype(o_ref.dtype)


def _norm_matmul(x2d, shift, scale, w, rows_per_mod, head_gains, seg, normed):
    m, d = x2d.shape
    n = w.shape[1]
    tm = _pick_tile(rows_per_mod, 512, 8)
    tiles_per_mod = rows_per_mod // tm
    mod_spec = pl.BlockSpec((None, 1, d), lambda i: (i // tiles_per_mod, 0, 0))
    return pl.pallas_call(
        functools.partial(_norm_matmul_kernel, seg=seg, normed=normed),
        grid=(m // tm,),
        in_specs=[
            pl.BlockSpec((tm, d), lambda i: (i, 0)),
            mod_spec,
            mod_spec,
            pl.BlockSpec((d, n), lambda i: (0, 0), pipeline_mode=pl.Buffered(1)),
            pl.BlockSpec(head_gains.shape, lambda i: (0, 0, 0)),
        ],
        out_specs=pl.BlockSpec((tm, n), lambda i: (i, 0)),
        out_shape=jax.ShapeDtypeStruct((m, n), BF16),
        compiler_params=_params("parallel"),
        name="norm_in_proj",
    )(x2d, shift, scale, w, head_gains)


def _scan_tile_index(i, n_ctx_tiles, n_tiles, reverse):
    if not reverse:
        return i
    return jnp.where(i < n_ctx_tiles, n_ctx_tiles - 1 - i, n_tiles - 1 - (i - n_ctx_tiles))


def _rnn_kernel(xp_ref, x_ref, xn_ref, cw_ref, cb_ref, wg_ref, bg_ref, lam_ref, *rest,
                n_ctx_tiles, n_tiles, reverse, add_prev):
    if add_prev:
        prev_ref, out_ref, a_scr, u_scr, h_scr = rest
    else:
        out_ref, a_scr, u_scr, h_scr = rest
    tl, nb, c = x_ref.shape
    i = pl.program_id(0)
    ti = _scan_tile_index(i, n_ctx_tiles, n_tiles, reverse)

    @pl.when(i == 0)
    def _():
        h_scr[...] = jnp.zeros_like(h_scr)

    at_start = jnp.logical_or(ti == 0, ti == n_ctx_tiles)
    at_end = jnp.logical_or(ti == n_ctx_tiles - 1, ti == n_tiles - 1)
    xp = jnp.where(at_start, 0.0, xp_ref[...].astype(F32))
    xn = jnp.where(at_end, 0.0, xn_ref[...].astype(F32))
    xe = jnp.concatenate([xp, x_ref[...].astype(F32), xn], axis=0)
    xc = cb_ref[...] + xe[0:tl] * cw_ref[0:1, :]
    for k in range(1, CONV_TAPS):
        xc = xc + xe[k:k + tl] * cw_ref[k:k + 1, :]
    xc = xc.reshape(tl * nb, c)

    lam = lam_ref[...]
    log_sig = jnp.minimum(lam, 0.0) - jnp.log(1.0 + jnp.exp(-jnp.abs(lam)))
    c8 = LRU_C * log_sig
    for j in range(c // MXU_DIM):
        sl = slice(MXU_DIM * j, MXU_DIM * (j + 1))
        xj = xc[:, sl]
        g = jnp.dot(xj.astype(BF16), wg_ref[j], preferred_element_type=F32) + bg_ref[j]
        r = _sigmoid(g[:, :MXU_DIM])
        ig = _sigmoid(g[:, MXU_DIM:])
        a = jnp.exp(c8[:, sl] * r)
        a_scr[:, sl] = a
        u_scr[:, sl] = jnp.sqrt(1.0 - a * a) * (ig * xj)

    def step(s, h):
        tt = (tl - 1 - s) if reverse else s
        row = pl.multiple_of(tt * nb, nb)
        h = a_scr[pl.ds(row, nb), :] * h + u_scr[pl.ds(row, nb), :]
        val = h + prev_ref[tt].astype(F32) if add_prev else h
        out_ref[tt] = val.astype(out_ref.dtype)
        return h

    h_scr[...] = lax.fori_loop(0, tl, step, h_scr[...], unroll=4)


def _rnn_scan(xr, conv_w, conv_b, wg, bg, lam, n_ctx, reverse, prev):
    t, nb, c = xr.shape
    tl = _pick_tile(n_ctx, 64, 2)
    assert t % tl == 0 and n_ctx % tl == 0
    n_tiles, n_ctx_tiles = t // tl, n_ctx // tl
    tile = functools.partial(_scan_tile_index, n_ctx_tiles=n_ctx_tiles, n_tiles=n_tiles, reverse=reverse)
    half = tl // CONV_LEFT
    full = lambda shape: pl.BlockSpec(shape, lambda i: (0,) * len(shape))
    in_specs = [
        pl.BlockSpec((CONV_LEFT, nb, c), lambda i: (jnp.maximum(tile(i) * half - 1, 0), 0, 0)),
        pl.BlockSpec((tl, nb, c), lambda i: (tile(i), 0, 0)),
        pl.BlockSpec((1, nb, c), lambda i: (jnp.minimum((tile(i) + 1) * tl, t - 1), 0, 0)),
        full((CONV_TAPS, c)),
        full((1, c)),
        full(wg.shape),
        full(bg.shape),
        full((1, c)),
    ]
    args = [xr, xr, xr, conv_w, conv_b.reshape(1, c), wg, bg, lam.reshape(1, c)]
    if prev is not None:
        in_specs.append(pl.BlockSpec((tl, nb, c), lambda i: (tile(i), 0, 0)))
        args.append(prev)
    kern = functools.partial(_rnn_kernel, n_ctx_tiles=n_ctx_tiles, n_tiles=n_tiles, reverse=reverse,
                             add_prev=prev is not None)
    return pl.pallas_call(
        kern,
        grid=(n_tiles,),
        in_specs=in_specs,
        out_specs=pl.BlockSpec((tl, nb, c), lambda i: (tile(i), 0, 0)),
        out_shape=jax.ShapeDtypeStruct((t, nb, c), BF16),
        scratch_shapes=[pltpu.VMEM((tl * nb, c), F32), pltpu.VMEM((tl * nb, c), F32), pltpu.VMEM((nb, c), F32)],
        compiler_params=_params("arbitrary"),
        name="rglru_scan_rev" if reverse else "rglru_scan_fwd",
    )(*args)


def _gate_weights(rg_w_d, rg_b_d):
    _, nblk, bw, _ = rg_w_d.shape
    per = MXU_DIM // bw
    ngrp = nblk // per
    w = rg_w_d.reshape(2, ngrp, per, bw, bw)
    eye = jnp.eye(per, dtype=rg_w_d.dtype)
    dense = w[:, :, :, :, None, :] * eye[None, None, :, None, :, None]
    dense = dense.reshape(2, ngrp, MXU_DIM, MXU_DIM)
    wg = jnp.concatenate([dense[0], dense[1]], axis=-1).astype(BF16)
    b = rg_b_d.reshape(2, ngrp, 1, MXU_DIM)
    bg = jnp.concatenate([b[0], b[1]], axis=-1)
    return wg, bg


Q_SCALE = (HEAD_DIM ** -0.5) * float(np.log2(np.e))


def _dot_nt(a, b):
    return lax.dot_general(a, b, (((1,), (1,)), ((), ())), preferred_element_type=F32)


def _stack_heads(q, in_a):
    zero = jnp.zeros_like(q)
    return jnp.concatenate([jnp.where(in_a, q, zero), jnp.where(in_a, zero, q)], axis=0)


def _na_kernel(q_ref, k_ref, v_ref, kc_ref, vc_ref, bias_ref, o_ref, sa_scr, sb_scr, *, rows):
    cl = kc_ref.shape[0]
    gq = NA_GROUP_ROWS * GRID_W
    gk = NA_KEY_ROWS * GRID_W
    ngrp = rows // NA_GROUP_ROWS
    in_a = lax.broadcasted_iota(jnp.int32, (1, LANES), 1) < HEAD_DIM

    def key_base(g):
        kb = jnp.clip(NA_GROUP_ROWS * g - WIN_H // 2, 0, rows - NA_KEY_ROWS)
        return pl.multiple_of(kb * GRID_W, GRID_W)

    def scores(g, s_scr):
        cls = jnp.where(g == 0, 0, jnp.where(g == ngrp - 1, 2, 1))
        q2 = _stack_heads(q_ref[pl.ds(pl.multiple_of(g * gq, gq), gq), :], in_a)
        s_scr[:, 0:gk] = _dot_nt(q2, k_ref[pl.ds(key_base(g), gk), :]) + bias_ref[cls]
        s_scr[:, gk:gk + cl] = _dot_nt(q2, kc_ref[...])

    def attend(g, s_scr):
        sc = s_scr[...]
        p = jnp.exp2(sc - jnp.max(sc, axis=-1, keepdims=True))
        den = jnp.sum(p, axis=-1, keepdims=True)
        pb = p.astype(BF16)
        o2 = (jnp.dot(pb[:, 0:gk], v_ref[pl.ds(key_base(g), gk), :], preferred_element_type=F32)
              + jnp.dot(pb[:, gk:gk + cl], vc_ref[...], preferred_element_type=F32)) / den
        o_ref[pl.ds(pl.multiple_of(g * gq, gq), gq), :] = jnp.where(in_a, o2[:gq], o2[gq:]).astype(o_ref.dtype)

    scores(0, sa_scr)

    def pair(it, carry):
        g = 2 * it
        scores(g + 1, sb_scr)
        attend(g, sa_scr)
        scores(g + 2, sa_scr)
        attend(g + 1, sb_scr)
        return carry

    lax.fori_loop(0, ngrp // 2 - 1, pair, 0)
    scores(ngrp - 1, sb_scr)
    attend(ngrp - 2, sa_scr)
    attend(ngrp - 1, sb_scr)


def _na_bias_table(rpb, rows):
    nh, n_ro, n_co = rpb.shape
    ngrp = rows // NA_GROUP_ROWS
    qc = np.arange(GRID_W)[:, None]
    kc = np.arange(GRID_W)[None, :]
    wstart = np.clip(qc - WIN_W // 2, 0, GRID_W - WIN_W)
    col_valid = (kc >= wstart) & (kc < wstart + WIN_W)
    col_onehot = ((kc - qc + WIN_W - 1)[None] == np.arange(n_co)[:, None, None])
    a = np.arange(NA_GROUP_ROWS)[:, None]
    cr = np.arange(NA_KEY_ROWS)[None, :]
    row_onehot, row_valid = [], []
    for g in (0, 1, ngrp - 1):
        kb = int(np.clip(NA_GROUP_ROWS * g - WIN_H // 2, 0, rows - NA_KEY_ROWS))
        r = NA_GROUP_ROWS * g + a
        key_row = kb + cr
        start = np.clip(r - WIN_H // 2, 0, rows - WIN_H)
        valid = (key_row >= start) & (key_row < start + WIN_H)
        row_onehot.append(valid[..., None] & ((key_row - r + WIN_H - 1)[..., None] == np.arange(n_ro)))
        row_valid.append(valid)
    row_onehot = np.stack(row_onehot).astype(np.float32)
    valid = np.stack(row_valid)[:, :, None, :, None] & col_valid[None, None, :, None, :]
    by_col = jnp.einsum("hrj,jqk->hrqk", rpb.astype(F32), col_onehot.astype(np.float32),
                        precision=lax.Precision.HIGHEST)
    tab = jnp.einsum("hoqk,caro->hcaqrk", by_col, row_onehot, precision=lax.Precision.HIGHEST)
    tab = jnp.where(valid[None], tab * np.float32(np.log2(np.e)), MASK_VALUE)
    gq, gk = NA_GROUP_ROWS * GRID_W, NA_KEY_ROWS * GRID_W
    tab = tab.reshape(nh // 2, 2, 3, gq, gk)
    return jnp.transpose(tab, (0, 2, 1, 3, 4)).reshape(nh // 2, 3, 2 * gq, gk)


def _neighbourhood_attention(z_lat, z_ctx, bias, seq):
    nb, s, _ = z_lat.shape
    cl = z_ctx.shape[1]
    npair = bias.shape[0]
    d_att = npair * LANES
    cb = npair
    rows = s // GRID_W
    assert s % GRID_W == 0 and rows % (2 * NA_GROUP_ROWS) == 0 and rows >= NA_KEY_ROWS + NA_GROUP_ROWS
    s_shape = (2 * NA_GROUP_ROWS * GRID_W, NA_KEY_ROWS * GRID_W + cl)
    lat = lambda seg: pl.BlockSpec((None, s, LANES), lambda p, b: (b, 0, seg * cb + p))
    ctx = lambda seg: pl.BlockSpec((None, cl, LANES), lambda p, b: (b, 0, seg * cb + p))
    return pl.pallas_call(
        functools.partial(_na_kernel, rows=rows),
        grid=(npair, nb),
        in_specs=[lat(4), lat(1), lat(2), ctx(1), ctx(2),
                  pl.BlockSpec((None,) + bias.shape[1:], lambda p, b: (p, 0, 0, 0))],
        out_specs=pl.BlockSpec((None, s, LANES), lambda p, b: (b, 0, p)),
        out_shape=jax.ShapeDtypeStruct((nb, s, d_att), BF16),
        scratch_shapes=[pltpu.VMEM(s_shape, F32), pltpu.VMEM(s_shape, F32)],
        compiler_params=_params("parallel", "parallel"),
        name="neighbourhood_attention",
    )(z_lat, z_lat, z_lat, z_ctx, z_ctx, bias)


def _ctx_attn_kernel(q_ref, k_ref, v_ref, o_ref):
    in_a = lax.broadcasted_iota(jnp.int32, (1, LANES), 1) < HEAD_DIM
    n = q_ref.shape[0]
    sc = _dot_nt(_stack_heads(q_ref[...], in_a), k_ref[...])
    p = jnp.exp2(sc - jnp.max(sc, axis=-1, keepdims=True))
    den = jnp.sum(p, axis=-1, keepdims=True)
    o2 = jnp.dot(p.astype(BF16), v_ref[...], preferred_element_type=F32) / den
    o_ref[...] = jnp.where(in_a, o2[:n], o2[n:]).astype(o_ref.dtype)


def _context_attention(z_ctx, d_att):
    nb, cl, _ = z_ctx.shape
    cb = d_att // LANES
    ctx = lambda seg: pl.BlockSpec((None, cl, LANES), lambda p, b: (b, 0, seg * cb + p))
    return pl.pallas_call(
        _ctx_attn_kernel,
        grid=(cb, nb),
        in_specs=[ctx(4), ctx(1), ctx(2)],
        out_specs=pl.BlockSpec((None, cl, LANES), lambda p, b: (b, 0, p)),
        out_shape=jax.ShapeDtypeStruct((nb, cl, d_att), BF16),
        compiler_params=_params("parallel", "parallel"),
        name="context_attention",
    )(z_ctx, z_ctx, z_ctx)


def _merge_kernel(x_ref, g_ref, hs_ref, y_ref, na_ref, gr_ref, gn_ref, wr_ref, wn_ref, wo_ref, o_ref):
    y_rnn = (hs_ref[...].astype(F32) * _gelu_tanh(y_ref[...].astype(F32))).astype(BF16)
    t_rnn = jnp.dot(y_rnn, wr_ref[...], preferred_element_type=F32)
    t_na = jnp.dot(na_ref[...], wn_ref[...], preferred_element_type=F32)
    mix = _sigmoid(gr_ref[...].astype(F32)) * t_rnn + _sigmoid(gn_ref[...].astype(F32)) * t_na
    out = jnp.dot(mix.astype(BF16), wo_ref[...], preferred_element_type=F32)
    o_ref[...] = x_ref[...] + g_ref[...] * out


def _merge(x2d, gate, hs, z, na, w_rnn_o, w_na_o, w_out, rows_per_mod):
    m, d = x2d.shape
    c = hs.shape[1]
    da = na.shape[1]
    tm = _pick_tile(rows_per_mod, 512, 8)
    tiles_per_mod = rows_per_mod // tm
    y_blk = (c + 2 * da) // c
    gr_blk = (2 * c + 3 * da) // d
    gn_blk = gr_blk + 1
    row = lambda width, blk=0: pl.BlockSpec((tm, width), lambda i: (i, blk))
    whole = lambda w: pl.BlockSpec(w.shape, lambda i: (0, 0))
    return pl.pallas_call(
        _merge_kernel,
        grid=(m // tm,),
        in_specs=[row(d), pl.BlockSpec((None, 1, d), lambda i: (i // tiles_per_mod, 0, 0)),
                  row(c), row(c, y_blk), row(da), row(d, gr_blk), row(d, gn_blk),
                  whole(w_rnn_o), whole(w_na_o), whole(w_out)],
        out_specs=row(d),
        out_shape=jax.ShapeDtypeStruct((m, d), F32),
        compiler_params=_params("parallel"),
        name="merge_out_proj",
    )(x2d, gate, hs, z, na, z, z, w_rnn_o, w_na_o, w_out)


def _ffn_kernel(x_ref, sh_ref, sc_ref, g_ref, w1_ref, w3_ref, w2_ref, o_ref, h_scr, acc_scr):
    f = pl.program_id(1)

    @pl.when(f == 0)
    def _():
        h_scr[...] = _modulated_norm(x_ref[...], sh_ref[...], sc_ref[...]).astype(BF16)
        acc_scr[...] = jnp.zeros_like(acc_scr)

    h = h_scr[...]
    a = jnp.dot(h, w1_ref[...], preferred_element_type=F32)
    b = jnp.dot(h, w3_ref[...], preferred_element_type=F32)
    acc_scr[...] += jnp.dot((_silu(a) * b).astype(BF16), w2_ref[...], preferred_element_type=F32)

    @pl.when(f == pl.num_programs(1) - 1)
    def _():
        o_ref[...] = x_ref[...] + g_ref[...] * acc_scr[...]


def _dense_ffn(x2d, shift, scale, gate, w1, w3, w2, rows_per_mod):
    m, d = x2d.shape
    dff = w1.shape[1]
    tm = _pick_tile(rows_per_mod, 512, 8)
    tf = _pick_tile(dff, 1536, LANES)
    tiles_per_mod = rows_per_mod // tm
    mod = pl.BlockSpec((None, 1, d), lambda i, f: (i // tiles_per_mod, 0, 0))
    return pl.pallas_call(
        _ffn_kernel,
        grid=(m // tm, dff // tf),
        in_specs=[pl.BlockSpec((tm, d), lambda i, f: (i, 0)), mod, mod, mod,
                  pl.BlockSpec((d, tf), lambda i, f: (0, f)),
                  pl.BlockSpec((d, tf), lambda i, f: (0, f)),
                  pl.BlockSpec((tf, d), lambda i, f: (f, 0))],
        out_specs=pl.BlockSpec((tm, d), lambda i, f: (i, 0)),
        out_shape=jax.ShapeDtypeStruct((m, d), F32),
        scratch_shapes=[pltpu.VMEM((tm, d), BF16), pltpu.VMEM((tm, d), F32)],
        compiler_params=_params("parallel", "arbitrary"),
        name="dense_swiglu",
    )(x2d, shift, scale, gate, w1, w3, w2)


def _route_kernel(x_ref, sh_ref, sc_ref, wr_ref, wrl_ref, h_ref, r_ref, *, n_experts):
    h = _modulated_norm(x_ref[...], sh_ref[...], sc_ref[...])
    h_ref[...] = h
    h_hi = h.astype(BF16)
    h_lo = (h - h_hi.astype(F32)).astype(BF16)
    logits = (jnp.dot(h_hi, wr_ref[...], preferred_element_type=F32)
              + jnp.dot(h_lo, wr_ref[...], preferred_element_type=F32)
              + jnp.dot(h_hi, wrl_ref[...], preferred_element_type=F32))
    lane = lax.broadcasted_iota(jnp.int32, logits.shape, 1).astype(F32)
    neg = -jnp.inf
    lg = jnp.where(lane < n_experts, logits, neg)
    m1 = jnp.max(lg, axis=-1, keepdims=True)
    i1 = jnp.min(jnp.where(lg == m1, lane, float(LANES)), axis=-1, keepdims=True)
    lg2 = jnp.where(lane == i1, neg, lg)
    m2 = jnp.max(lg2, axis=-1, keepdims=True)
    i2 = jnp.min(jnp.where(lg2 == m2, lane, float(LANES)), axis=-1, keepdims=True)
    e = jnp.exp(m2 - m1)
    w1 = 1.0 / (1.0 + e)
    w2 = e / (1.0 + e)
    r_ref[...] = jnp.where(lane == 0, i1, jnp.where(lane == 1, i2, jnp.where(lane == 2, w1,
                           jnp.where(lane == 3, w2, 0.0))))


def _route(x2d, shift, scale, router, rows_per_mod):
    m, d = x2d.shape
    n_experts = router.shape[1]
    wr32 = jnp.zeros((d, LANES), F32).at[:, :n_experts].set(router)
    wr = wr32.astype(BF16)
    wrl = (wr32 - wr.astype(F32)).astype(BF16)
    tm = _pick_tile(rows_per_mod, 512, 8)
    tiles_per_mod = rows_per_mod // tm
    mod = pl.BlockSpec((None, 1, d), lambda i: (i // tiles_per_mod, 0, 0))
    rspec = pl.BlockSpec((d, LANES), lambda i: (0, 0))
    return pl.pallas_call(
        functools.partial(_route_kernel, n_experts=n_experts),
        grid=(m // tm,),
        in_specs=[pl.BlockSpec((tm, d), lambda i: (i, 0)), mod, mod, rspec, rspec],
        out_specs=[pl.BlockSpec((tm, d), lambda i: (i, 0)), pl.BlockSpec((tm, LANES), lambda i: (i, 0))],
        out_shape=[jax.ShapeDtypeStruct((m, d), F32), jax.ShapeDtypeStruct((m, LANES), F32)],
        compiler_params=_params("parallel"),
        name="moe_route",
    )(x2d, shift, scale, wr, wrl)


EXPERT_TILE_ROWS = 1024


def _expert_ffn_kernel(te_ref, nv_ref, src_ref, h_hbm, w1_ref, w3_ref, w2_ref, o_ref, xbuf, sem, h_scr, acc_scr):
    t = pl.program_id(0)
    f = pl.program_id(1)
    nv = nv_ref[0]
    tm = xbuf.shape[1]

    def issue_rows(tile, slot):
        base = tile * tm

        def body(i, carry):
            pltpu.make_async_copy(h_hbm.at[pl.ds(src_ref[base + i], 1), :], xbuf.at[slot, pl.ds(i, 1), :],
                                  sem.at[slot]).start()
            return carry

        lax.fori_loop(0, tm, body, 0, unroll=8)

    def wait_rows(slot):
        pltpu.make_async_copy(h_hbm.at[pl.ds(0, tm), :], xbuf.at[slot], sem.at[slot]).wait()

    @pl.when(t < nv)
    def _():
        @pl.when(f == 0)
        def _():
            slot = t % 2

            @pl.when(t == 0)
            def _():
                issue_rows(0, 0)

            wait_rows(slot)
            h_scr[...] = xbuf[slot].astype(BF16)
            acc_scr[...] = jnp.zeros_like(acc_scr)

            @pl.when(t + 1 < nv)
            def _():
                issue_rows(t + 1, 1 - slot)

        h = h_scr[...]
        a = jnp.dot(h, w1_ref[...].astype(BF16), preferred_element_type=F32)
        b = jnp.dot(h, w3_ref[...].astype(BF16), preferred_element_type=F32)
        acc_scr[...] += jnp.dot((_silu(a) * b).astype(BF16), w2_ref[...].astype(BF16),
                                preferred_element_type=F32)

        @pl.when(f == pl.num_programs(1) - 1)
        def _():
            o_ref[...] = acc_scr[...]

    @pl.when(jnp.logical_and(t >= nv, f == pl.num_programs(1) - 1))
    def _():
        o_ref[...] = jnp.zeros_like(o_ref)


def _expert_ffn(h, src, tile_expert, n_valid, w1, w3, w2):
    tm = EXPERT_TILE_ROWS
    n = src.shape[0]
    d = h.shape[1]
    dfe = w1.shape[2]
    tf = _pick_tile(dfe, 512, LANES)
    nf = dfe // tf

    def fsel(t, f, nv):
        return jnp.where(t < nv[0], f, nf - 1)

    return pl.pallas_call(
        _expert_ffn_kernel,
        grid_spec=pltpu.PrefetchScalarGridSpec(
            num_scalar_prefetch=3,
            grid=(n // tm, nf),
            in_specs=[pl.BlockSpec(memory_space=pl.ANY),
                      pl.BlockSpec((None, d, tf), lambda t, f, te, nv, sr: (te[t], 0, fsel(t, f, nv))),
                      pl.BlockSpec((None, d, tf), lambda t, f, te, nv, sr: (te[t], 0, fsel(t, f, nv))),
                      pl.BlockSpec((None, tf, d), lambda t, f, te, nv, sr: (te[t], fsel(t, f, nv), 0))],
            out_specs=pl.BlockSpec((tm, d), lambda t, f, te, nv, sr: (t, 0)),
            scratch_shapes=[pltpu.VMEM((2, tm, d), F32), pltpu.SemaphoreType.DMA((2,)),
                            pltpu.VMEM((tm, d), BF16), pltpu.VMEM((tm, d), F32)],
        ),
        out_shape=jax.ShapeDtypeStruct((n, d), F32),
        compiler_params=_params("arbitrary", "arbitrary"),
        name="expert_swiglu",
    )(tile_expert, n_valid, src, h, w1, w3, w2)


def _combine_kernel(pos_ref, x_ref, g_ref, r_ref, ys_hbm, o_ref, ybuf, sem):
    i = pl.program_id(0)
    n = pl.num_programs(0)
    tm = ybuf.shape[2]
    m = n * tm

    def issue_rows(tile, slot):
        for k in range(TOP_K):
            base = k * m + tile * tm

            def body(j, carry):
                pltpu.make_async_copy(ys_hbm.at[pl.ds(pos_ref[base + j], 1), :],
                                      ybuf.at[slot, k, pl.ds(j, 1), :], sem.at[slot]).start()
                return carry

            lax.fori_loop(0, tm, body, 0, unroll=8)

    slot = i % 2

    @pl.when(i == 0)
    def _():
        issue_rows(0, 0)

    pltpu.make_async_copy(ybuf.at[1 - slot], ybuf.at[slot], sem.at[slot]).wait()

    @pl.when(i + 1 < n)
    def _():
        issue_rows(i + 1, 1 - slot)

    r = r_ref[...]
    mix = r[:, 2:3] * ybuf[slot, 0] + r[:, 3:4] * ybuf[slot, 1]
    o_ref[...] = x_ref[...] + g_ref[...] * mix


def _combine(x2d, gate, route, ys, pos, rows_per_mod):
    m, d = x2d.shape
    tm = _pick_tile(rows_per_mod, 512, 8)
    tiles_per_mod = rows_per_mod // tm
    return pl.pallas_call(
        _combine_kernel,
        grid_spec=pltpu.PrefetchScalarGridSpec(
            num_scalar_prefetch=1,
            grid=(m // tm,),
            in_specs=[pl.BlockSpec((tm, d), lambda i, ps: (i, 0)),
                      pl.BlockSpec((None, 1, d), lambda i, ps: (i // tiles_per_mod, 0, 0)),
                      pl.BlockSpec((tm, LANES), lambda i, ps: (i, 0)),
                      pl.BlockSpec(memory_space=pl.ANY)],
            out_specs=pl.BlockSpec((tm, d), lambda i, ps: (i, 0)),
            scratch_shapes=[pltpu.VMEM((2, TOP_K, tm, d), F32), pltpu.SemaphoreType.DMA((2,))],
        ),
        out_shape=jax.ShapeDtypeStruct((m, d), F32),
        compiler_params=_params("arbitrary"),
        name="moe_combine",
    )(pos, x2d, gate, route, ys)


def _moe_ffn(x2d, shift, scale, gate, router, w1, w3, w2, rows_per_mod):
    m, d = x2d.shape
    n_experts = router.shape[1]
    h, route = _route(x2d, shift, scale, router, rows_per_mod)
    tm = EXPERT_TILE_ROWS
    expert = route[:, :TOP_K].astype(jnp.int32).T.reshape(-1)
    onehot = (expert[:, None] == jnp.arange(n_experts)[None, :]).astype(jnp.int32)
    csum = jnp.cumsum(onehot, axis=0)
    rank = jnp.sum(onehot * (csum - 1), axis=1)
    tiles = (csum[-1] + tm - 1) // tm
    tile_end = jnp.cumsum(tiles)
    pos = (jnp.sum(onehot * ((tile_end - tiles) * tm)[None, :], axis=1) + rank).astype(jnp.int32)
    n_tiles = (TOP_K * m) // tm + n_experts
    src = jnp.zeros((n_tiles * tm,), jnp.int32).at[pos].set(jnp.arange(TOP_K * m, dtype=jnp.int32) % m)
    tile_expert = jnp.minimum(jnp.sum((jnp.arange(n_tiles)[:, None] >= tile_end[None, :]).astype(jnp.int32), axis=1),
                              n_experts - 1).astype(jnp.int32)
    n_valid = tile_end[-1:].astype(jnp.int32)
    ys = _expert_ffn(h, src, tile_expert, n_valid, w1, w3, w2)
    return _combine(x2d, gate, route, ys, pos, rows_per_mod)


def kernel(x, c, ctx, c_ctx, w_mod, b_mod, w_in, conv_w, conv_b, rg_lambda, rg_w, rg_b, q_gain, k_gain, rpb,
           w_rnn_o, w_na_o, w_out, ffn_w1, ffn_w3, ffn_w2, router, moe_w1, moe_w3, moe_w2):
    nb, seq, d = x.shape
    cl = ctx.shape[1]
    depth = w_mod.shape[0]
    c_rnn = conv_w.shape[2]
    d_att = rpb.shape[1] * HEAD_DIM
    ctx_cols = c_rnn + 2 * d_att
    rows = seq // GRID_W

    n_cond = -(-(nb + 1) // 8) * 8
    cond = jnp.zeros((n_cond, d), F32).at[:nb].set(c).at[nb].set(c_ctx)
    mods = _adaln(cond, w_mod, b_mod)

    x2 = x.reshape(nb * seq, d)
    xc2 = ctx.reshape(nb * cl, d)
    for l in range(depth):
        ctx_out = l < depth - 1
        lat = [mods[l, :nb, k * d:(k + 1) * d].reshape(nb, 1, d) for k in range(N_MOD)]
        cmod = [mods[l, nb:nb + 1, k * d:(k + 1) * d].reshape(1, 1, d) for k in range(N_MOD)]
        w_in_l = w_in[l].astype(BF16)
        tile_heads = lambda g: jnp.tile(g.astype(F32), c_rnn // HEAD_DIM).reshape(1, c_rnn)
        head_gains = jnp.stack([tile_heads(k_gain[l]), tile_heads(q_gain[l]) * Q_SCALE])

        z_lat = _norm_matmul(x2, lat[0], lat[1], w_in_l, seq, head_gains, c_rnn, (1, 4))
        if ctx_out:
            z_ctx = _norm_matmul(xc2, cmod[0], cmod[1], w_in_l, nb * cl, head_gains, c_rnn, (1, 4))
        else:
            z_ctx = _norm_matmul(xc2, cmod[0], cmod[1], w_in_l[:, :ctx_cols], nb * cl, head_gains, c_rnn, (1,))
        ncl, ncc = z_lat.shape[1], z_ctx.shape[1]
        z_lat3 = z_lat.reshape(nb, seq, ncl)
        z_ctx3 = z_ctx.reshape(nb, cl, ncc)

        xr = jnp.concatenate([jnp.transpose(z_ctx3[:, :, :c_rnn], (1, 0, 2)),
                              jnp.transpose(z_lat3[:, :, :c_rnn], (1, 0, 2))], axis=0)
        wg_f, bg_f = _gate_weights(rg_w[l, 0], rg_b[l, 0])
        wg_r, bg_r = _gate_weights(rg_w[l, 1], rg_b[l, 1])
        hs_r = _rnn_scan(xr, conv_w[l], conv_b[l], wg_r, bg_r, rg_lambda[l, 1], cl, True, None)
        hs = _rnn_scan(xr, conv_w[l], conv_b[l], wg_f, bg_f, rg_lambda[l, 0], cl, False, hs_r)
        hs_lat = jnp.transpose(hs[cl:], (1, 0, 2)).reshape(nb * seq, c_rnn)

        bias = _na_bias_table(rpb[l], rows)
        na_lat = _neighbourhood_attention(z_lat3, z_ctx3, bias, seq)

        wr, wn, wo = w_rnn_o[l].astype(BF16), w_na_o[l].astype(BF16), w_out[l].astype(BF16)
        x2 = _merge(x2, lat[2], hs_lat, z_lat, na_lat.reshape(nb * seq, d_att), wr, wn, wo, seq)
        if ctx_out:
            hs_ctx = jnp.transpose(hs[:cl], (1, 0, 2)).reshape(nb * cl, c_rnn)
            na_ctx = _context_attention(z_ctx3, d_att)
            xc2 = _merge(xc2, cmod[2], hs_ctx, z_ctx, na_ctx.reshape(nb * cl, d_att), wr, wn, wo, nb * cl)

        j = l // 2
        if l % 2 == 0:
            w1, w3, w2 = ffn_w1[j].astype(BF16), ffn_w3[j].astype(BF16), ffn_w2[j].astype(BF16)
            x2 = _dense_ffn(x2, lat[3], lat[4], lat[5], w1, w3, w2, seq)
            if ctx_out:
                xc2 = _dense_ffn(xc2, cmod[3], cmod[4], cmod[5], w1, w3, w2, nb * cl)
        else:
            w1, w3, w2 = moe_w1[j], moe_w3[j], moe_w2[j]
            x2 = _moe_ffn(x2, lat[3], lat[4], lat[5], router[j], w1, w3, w2, seq)
            if ctx_out:
                xc2 = _moe_ffn(xc2, cmod[3], cmod[4], cmod[5], router[j], w1, w3, w2, nb * cl)
    return x2.reshape(nb, seq, d)
```

```python
import functools

import numpy as np
import jax
import jax.numpy as jnp
from jax import lax
from jax.experimental import pallas as pl
from jax.experimental.pallas import tpu as pltpu

F32 = jnp.float32
BF16 = jnp.bfloat16

EPS = 1e-6
N_MOD = 6
GRID_W = 64
WIN_H = 8
WIN_W = 16
HEAD_DIM = 64
RNN_BW = 64
CONV_TAPS = 4
CONV_LEFT = 2
LRU_C = 8.0
TOP_K = 2
MASK_VALUE = -1e30
SQRT_FLOOR = 1e-30

LANES = 128
MXU_DIM = 256
VMEM_LIMIT_BYTES = 56 * 1024 * 1024

NA_GROUP_ROWS = 4
NA_KEY_ROWS = 12


def _params(*sem):
    return pltpu.CompilerParams(dimension_semantics=sem, vmem_limit_bytes=VMEM_LIMIT_BYTES)


def _sigmoid(x):
    return 0.5 * (jnp.tanh(0.5 * x) + 1.0)


def _silu(x):
    return x * _sigmoid(x)


def _gelu_tanh(x):
    return 0.5 * x * (1.0 + jnp.tanh(np.sqrt(2.0 / np.pi) * (x + 0.044715 * (x * x * x))))


def _modulated_norm(x, shift, scale):
    ms = jnp.mean(x * x, axis=-1, keepdims=True)
    return x * lax.rsqrt(ms + EPS) * (1.0 + scale) + shift


def _pick_tile(n, cap, mult):
    best = None
    for t in range(mult, min(n, cap) + 1, mult):
        if n % t == 0:
            best = t
    assert best is not None, (n, cap, mult)
    return best


def _adaln_kernel(c_ref, w_ref, b_ref, o_ref):
    s = _silu(c_ref[...]).astype(BF16)
    o_ref[...] = jnp.dot(s, w_ref[...].astype(BF16), preferred_element_type=F32) + b_ref[...]


def _adaln(cond, w_mod, b_mod):
    depth, d, n = w_mod.shape
    r = cond.shape[0]
    tn = _pick_tile(n, 1536, LANES)
    return pl.pallas_call(
        _adaln_kernel,
        grid=(depth, n // tn),
        in_specs=[
            pl.BlockSpec((r, d), lambda l, j: (0, 0)),
            pl.BlockSpec((None, d, tn), lambda l, j: (l, 0, j)),
            pl.BlockSpec((None, 1, tn), lambda l, j: (l, 0, j)),
        ],
        out_specs=pl.BlockSpec((None, r, tn), lambda l, j: (l, 0, j)),
        out_shape=jax.ShapeDtypeStruct((depth, r, n), F32),
        compiler_params=_params("parallel", "parallel"),
        name="adaln",
    )(cond, w_mod, b_mod.reshape(depth, 1, n))


def _head_norm(x, gain, in_a):
    sq = x * x
    sa = jnp.sum(jnp.where(in_a, sq, 0.0), axis=-1, keepdims=True)
    sb = jnp.sum(jnp.where(in_a, 0.0, sq), axis=-1, keepdims=True)
    ms = jnp.where(in_a, sa, sb) * (1.0 / HEAD_DIM)
    return x * lax.rsqrt(ms + EPS) * gain


def _norm_matmul_kernel(x_ref, sh_ref, sc_ref, w_ref, hg_ref, o_ref, *, seg, normed):
    h = _modulated_norm(x_ref[...], sh_ref[...], sc_ref[...]).astype(BF16)
    in_a = lax.broadcasted_iota(jnp.int32, (1, LANES), 1) < HEAD_DIM
    for j in range(w_ref.shape[1] // seg):
        acc = jnp.dot(h, w_ref[:, j * seg:(j + 1) * seg], preferred_element_type=F32)
        if j in normed:
            gain = hg_ref[normed.index(j)]
            for sl in range(seg // LANES):
                lanes = slice(sl * LANES, (sl + 1) * LANES)
                o_ref[:, j * seg + sl * LANES:j * seg + (sl + 1) * LANES] = _head_norm(
                    acc[:, lanes], gain[:, lanes], in_a).astype(o_ref.dtype)
        else:
            o_ref[:, j * seg:(j + 1) * seg] = acc.astype(o_ref.dtype)


def _norm_matmul(x2d, shift, scale, w, rows_per_mod, head_gains, seg, normed):
    m, d = x2d.shape
    n = w.shape[1]
    tm = _pick_tile(rows_per_mod, 512, 8)
    tiles_per_mod = rows_per_mod // tm
    mod_spec = pl.BlockSpec((None, 1, d), lambda i: (i // tiles_per_mod, 0, 0))
    return pl.pallas_call(
        functools.partial(_norm_matmul_kernel, seg=seg, normed=normed),
        grid=(m // tm,),
        in_specs=[
            pl.BlockSpec((tm, d), lambda i: (i, 0)),
            mod_spec,
            mod_spec,
            pl.BlockSpec((d, n), lambda i: (0, 0), pipeline_mode=pl.Buffered(1)),
            pl.BlockSpec(head_gains.shape, lambda i: (0, 0, 0)),
        ],
        out_specs=pl.BlockSpec((tm, n), lambda i: (i, 0)),
        out_shape=jax.ShapeDtypeStruct((m, n), BF16),
        compiler_params=_params("parallel"),
        name="norm_in_proj",
    )(x2d, shift, scale, w, head_gains)


def _scan_tile_index(i, n_ctx_tiles, n_tiles, reverse):
    if not reverse:
        return i
    return jnp.where(i < n_ctx_tiles, n_ctx_tiles - 1 - i, n_tiles - 1 - (i - n_ctx_tiles))


def _rnn_kernel(xp_ref, x_ref, xn_ref, cw_ref, cb_ref, wg_ref, bg_ref, lam_ref, *rest,
                n_ctx_tiles, n_tiles, reverse, add_prev):
    if add_prev:
        prev_ref, out_ref, a_scr, u_scr, h_scr = rest
    else:
        out_ref, a_scr, u_scr, h_scr = rest
    tl, nb, c = x_ref.shape
    i = pl.program_id(0)
    ti = _scan_tile_index(i, n_ctx_tiles, n_tiles, reverse)

    @pl.when(i == 0)
    def _():
        h_scr[...] = jnp.zeros_like(h_scr)

    at_start = jnp.logical_or(ti == 0, ti == n_ctx_tiles)
    at_end = jnp.logical_or(ti == n_ctx_tiles - 1, ti == n_tiles - 1)
    xp = jnp.where(at_start, 0.0, xp_ref[...].astype(F32))
    xn = jnp.where(at_end, 0.0, xn_ref[...].astype(F32))
    xe = jnp.concatenate([xp, x_ref[...].astype(F32), xn], axis=0)
    xc = cb_ref[...] + xe[0:tl] * cw_ref[0:1, :]
    for k in range(1, CONV_TAPS):
        xc = xc + xe[k:k + tl] * cw_ref[k:k + 1, :]
    xc = xc.reshape(tl * nb, c)

    lam = lam_ref[...]
    log_sig = jnp.minimum(lam, 0.0) - jnp.log(1.0 + jnp.exp(-jnp.abs(lam)))
    k2 = (0.5 * LRU_C * np.log2(np.e)) * log_sig
    for j in range(c // MXU_DIM):
        sl = slice(MXU_DIM * j, MXU_DIM * (j + 1))
        xj = xc[:, sl]
        t = jnp.tanh(jnp.dot(xj.astype(BF16), wg_ref[j], preferred_element_type=F32) + bg_ref[j])
        a = jnp.exp2(k2[:, sl] * t[:, :MXU_DIM] + k2[:, sl])
        v = 1.0 - a * a
        gated_x = (t[:, MXU_DIM:] + 1.0) * (0.5 * xj)
        a_scr[:, sl] = a
        u_scr[:, sl] = (v * lax.rsqrt(jnp.maximum(v, SQRT_FLOOR))) * gated_x

    def step(s, h):
        tt = (tl - 1 - s) if reverse else s
        row = pl.multiple_of(tt * nb, nb)
        h = a_scr[pl.ds(row, nb), :] * h + u_scr[pl.ds(row, nb), :]
        val = h + prev_ref[tt].astype(F32) if add_prev else h
        out_ref[tt] = val.astype(out_ref.dtype)
        return h

    h_scr[...] = lax.fori_loop(0, tl, step, h_scr[...], unroll=4)


def _rnn_scan(xr, conv_w, conv_b, wg, bg, lam, n_ctx, reverse, prev):
    t, nb, c = xr.shape
    tl = _pick_tile(n_ctx, 64, 2)
    assert t % tl == 0 and n_ctx % tl == 0
    n_tiles, n_ctx_tiles = t // tl, n_ctx // tl
    tile = functools.partial(_scan_tile_index, n_ctx_tiles=n_ctx_tiles, n_tiles=n_tiles, reverse=reverse)
    half = tl // CONV_LEFT
    full = lambda shape: pl.BlockSpec(shape, lambda i: (0,) * len(shape))
    in_specs = [
        pl.BlockSpec((CONV_LEFT, nb, c), lambda i: (jnp.maximum(tile(i) * half - 1, 0), 0, 0)),
        pl.BlockSpec((tl, nb, c), lambda i: (tile(i), 0, 0)),
        pl.BlockSpec((1, nb, c), lambda i: (jnp.minimum((tile(i) + 1) * tl, t - 1), 0, 0)),
        full((CONV_TAPS, c)),
        full((1, c)),
        full(wg.shape),
        full(bg.shape),
        full((1, c)),
    ]
    args = [xr, xr, xr, conv_w, conv_b.reshape(1, c), wg, bg, lam.reshape(1, c)]
    if prev is not None:
        in_specs.append(pl.BlockSpec((tl, nb, c), lambda i: (tile(i), 0, 0)))
        args.append(prev)
    kern = functools.partial(_rnn_kernel, n_ctx_tiles=n_ctx_tiles, n_tiles=n_tiles, reverse=reverse,
                             add_prev=prev is not None)
    return pl.pallas_call(
        kern,
        grid=(n_tiles,),
        in_specs=in_specs,
        out_specs=pl.BlockSpec((tl, nb, c), lambda i: (tile(i), 0, 0)),
        out_shape=jax.ShapeDtypeStruct((t, nb, c), BF16),
        scratch_shapes=[pltpu.VMEM((tl * nb, c), F32), pltpu.VMEM((tl * nb, c), F32), pltpu.VMEM((nb, c), F32)],
        compiler_params=_params("arbitrary"),
        name="rglru_scan_rev" if reverse else "rglru_scan_fwd",
    )(*args)


def _gate_weights(rg_w_d, rg_b_d):
    _, nblk, bw, _ = rg_w_d.shape
    per = MXU_DIM // bw
    ngrp = nblk // per
    w = rg_w_d.reshape(2, ngrp, per, bw, bw)
    eye = jnp.eye(per, dtype=rg_w_d.dtype)
    dense = w[:, :, :, :, None, :] * eye[None, None, :, None, :, None]
    dense = dense.reshape(2, ngrp, MXU_DIM, MXU_DIM)
    wg = jnp.concatenate([dense[0], dense[1]], axis=-1).astype(BF16)
    b = rg_b_d.reshape(2, ngrp, 1, MXU_DIM)
    bg = jnp.concatenate([b[0], b[1]], axis=-1)
    return wg * 0.5, bg * 0.5


Q_SCALE = (HEAD_DIM ** -0.5) * float(np.log2(np.e))


def _dot_nt(a, b):
    return lax.dot_general(a, b, (((1,), (1,)), ((), ())), preferred_element_type=F32)


def _stack_heads(q, in_a):
    zero = jnp.zeros_like(q)
    return jnp.concatenate([jnp.where(in_a, q, zero), jnp.where(in_a, zero, q)], axis=0)


def _na_kernel(q_ref, k_ref, v_ref, kc_ref, vc_ref, bias_ref, o_ref, sa_scr, sb_scr, *, rows):
    cl = kc_ref.shape[0]
    gq = NA_GROUP_ROWS * GRID_W
    gk = NA_KEY_ROWS * GRID_W
    ngrp = rows // NA_GROUP_ROWS
    in_a = lax.broadcasted_iota(jnp.int32, (1, LANES), 1) < HEAD_DIM

    def key_base(g):
        kb = jnp.clip(NA_GROUP_ROWS * g - WIN_H // 2, 0, rows - NA_KEY_ROWS)
        return pl.multiple_of(kb * GRID_W, GRID_W)

    def scores(g, s_scr):
        cls = jnp.where(g == 0, 0, jnp.where(g == ngrp - 1, 2, 1))
        q2 = _stack_heads(q_ref[pl.ds(pl.multiple_of(g * gq, gq), gq), :], in_a)
        s_scr[:, 0:gk] = _dot_nt(q2, k_ref[pl.ds(key_base(g), gk), :]) + bias_ref[cls]
        s_scr[:, gk:gk + cl] = _dot_nt(q2, kc_ref[...])

    def attend(g, s_scr):
        sc = s_scr[...]
        p = jnp.exp2(sc - jnp.max(sc, axis=-1, keepdims=True))
        den = jnp.sum(p, axis=-1, keepdims=True)
        pb = p.astype(BF16)
        o2 = (jnp.dot(pb[:, 0:gk], v_ref[pl.ds(key_base(g), gk), :], preferred_element_type=F32)
              + jnp.dot(pb[:, gk:gk + cl], vc_ref[...], preferred_element_type=F32)) / den
        o_ref[pl.ds(pl.multiple_of(g * gq, gq), gq), :] = jnp.where(in_a, o2[:gq], o2[gq:]).astype(o_ref.dtype)

    scores(0, sa_scr)

    def pair(it, carry):
        g = 2 * it
        scores(g + 1, sb_scr)
        attend(g, sa_scr)
        scores(g + 2, sa_scr)
        attend(g + 1, sb_scr)
        return carry

    lax.fori_loop(0, ngrp // 2 - 1, pair, 0)
    scores(ngrp - 1, sb_scr)
    attend(ngrp - 2, sa_scr)
    attend(ngrp - 1, sb_scr)


def _na_bias_table(rpb, rows):
    nh, n_ro, n_co = rpb.shape
    ngrp = rows // NA_GROUP_ROWS
    qc = np.arange(GRID_W)[:, None]
    kc = np.arange(GRID_W)[None, :]
    wstart = np.clip(qc - WIN_W // 2, 0, GRID_W - WIN_W)
    col_valid = (kc >= wstart) & (kc < wstart + WIN_W)
    col_onehot = ((kc - qc + WIN_W - 1)[None] == np.arange(n_co)[:, None, None])
    a = np.arange(NA_GROUP_ROWS)[:, None]
    cr = np.arange(NA_KEY_ROWS)[None, :]
    row_onehot, row_valid = [], []
    for g in (0, 1, ngrp - 1):
        kb = int(np.clip(NA_GROUP_ROWS * g - WIN_H // 2, 0, rows - NA_KEY_ROWS))
        r = NA_GROUP_ROWS * g + a
        key_row = kb + cr
        start = np.clip(r - WIN_H // 2, 0, rows - WIN_H)
        valid = (key_row >= start) & (key_row < start + WIN_H)
        row_onehot.append(valid[..., None] & ((key_row - r + WIN_H - 1)[..., None] == np.arange(n_ro)))
        row_valid.append(valid)
    row_onehot = np.stack(row_onehot).astype(np.float32)
    valid = np.stack(row_valid)[:, :, None, :, None] & col_valid[None, None, :, None, :]
    rpb2 = rpb.astype(F32).reshape(nh // 2, 2, n_ro, n_co)
    by_col = jnp.einsum("pgrj,jqk->pgrqk", rpb2, col_onehot.astype(np.float32),
                        precision=lax.Precision.HIGHEST)
    tab = jnp.einsum("pgoqk,caro->pcgaqrk", by_col, row_onehot, precision=lax.Precision.HIGHEST)
    tab = jnp.where(valid[None, :, None], tab * np.float32(np.log2(np.e)), MASK_VALUE)
    return tab.reshape(nh // 2, 3, 2 * NA_GROUP_ROWS * GRID_W, NA_KEY_ROWS * GRID_W)


def _neighbourhood_attention(z_lat, z_ctx, bias, seq):
    nb, s, _ = z_lat.shape
    cl = z_ctx.shape[1]
    npair = bias.shape[0]
    d_att = npair * LANES
    cb = npair
    rows = s // GRID_W
    assert s % GRID_W == 0 and rows % (2 * NA_GROUP_ROWS) == 0 and rows >= NA_KEY_ROWS + NA_GROUP_ROWS
    s_shape = (2 * NA_GROUP_ROWS * GRID_W, NA_KEY_ROWS * GRID_W + cl)
    lat = lambda seg: pl.BlockSpec((None, s, LANES), lambda p, b: (b, 0, seg * cb + p))
    ctx = lambda seg: pl.BlockSpec((None, cl, LANES), lambda p, b: (b, 0, seg * cb + p))
    return pl.pallas_call(
        functools.partial(_na_kernel, rows=rows),
        grid=(npair, nb),
        in_specs=[lat(4), lat(1), lat(2), ctx(1), ctx(2),
                  pl.BlockSpec((None,) + bias.shape[1:], lambda p, b: (p, 0, 0, 0))],
        out_specs=pl.BlockSpec((None, s, LANES), lambda p, b: (b, 0, p)),
        out_shape=jax.ShapeDtypeStruct((nb, s, d_att), BF16),
        scratch_shapes=[pltpu.VMEM(s_shape, F32), pltpu.VMEM(s_shape, F32)],
        compiler_params=_params("parallel", "parallel"),
        name="neighbourhood_attention",
    )(z_lat, z_lat, z_lat, z_ctx, z_ctx, bias)


def _ctx_attn_kernel(q_ref, k_ref, v_ref, o_ref):
    in_a = lax.broadcasted_iota(jnp.int32, (1, LANES), 1) < HEAD_DIM
    n = q_ref.shape[0]
    sc = _dot_nt(_stack_heads(q_ref[...], in_a), k_ref[...])
    p = jnp.exp2(sc - jnp.max(sc, axis=-1, keepdims=True))
    den = jnp.sum(p, axis=-1, keepdims=True)
    o2 = jnp.dot(p.astype(BF16), v_ref[...], preferred_element_type=F32) / den
    o_ref[...] = jnp.where(in_a, o2[:n], o2[n:]).astype(o_ref.dtype)


def _context_attention(z_ctx, d_att):
    nb, cl, _ = z_ctx.shape
    cb = d_att // LANES
    ctx = lambda seg: pl.BlockSpec((None, cl, LANES), lambda p, b: (b, 0, seg * cb + p))
    return pl.pallas_call(
        _ctx_attn_kernel,
        grid=(cb, nb),
        in_specs=[ctx(4), ctx(1), ctx(2)],
        out_specs=pl.BlockSpec((None, cl, LANES), lambda p, b: (b, 0, p)),
        out_shape=jax.ShapeDtypeStruct((nb, cl, d_att), BF16),
        compiler_params=_params("parallel", "parallel"),
        name="context_attention",
    )(z_ctx, z_ctx, z_ctx)


def _merge_kernel(x_ref, g_ref, hs_ref, y_ref, na_ref, gr_ref, gn_ref, wr_ref, wn_ref, wo_ref, o_ref):
    y_rnn = (hs_ref[...].astype(F32) * _gelu_tanh(y_ref[...].astype(F32))).astype(BF16)
    t_rnn = jnp.dot(y_rnn, wr_ref[...], preferred_element_type=F32)
    t_na = jnp.dot(na_ref[...], wn_ref[...], preferred_element_type=F32)
    mix = _sigmoid(gr_ref[...].astype(F32)) * t_rnn + _sigmoid(gn_ref[...].astype(F32)) * t_na
    out = jnp.dot(mix.astype(BF16), wo_ref[...], preferred_element_type=F32)
    o_ref[...] = x_ref[...] + g_ref[...] * out


def _merge(x2d, gate, hs, z, na, w_rnn_o, w_na_o, w_out, rows_per_mod):
    m, d = x2d.shape
    c = hs.shape[1]
    da = na.shape[1]
    tm = _pick_tile(rows_per_mod, 512, 8)
    tiles_per_mod = rows_per_mod // tm
    y_blk = (c + 2 * da) // c
    gr_blk = (2 * c + 3 * da) // d
    gn_blk = gr_blk + 1
    row = lambda width, blk=0: pl.BlockSpec((tm, width), lambda i: (i, blk))
    whole = lambda w: pl.BlockSpec(w.shape, lambda i: (0, 0))
    return pl.pallas_call(
        _merge_kernel,
        grid=(m // tm,),
        in_specs=[row(d), pl.BlockSpec((None, 1, d), lambda i: (i // tiles_per_mod, 0, 0)),
                  row(c), row(c, y_blk), row(da), row(d, gr_blk), row(d, gn_blk),
                  whole(w_rnn_o), whole(w_na_o), whole(w_out)],
        out_specs=row(d),
        out_shape=jax.ShapeDtypeStruct((m, d), F32),
        compiler_params=_params("parallel"),
        name="merge_out_proj",
    )(x2d, gate, hs, z, na, z, z, w_rnn_o, w_na_o, w_out)


def _ffn_kernel(x_ref, sh_ref, sc_ref, g_ref, w1_ref, w3_ref, w2_ref, o_ref, h_scr, acc_scr):
    f = pl.program_id(1)

    @pl.when(f == 0)
    def _():
        h_scr[...] = _modulated_norm(x_ref[...], sh_ref[...], sc_ref[...]).astype(BF16)
        acc_scr[...] = jnp.zeros_like(acc_scr)

    h = h_scr[...]
    a = jnp.dot(h, w1_ref[...], preferred_element_type=F32)
    b = jnp.dot(h, w3_ref[...], preferred_element_type=F32)
    acc_scr[...] += jnp.dot((_silu(a) * b).astype(BF16), w2_ref[...], preferred_element_type=F32)

    @pl.when(f == pl.num_programs(1) - 1)
    def _():
        o_ref[...] = x_ref[...] + g_ref[...] * acc_scr[...]


def _dense_ffn(x2d, shift, scale, gate, w1, w3, w2, rows_per_mod):
    m, d = x2d.shape
    dff = w1.shape[1]
    tm = _pick_tile(rows_per_mod, 512, 8)
    tf = _pick_tile(dff, 1536, LANES)
    tiles_per_mod = rows_per_mod // tm
    mod = pl.BlockSpec((None, 1, d), lambda i, f: (i // tiles_per_mod, 0, 0))
    return pl.pallas_call(
        _ffn_kernel,
        grid=(m // tm, dff // tf),
        in_specs=[pl.BlockSpec((tm, d), lambda i, f: (i, 0)), mod, mod, mod,
                  pl.BlockSpec((d, tf), lambda i, f: (0, f)),
                  pl.BlockSpec((d, tf), lambda i, f: (0, f)),
                  pl.BlockSpec((tf, d), lambda i, f: (f, 0))],
        out_specs=pl.BlockSpec((tm, d), lambda i, f: (i, 0)),
        out_shape=jax.ShapeDtypeStruct((m, d), F32),
        scratch_shapes=[pltpu.VMEM((tm, d), BF16), pltpu.VMEM((tm, d), F32)],
        compiler_params=_params("parallel", "arbitrary"),
        name="dense_swiglu",
    )(x2d, shift, scale, gate, w1, w3, w2)


SUBLANES = 8


def _to_token_tiles(ref, x):
    n = x.shape[0]
    for s in range(SUBLANES):
        ref[pl.ds(s, n, stride=SUBLANES), :] = x[:, s * LANES:(s + 1) * LANES]


def _from_token_tiles(ref, n, s):
    return ref[pl.ds(s, n, stride=SUBLANES), :]


def _route_kernel(x_ref, sh_ref, sc_ref, wr_ref, wrl_ref, h_ref, r_ref, *, n_experts):
    h = _modulated_norm(x_ref[...], sh_ref[...], sc_ref[...])
    _to_token_tiles(h_ref, h)
    h_hi = h.astype(BF16)
    h_lo = (h - h_hi.astype(F32)).astype(BF16)
    logits = (jnp.dot(h_hi, wr_ref[...], preferred_element_type=F32)
              + jnp.dot(h_lo, wr_ref[...], preferred_element_type=F32)
              + jnp.dot(h_hi, wrl_ref[...], preferred_element_type=F32))
    lane = lax.broadcasted_iota(jnp.int32, logits.shape, 1).astype(F32)
    neg = -jnp.inf
    lg = jnp.where(lane < n_experts, logits, neg)
    m1 = jnp.max(lg, axis=-1, keepdims=True)
    i1 = jnp.min(jnp.where(lg == m1, lane, float(LANES)), axis=-1, keepdims=True)
    lg2 = jnp.where(lane == i1, neg, lg)
    m2 = jnp.max(lg2, axis=-1, keepdims=True)
    i2 = jnp.min(jnp.where(lg2 == m2, lane, float(LANES)), axis=-1, keepdims=True)
    e = jnp.exp(m2 - m1)
    w1 = 1.0 / (1.0 + e)
    w2 = e / (1.0 + e)
    r_ref[...] = jnp.where(lane == 0, i1, jnp.where(lane == 1, i2, jnp.where(lane == 2, w1,
                           jnp.where(lane == 3, w2, 0.0))))


def _route(x2d, shift, scale, router, rows_per_mod):
    m, d = x2d.shape
    assert d == SUBLANES * LANES
    n_experts = router.shape[1]
    wr32 = jnp.zeros((d, LANES), F32).at[:, :n_experts].set(router)
    wr = wr32.astype(BF16)
    wrl = (wr32 - wr.astype(F32)).astype(BF16)
    tm = _pick_tile(rows_per_mod, 512, 8)
    tiles_per_mod = rows_per_mod // tm
    mod = pl.BlockSpec((None, 1, d), lambda i: (i // tiles_per_mod, 0, 0))
    rspec = pl.BlockSpec((d, LANES), lambda i: (0, 0))
    return pl.pallas_call(
        functools.partial(_route_kernel, n_experts=n_experts),
        grid=(m // tm,),
        in_specs=[pl.BlockSpec((tm, d), lambda i: (i, 0)), mod, mod, rspec, rspec],
        out_specs=[pl.BlockSpec((tm * SUBLANES, LANES), lambda i: (i, 0)),
                   pl.BlockSpec((tm, LANES), lambda i: (i, 0))],
        out_shape=[jax.ShapeDtypeStruct((m * SUBLANES, LANES), F32), jax.ShapeDtypeStruct((m, LANES), F32)],
        compiler_params=_params("parallel"),
        name="moe_route",
    )(x2d, shift, scale, wr, wrl)


EXPERT_TILE_ROWS = 1024


def _expert_ffn_kernel(te_ref, nv_ref, src_ref, h_hbm, w1_ref, w3_ref, w2_ref, o_ref, xbuf, sem, h_scr, acc_scr):
    t = pl.program_id(0)
    f = pl.program_id(1)
    nv = nv_ref[0]
    tm = h_scr.shape[0]

    def issue_rows(tile, slot):
        base = tile * tm

        def body(i, carry):
            src_row = pl.multiple_of(src_ref[base + i] * SUBLANES, SUBLANES)
            dst_row = pl.multiple_of(i * SUBLANES, SUBLANES)
            pltpu.make_async_copy(h_hbm.at[pl.ds(src_row, SUBLANES), :],
                                  xbuf.at[slot, pl.ds(dst_row, SUBLANES), :], sem.at[slot]).start()
            return carry

        lax.fori_loop(0, tm, body, 0, unroll=8)

    def wait_rows(slot):
        pltpu.make_async_copy(h_hbm.at[pl.ds(0, tm * SUBLANES), :], xbuf.at[slot], sem.at[slot]).wait()

    @pl.when(t < nv)
    def _():
        @pl.when(f == 0)
        def _():
            slot = t % 2

            @pl.when(t == 0)
            def _():
                issue_rows(0, 0)

            wait_rows(slot)
            for s in range(SUBLANES):
                h_scr[:, s * LANES:(s + 1) * LANES] = _from_token_tiles(xbuf.at[slot], tm, s).astype(BF16)
            acc_scr[...] = jnp.zeros_like(acc_scr)

            @pl.when(t + 1 < nv)
            def _():
                issue_rows(t + 1, 1 - slot)

        h = h_scr[...]
        a = jnp.dot(h, w1_ref[...].astype(BF16), preferred_element_type=F32)
        b = jnp.dot(h, w3_ref[...].astype(BF16), preferred_element_type=F32)
        acc_scr[...] += jnp.dot((_silu(a) * b).astype(BF16), w2_ref[...].astype(BF16),
                                preferred_element_type=F32)

        @pl.when(f == pl.num_programs(1) - 1)
        def _():
            _to_token_tiles(o_ref, acc_scr[...])

    @pl.when(jnp.logical_and(t >= nv, f == pl.num_programs(1) - 1))
    def _():
        o_ref[...] = jnp.zeros_like(o_ref)


def _expert_ffn(h, src, tile_expert, n_valid, w1, w3, w2):
    tm = EXPERT_TILE_ROWS
    n = src.shape[0]
    d = w1.shape[1]
    dfe = w1.shape[2]
    tf = _pick_tile(dfe, 512, LANES)
    nf = dfe // tf

    def fsel(t, f, nv):
        return jnp.where(t < nv[0], f, nf - 1)

    return pl.pallas_call(
        _expert_ffn_kernel,
        grid_spec=pltpu.PrefetchScalarGridSpec(
            num_scalar_prefetch=3,
            grid=(n // tm, nf),
            in_specs=[pl.BlockSpec(memory_space=pl.ANY),
                      pl.BlockSpec((None, d, tf), lambda t, f, te, nv, sr: (te[t], 0, fsel(t, f, nv))),
                      pl.BlockSpec((None, d, tf), lambda t, f, te, nv, sr: (te[t], 0, fsel(t, f, nv))),
                      pl.BlockSpec((None, tf, d), lambda t, f, te, nv, sr: (te[t], fsel(t, f, nv), 0))],
            out_specs=pl.BlockSpec((tm * SUBLANES, LANES), lambda t, f, te, nv, sr: (t, 0)),
            scratch_shapes=[pltpu.VMEM((2, tm * SUBLANES, LANES), F32), pltpu.SemaphoreType.DMA((2,)),
                            pltpu.VMEM((tm, d), BF16), pltpu.VMEM((tm, d), F32)],
        ),
        out_shape=jax.ShapeDtypeStruct((n * SUBLANES, LANES), F32),
        compiler_params=_params("arbitrary", "arbitrary"),
        name="expert_swiglu",
    )(tile_expert, n_valid, src, h, w1, w3, w2)


def _combine_kernel(pos_ref, x_ref, g_ref, r_ref, ys_hbm, o_ref, ybuf, sem):
    i = pl.program_id(0)
    n = pl.num_programs(0)
    tm = x_ref.shape[0]
    m = n * tm

    def issue_rows(tile, slot):
        for k in range(TOP_K):
            base = k * m + tile * tm

            def body(j, carry):
                src_row = pl.multiple_of(pos_ref[base + j] * SUBLANES, SUBLANES)
                dst_row = pl.multiple_of(j * SUBLANES, SUBLANES)
                pltpu.make_async_copy(ys_hbm.at[pl.ds(src_row, SUBLANES), :],
                                      ybuf.at[slot, k, pl.ds(dst_row, SUBLANES), :], sem.at[slot]).start()
                return carry

            lax.fori_loop(0, tm, body, 0, unroll=8)

    slot = i % 2

    @pl.when(i == 0)
    def _():
        issue_rows(0, 0)

    pltpu.make_async_copy(ybuf.at[1 - slot], ybuf.at[slot], sem.at[slot]).wait()

    @pl.when(i + 1 < n)
    def _():
        issue_rows(i + 1, 1 - slot)

    r = r_ref[...]
    w1, w2 = r[:, 2:3], r[:, 3:4]
    for s in range(SUBLANES):
        lanes = slice(s * LANES, (s + 1) * LANES)
        mix = (w1 * _from_token_tiles(ybuf.at[slot, 0], tm, s) + w2 * _from_token_tiles(ybuf.at[slot, 1], tm, s))
        o_ref[:, lanes] = x_ref[:, lanes] + g_ref[:, lanes] * mix


def _combine(x2d, gate, route, ys, pos, rows_per_mod):
    m, d = x2d.shape
    tm = _pick_tile(rows_per_mod, 512, 8)
    tiles_per_mod = rows_per_mod // tm
    return pl.pallas_call(
        _combine_kernel,
        grid_spec=pltpu.PrefetchScalarGridSpec(
            num_scalar_prefetch=1,
            grid=(m // tm,),
            in_specs=[pl.BlockSpec((tm, d), lambda i, ps: (i, 0)),
                      pl.BlockSpec((None, 1, d), lambda i, ps: (i // tiles_per_mod, 0, 0)),
                      pl.BlockSpec((tm, LANES), lambda i, ps: (i, 0)),
                      pl.BlockSpec(memory_space=pl.ANY)],
            out_specs=pl.BlockSpec((tm, d), lambda i, ps: (i, 0)),
            scratch_shapes=[pltpu.VMEM((2, TOP_K, tm * SUBLANES, LANES), F32), pltpu.SemaphoreType.DMA((2,))],
        ),
        out_shape=jax.ShapeDtypeStruct((m, d), F32),
        compiler_params=_params("arbitrary"),
        name="moe_combine",
    )(pos, x2d, gate, route, ys)


def _moe_ffn(x2d, shift, scale, gate, router, w1, w3, w2, rows_per_mod):
    m, d = x2d.shape
    n_experts = router.shape[1]
    h, route = _route(x2d, shift, scale, router, rows_per_mod)
    tm = EXPERT_TILE_ROWS
    expert = route[:, :TOP_K].astype(jnp.int32).T.reshape(-1)
    onehot = (expert[:, None] == jnp.arange(n_experts)[None, :]).astype(jnp.int32)
    csum = jnp.cumsum(onehot, axis=0)
    rank = jnp.sum(onehot * (csum - 1), axis=1)
    tiles = (csum[-1] + tm - 1) // tm
    tile_end = jnp.cumsum(tiles)
    pos = (jnp.sum(onehot * ((tile_end - tiles) * tm)[None, :], axis=1) + rank).astype(jnp.int32)
    n_tiles = (TOP_K * m) // tm + n_experts
    src = jnp.zeros((n_tiles * tm,), jnp.int32).at[pos].set(jnp.arange(TOP_K * m, dtype=jnp.int32) % m)
    tile_expert = jnp.minimum(jnp.sum((jnp.arange(n_tiles)[:, None] >= tile_end[None, :]).astype(jnp.int32), axis=1),
                              n_experts - 1).astype(jnp.int32)
    n_valid = tile_end[-1:].astype(jnp.int32)
    ys = _expert_ffn(h, src, tile_expert, n_valid, w1, w3, w2)
    return _combine(x2d, gate, route, ys, pos, rows_per_mod)


def kernel(x, c, ctx, c_ctx, w_mod, b_mod, w_in, conv_w, conv_b, rg_lambda, rg_w, rg_b, q_gain, k_gain, rpb,
           w_rnn_o, w_na_o, w_out, ffn_w1, ffn_w3, ffn_w2, router, moe_w1, moe_w3, moe_w2):
    nb, seq, d = x.shape
    cl = ctx.shape[1]
    depth = w_mod.shape[0]
    c_rnn = conv_w.shape[2]
    d_att = rpb.shape[1] * HEAD_DIM
    ctx_cols = c_rnn + 2 * d_att
    rows = seq // GRID_W

    n_cond = -(-(nb + 1) // 8) * 8
    cond = jnp.zeros((n_cond, d), F32).at[:nb].set(c).at[nb].set(c_ctx)
    mods = _adaln(cond, w_mod, b_mod)

    x2 = x.reshape(nb * seq, d)
    xc2 = ctx.reshape(nb * cl, d)
    for l in range(depth):
        ctx_out = l < depth - 1
        lat = [mods[l, :nb, k * d:(k + 1) * d].reshape(nb, 1, d) for k in range(N_MOD)]
        cmod = [mods[l, nb:nb + 1, k * d:(k + 1) * d].reshape(1, 1, d) for k in range(N_MOD)]
        w_in_l = w_in[l].astype(BF16)
        tile_heads = lambda g: jnp.tile(g.astype(F32), c_rnn // HEAD_DIM).reshape(1, c_rnn)
        head_gains = jnp.stack([tile_heads(k_gain[l]), tile_heads(q_gain[l]) * Q_SCALE])

        z_lat = _norm_matmul(x2, lat[0], lat[1], w_in_l, seq, head_gains, c_rnn, (1, 4))
        if ctx_out:
            z_ctx = _norm_matmul(xc2, cmod[0], cmod[1], w_in_l, nb * cl, head_gains, c_rnn, (1, 4))
        else:
            z_ctx = _norm_matmul(xc2, cmod[0], cmod[1], w_in_l[:, :ctx_cols], nb * cl, head_gains, c_rnn, (1,))
        ncl, ncc = z_lat.shape[1], z_ctx.shape[1]
        z_lat3 = z_lat.reshape(nb, seq, ncl)
        z_ctx3 = z_ctx.reshape(nb, cl, ncc)

        xr = jnp.concatenate([jnp.transpose(z_ctx3[:, :, :c_rnn], (1, 0, 2)),
                              jnp.transpose(z_lat3[:, :, :c_rnn], (1, 0, 2))], axis=0)
        wg_f, bg_f = _gate_weights(rg_w[l, 0], rg_b[l, 0])
        wg_r, bg_r = _gate_weights(rg_w[l, 1], rg_b[l, 1])
        hs_r = _rnn_scan(xr, conv_w[l], conv_b[l], wg_r, bg_r, rg_lambda[l, 1], cl, True, None)
        hs = _rnn_scan(xr, conv_w[l], conv_b[l], wg_f, bg_f, rg_lambda[l, 0], cl, False, hs_r)
        hs_lat = jnp.transpose(hs[cl:], (1, 0, 2)).reshape(nb * seq, c_rnn)

        bias = _na_bias_table(rpb[l], rows)
        na_lat = _neighbourhood_attention(z_lat3, z_ctx3, bias, seq)

        wr, wn, wo = w_rnn_o[l].astype(BF16), w_na_o[l].astype(BF16), w_out[l].astype(BF16)
        x2 = _merge(x2, lat[2], hs_lat, z_lat, na_lat.reshape(nb * seq, d_att), wr, wn, wo, seq)
        if ctx_out:
            hs_ctx = jnp.transpose(hs[:cl], (1, 0, 2)).reshape(nb * cl, c_rnn)
            na_ctx = _context_attention(z_ctx3, d_att)
            xc2 = _merge(xc2, cmod[2], hs_ctx, z_ctx, na_ctx.reshape(nb * cl, d_att), wr, wn, wo, nb * cl)

        j = l // 2
        if l % 2 == 0:
            w1, w3, w2 = ffn_w1[j].astype(BF16), ffn_w3[j].astype(BF16), ffn_w2[j].astype(BF16)
            x2 = _dense_ffn(x2, lat[3], lat[4], lat[5], w1, w3, w2, seq)
            if ctx_out:
                xc2 = _dense_ffn(xc2, cmod[3], cmod[4], cmod[5], w1, w3, w2, nb * cl)
        else:
            w1, w3, w2 = moe_w1[j], moe_w3[j], moe_w2[j]
            x2 = _moe_ffn(x2, lat[3], lat[4], lat[5], router[j], w1, w3, w2, seq)
            if ctx_out:
                xc2 = _moe_ffn(xc2, cmod[3], cmod[4], cmod[5], router[j], w1, w3, w2, nb * cl)
    return x2.reshape(nb, seq, d)
```

```python
import functools

import numpy as np
import jax
import jax.numpy as jnp
from jax import lax
from jax.experimental import pallas as pl
from jax.experimental.pallas import tpu as pltpu

F32 = jnp.float32
BF16 = jnp.bfloat16

EPS = 1e-6
N_MOD = 6
GRID_W = 64
WIN_H = 8
WIN_W = 16
HEAD_DIM = 64
RNN_BW = 64
CONV_TAPS = 4
CONV_LEFT = 2
LRU_C = 8.0
TOP_K = 2
MASK_VALUE = -1e30
SQRT_FLOOR = 1e-30

LANES = 128
MXU_DIM = 256
VMEM_LIMIT_BYTES = 56 * 1024 * 1024

NA_GROUP_ROWS = 4
NA_KEY_ROWS = 12


def _params(*sem):
    return pltpu.CompilerParams(dimension_semantics=sem, vmem_limit_bytes=VMEM_LIMIT_BYTES)


def _sigmoid(x):
    return 0.5 * (jnp.tanh(0.5 * x) + 1.0)


def _silu(x):
    return x * _sigmoid(x)


def _gelu_tanh(x):
    return 0.5 * x * (1.0 + jnp.tanh(np.sqrt(2.0 / np.pi) * (x + 0.044715 * (x * x * x))))


def _modulated_norm(x, shift, scale):
    ms = jnp.mean(x * x, axis=-1, keepdims=True)
    return x * lax.rsqrt(ms + EPS) * (1.0 + scale) + shift


def _pick_tile(n, cap, mult):
    best = None
    for t in range(mult, min(n, cap) + 1, mult):
        if n % t == 0:
            best = t
    assert best is not None, (n, cap, mult)
    return best


def _adaln_kernel(c_ref, w_ref, b_ref, o_ref):
    s = _silu(c_ref[...]).astype(BF16)
    o_ref[...] = jnp.dot(s, w_ref[...].astype(BF16), preferred_element_type=F32) + b_ref[...]


def _adaln(cond, w_mod, b_mod):
    depth, d, n = w_mod.shape
    r = cond.shape[0]
    tn = _pick_tile(n, 1536, LANES)
    return pl.pallas_call(
        _adaln_kernel,
        grid=(depth, n // tn),
        in_specs=[
            pl.BlockSpec((r, d), lambda l, j: (0, 0)),
            pl.BlockSpec((None, d, tn), lambda l, j: (l, 0, j)),
            pl.BlockSpec((None, 1, tn), lambda l, j: (l, 0, j)),
        ],
        out_specs=pl.BlockSpec((None, r, tn), lambda l, j: (l, 0, j)),
        out_shape=jax.ShapeDtypeStruct((depth, r, n), F32),
        compiler_params=_params("parallel", "parallel"),
        name="adaln",
    )(cond, w_mod, b_mod.reshape(depth, 1, n))


def _head_norm(x, gain, in_a):
    sq = x * x
    sa = jnp.sum(jnp.where(in_a, sq, 0.0), axis=-1, keepdims=True)
    sb = jnp.sum(jnp.where(in_a, 0.0, sq), axis=-1, keepdims=True)
    ms = jnp.where(in_a, sa, sb) * (1.0 / HEAD_DIM)
    return x * lax.rsqrt(ms + EPS) * gain


def _norm_matmul_kernel(x_ref, sh_ref, sc_ref, w_ref, hg_ref, o_ref, *, seg, normed):
    h = _modulated_norm(x_ref[...], sh_ref[...], sc_ref[...]).astype(BF16)
    in_a = lax.broadcasted_iota(jnp.int32, (1, LANES), 1) < HEAD_DIM
    for j in range(w_ref.shape[1] // seg):
        acc = jnp.dot(h, w_ref[:, j * seg:(j + 1) * seg], preferred_element_type=F32)
        if j in normed:
            gain = hg_ref[normed.index(j)]
            for sl in range(seg // LANES):
                lanes = slice(sl * LANES, (sl + 1) * LANES)
                o_ref[:, j * seg + sl * LANES:j * seg + (sl + 1) * LANES] = _head_norm(
                    acc[:, lanes], gain[:, lanes], in_a).astype(o_ref.dtype)
        else:
            o_ref[:, j * seg:(j + 1) * seg] = acc.astype(o_ref.dtype)


def _norm_matmul(x2d, shift, scale, w, rows_per_mod, head_gains, seg, normed):
    m, d = x2d.shape
    n = w.shape[1]
    tm = _pick_tile(rows_per_mod, 512, 8)
    tiles_per_mod = rows_per_mod // tm
    mod_spec = pl.BlockSpec((None, 1, d), lambda i: (i // tiles_per_mod, 0, 0))
    return pl.pallas_call(
        functools.partial(_norm_matmul_kernel, seg=seg, normed=normed),
        grid=(m // tm,),
        in_specs=[
            pl.BlockSpec((tm, d), lambda i: (i, 0)),
            mod_spec,
            mod_spec,
            pl.BlockSpec((d, n), lambda i: (0, 0), pipeline_mode=pl.Buffered(1)),
            pl.BlockSpec(head_gains.shape, lambda i: (0, 0, 0)),
        ],
        out_specs=pl.BlockSpec((tm, n), lambda i: (i, 0)),
        out_shape=jax.ShapeDtypeStruct((m, n), BF16),
        compiler_params=_params("parallel"),
        name="norm_in_proj",
    )(x2d, shift, scale, w, head_gains)


def _scan_tile_index(i, n_ctx_tiles, n_tiles, reverse):
    if not reverse:
        return i
    return jnp.where(i < n_ctx_tiles, n_ctx_tiles - 1 - i, n_tiles - 1 - (i - n_ctx_tiles))


def _rnn_kernel(xp_ref, x_ref, xn_ref, cw_ref, cb_ref, wg_ref, bg_ref, lam_ref, *rest,
                n_ctx_tiles, n_tiles, reverse, add_prev):
    if add_prev:
        prev_ref, out_ref, a_scr, u_scr, h_scr = rest
    else:
        out_ref, a_scr, u_scr, h_scr = rest
    tl, nb, c = x_ref.shape
    i = pl.program_id(0)
    ti = _scan_tile_index(i, n_ctx_tiles, n_tiles, reverse)

    @pl.when(i == 0)
    def _():
        h_scr[...] = jnp.zeros_like(h_scr)

    at_start = jnp.logical_or(ti == 0, ti == n_ctx_tiles)
    at_end = jnp.logical_or(ti == n_ctx_tiles - 1, ti == n_tiles - 1)
    xp = jnp.where(at_start, 0.0, xp_ref[...].astype(F32))
    xn = jnp.where(at_end, 0.0, xn_ref[...].astype(F32))
    xe = jnp.concatenate([xp, x_ref[...].astype(F32), xn], axis=0)
    xc = cb_ref[...] + xe[0:tl] * cw_ref[0:1, :]
    for k in range(1, CONV_TAPS):
        xc = xc + xe[k:k + tl] * cw_ref[k:k + 1, :]
    xc = xc.reshape(tl * nb, c)

    lam = lam_ref[...]
    log_sig = jnp.minimum(lam, 0.0) - jnp.log(1.0 + jnp.exp(-jnp.abs(lam)))
    k2 = (0.5 * LRU_C * np.log2(np.e)) * log_sig
    for j in range(c // MXU_DIM):
        sl = slice(MXU_DIM * j, MXU_DIM * (j + 1))
        xj = xc[:, sl]
        t = jnp.tanh(jnp.dot(xj.astype(BF16), wg_ref[j], preferred_element_type=F32) + bg_ref[j])
        a = jnp.exp2(k2[:, sl] * t[:, :MXU_DIM] + k2[:, sl])
        v = 1.0 - a * a
        gated_x = (t[:, MXU_DIM:] + 1.0) * (0.5 * xj)
        a_scr[:, sl] = a
        u_scr[:, sl] = (v * lax.rsqrt(jnp.maximum(v, SQRT_FLOOR))) * gated_x

    def step(s, h):
        tt = (tl - 1 - s) if reverse else s
        row = pl.multiple_of(tt * nb, nb)
        h = a_scr[pl.ds(row, nb), :] * h + u_scr[pl.ds(row, nb), :]
        val = h + prev_ref[tt].astype(F32) if add_prev else h
        out_ref[tt] = val.astype(out_ref.dtype)
        return h

    h_scr[...] = lax.fori_loop(0, tl, step, h_scr[...], unroll=4)


def _rnn_scan(xr, conv_w, conv_b, wg, bg, lam, n_ctx, reverse, prev):
    t, nb, c = xr.shape
    tl = _pick_tile(n_ctx, 64, 2)
    assert t % tl == 0 and n_ctx % tl == 0
    n_tiles, n_ctx_tiles = t // tl, n_ctx // tl
    tile = functools.partial(_scan_tile_index, n_ctx_tiles=n_ctx_tiles, n_tiles=n_tiles, reverse=reverse)
    half = tl // CONV_LEFT
    full = lambda shape: pl.BlockSpec(shape, lambda i: (0,) * len(shape))
    in_specs = [
        pl.BlockSpec((CONV_LEFT, nb, c), lambda i: (jnp.maximum(tile(i) * half - 1, 0), 0, 0)),
        pl.BlockSpec((tl, nb, c), lambda i: (tile(i), 0, 0)),
        pl.BlockSpec((1, nb, c), lambda i: (jnp.minimum((tile(i) + 1) * tl, t - 1), 0, 0)),
        full((CONV_TAPS, c)),
        full((1, c)),
        full(wg.shape),
        full(bg.shape),
        full((1, c)),
    ]
    args = [xr, xr, xr, conv_w, conv_b.reshape(1, c), wg, bg, lam.reshape(1, c)]
    if prev is not None:
        in_specs.append(pl.BlockSpec((tl, nb, c), lambda i: (tile(i), 0, 0)))
        args.append(prev)
    kern = functools.partial(_rnn_kernel, n_ctx_tiles=n_ctx_tiles, n_tiles=n_tiles, reverse=reverse,
                             add_prev=prev is not None)
    return pl.pallas_call(
        kern,
        grid=(n_tiles,),
        in_specs=in_specs,
        out_specs=pl.BlockSpec((tl, nb, c), lambda i: (tile(i), 0, 0)),
        out_shape=jax.ShapeDtypeStruct((t, nb, c), BF16),
        scratch_shapes=[pltpu.VMEM((tl * nb, c), F32), pltpu.VMEM((tl * nb, c), F32), pltpu.VMEM((nb, c), F32)],
        compiler_params=_params("arbitrary"),
        name="rglru_scan_rev" if reverse else "rglru_scan_fwd",
    )(*args)


def _gate_weights(rg_w_d, rg_b_d):
    _, nblk, bw, _ = rg_w_d.shape
    per = MXU_DIM // bw
    ngrp = nblk // per
    w = rg_w_d.reshape(2, ngrp, per, bw, bw)
    eye = jnp.eye(per, dtype=rg_w_d.dtype)
    dense = w[:, :, :, :, None, :] * eye[None, None, :, None, :, None]
    dense = dense.reshape(2, ngrp, MXU_DIM, MXU_DIM)
    wg = jnp.concatenate([dense[0], dense[1]], axis=-1).astype(BF16)
    b = rg_b_d.reshape(2, ngrp, 1, MXU_DIM)
    bg = jnp.concatenate([b[0], b[1]], axis=-1)
    return wg * 0.5, bg * 0.5


Q_SCALE = (HEAD_DIM ** -0.5) * float(np.log2(np.e))


def _dot_nt(a, b):
    return lax.dot_general(a, b, (((1,), (1,)), ((), ())), preferred_element_type=F32)


def _stack_heads(q, in_a):
    zero = jnp.zeros_like(q)
    return jnp.concatenate([jnp.where(in_a, q, zero), jnp.where(in_a, zero, q)], axis=0)


def _na_kernel(q_ref, k_ref, v_ref, kc_ref, vc_ref, bias_ref, o_ref, sa_scr, sb_scr, *, rows):
    cl = kc_ref.shape[0]
    gq = NA_GROUP_ROWS * GRID_W
    gk = NA_KEY_ROWS * GRID_W
    ngrp = rows // NA_GROUP_ROWS
    in_a = lax.broadcasted_iota(jnp.int32, (1, LANES), 1) < HEAD_DIM

    def key_base(g):
        kb = jnp.clip(NA_GROUP_ROWS * g - WIN_H // 2, 0, rows - NA_KEY_ROWS)
        return pl.multiple_of(kb * GRID_W, GRID_W)

    def scores(g, s_scr):
        cls = jnp.where(g == 0, 0, jnp.where(g == ngrp - 1, 2, 1))
        q2 = _stack_heads(q_ref[pl.ds(pl.multiple_of(g * gq, gq), gq), :], in_a)
        s_scr[:, 0:gk] = _dot_nt(q2, k_ref[pl.ds(key_base(g), gk), :]) + bias_ref[cls]
        s_scr[:, gk:gk + cl] = _dot_nt(q2, kc_ref[...])

    def attend(g, s_scr):
        sc = s_scr[...]
        p = jnp.exp2(sc - jnp.max(sc, axis=-1, keepdims=True))
        den = jnp.sum(p, axis=-1, keepdims=True)
        pb = p.astype(BF16)
        o2 = (jnp.dot(pb[:, 0:gk], v_ref[pl.ds(key_base(g), gk), :], preferred_element_type=F32)
              + jnp.dot(pb[:, gk:gk + cl], vc_ref[...], preferred_element_type=F32)) / den
        o_ref[pl.ds(pl.multiple_of(g * gq, gq), gq), :] = jnp.where(in_a, o2[:gq], o2[gq:]).astype(o_ref.dtype)

    scores(0, sa_scr)

    def pair(it, carry):
        g = 2 * it
        scores(g + 1, sb_scr)
        attend(g, sa_scr)
        scores(g + 2, sa_scr)
        attend(g + 1, sb_scr)
        return carry

    lax.fori_loop(0, ngrp // 2 - 1, pair, 0)
    scores(ngrp - 1, sb_scr)
    attend(ngrp - 2, sa_scr)
    attend(ngrp - 1, sb_scr)


def _na_bias_table(rpb, rows):
    nh, n_ro, n_co = rpb.shape
    ngrp = rows // NA_GROUP_ROWS
    qc = np.arange(GRID_W)[:, None]
    kc = np.arange(GRID_W)[None, :]
    wstart = np.clip(qc - WIN_W // 2, 0, GRID_W - WIN_W)
    col_valid = (kc >= wstart) & (kc < wstart + WIN_W)
    col_onehot = ((kc - qc + WIN_W - 1)[None] == np.arange(n_co)[:, None, None])
    rpb2 = rpb.astype(F32).reshape(nh // 2, 2, n_ro, n_co)
    by_col = jnp.einsum("pgrj,jqk->pgrqk", rpb2, col_onehot.astype(np.float32),
                        precision=lax.Precision.HIGHEST) * np.float32(np.log2(np.e))
    masked = jnp.full((nh // 2, 2, GRID_W, GRID_W), MASK_VALUE, F32)
    classes = []
    for g in (0, 1, ngrp - 1):
        kb = int(np.clip(NA_GROUP_ROWS * g - WIN_H // 2, 0, rows - NA_KEY_ROWS))
        per_row = []
        for a in range(NA_GROUP_ROWS):
            r = NA_GROUP_ROWS * g + a
            start = int(np.clip(r - WIN_H // 2, 0, rows - WIN_H))
            blocks = []
            for cr in range(NA_KEY_ROWS):
                key_row = kb + cr
                if start <= key_row < start + WIN_H:
                    blocks.append(jnp.where(col_valid, by_col[:, :, key_row - r + WIN_H - 1], MASK_VALUE))
                else:
                    blocks.append(masked)
            per_row.append(jnp.concatenate(blocks, axis=-1))
        classes.append(jnp.stack(per_row, axis=2))
    tab = jnp.stack(classes, axis=1)
    return tab.reshape(nh // 2, 3, 2 * NA_GROUP_ROWS * GRID_W, NA_KEY_ROWS * GRID_W)


def _neighbourhood_attention(z_lat, z_ctx, bias, seq):
    nb, s, _ = z_lat.shape
    cl = z_ctx.shape[1]
    npair = bias.shape[0]
    d_att = npair * LANES
    cb = npair
    rows = s // GRID_W
    assert s % GRID_W == 0 and rows % (2 * NA_GROUP_ROWS) == 0 and rows >= NA_KEY_ROWS + NA_GROUP_ROWS
    s_shape = (2 * NA_GROUP_ROWS * GRID_W, NA_KEY_ROWS * GRID_W + cl)
    lat = lambda seg: pl.BlockSpec((None, s, LANES), lambda p, b: (b, 0, seg * cb + p))
    ctx = lambda seg: pl.BlockSpec((None, cl, LANES), lambda p, b: (b, 0, seg * cb + p))
    return pl.pallas_call(
        functools.partial(_na_kernel, rows=rows),
        grid=(npair, nb),
        in_specs=[lat(4), lat(1), lat(2), ctx(1), ctx(2),
                  pl.BlockSpec((None,) + bias.shape[1:], lambda p, b: (p, 0, 0, 0))],
        out_specs=pl.BlockSpec((None, s, LANES), lambda p, b: (b, 0, p)),
        out_shape=jax.ShapeDtypeStruct((nb, s, d_att), BF16),
        scratch_shapes=[pltpu.VMEM(s_shape, F32), pltpu.VMEM(s_shape, F32)],
        compiler_params=_params("parallel", "parallel"),
        name="neighbourhood_attention",
    )(z_lat, z_lat, z_lat, z_ctx, z_ctx, bias)


def _ctx_attn_kernel(q_ref, k_ref, v_ref, o_ref):
    in_a = lax.broadcasted_iota(jnp.int32, (1, LANES), 1) < HEAD_DIM
    n = q_ref.shape[0]
    sc = _dot_nt(_stack_heads(q_ref[...], in_a), k_ref[...])
    p = jnp.exp2(sc - jnp.max(sc, axis=-1, keepdims=True))
    den = jnp.sum(p, axis=-1, keepdims=True)
    o2 = jnp.dot(p.astype(BF16), v_ref[...], preferred_element_type=F32) / den
    o_ref[...] = jnp.where(in_a, o2[:n], o2[n:]).astype(o_ref.dtype)


def _context_attention(z_ctx, d_att):
    nb, cl, _ = z_ctx.shape
    cb = d_att // LANES
    ctx = lambda seg: pl.BlockSpec((None, cl, LANES), lambda p, b: (b, 0, seg * cb + p))
    return pl.pallas_call(
        _ctx_attn_kernel,
        grid=(cb, nb),
        in_specs=[ctx(4), ctx(1), ctx(2)],
        out_specs=pl.BlockSpec((None, cl, LANES), lambda p, b: (b, 0, p)),
        out_shape=jax.ShapeDtypeStruct((nb, cl, d_att), BF16),
        compiler_params=_params("parallel", "parallel"),
        name="context_attention",
    )(z_ctx, z_ctx, z_ctx)


def _merge_kernel(x_ref, g_ref, hs_ref, y_ref, na_ref, gr_ref, gn_ref, wr_ref, wn_ref, wo_ref, o_ref):
    y_rnn = (hs_ref[...].astype(F32) * _gelu_tanh(y_ref[...].astype(F32))).astype(BF16)
    t_rnn = jnp.dot(y_rnn, wr_ref[...], preferred_element_type=F32)
    t_na = jnp.dot(na_ref[...], wn_ref[...], preferred_element_type=F32)
    mix = _sigmoid(gr_ref[...].astype(F32)) * t_rnn + _sigmoid(gn_ref[...].astype(F32)) * t_na
    out = jnp.dot(mix.astype(BF16), wo_ref[...], preferred_element_type=F32)
    o_ref[...] = x_ref[...] + g_ref[...] * out


def _merge(x2d, gate, hs, z, na, w_rnn_o, w_na_o, w_out, rows_per_mod):
    m, d = x2d.shape
    c = hs.shape[1]
    da = na.shape[1]
    tm = _pick_tile(rows_per_mod, 512, 8)
    tiles_per_mod = rows_per_mod // tm
    y_blk = (c + 2 * da) // c
    gr_blk = (2 * c + 3 * da) // d
    gn_blk = gr_blk + 1
    row = lambda width, blk=0: pl.BlockSpec((tm, width), lambda i: (i, blk))
    whole = lambda w: pl.BlockSpec(w.shape, lambda i: (0, 0))
    return pl.pallas_call(
        _merge_kernel,
        grid=(m // tm,),
        in_specs=[row(d), pl.BlockSpec((None, 1, d), lambda i: (i // tiles_per_mod, 0, 0)),
                  row(c), row(c, y_blk), row(da), row(d, gr_blk), row(d, gn_blk),
                  whole(w_rnn_o), whole(w_na_o), whole(w_out)],
        out_specs=row(d),
        out_shape=jax.ShapeDtypeStruct((m, d), F32),
        compiler_params=_params("parallel"),
        name="merge_out_proj",
    )(x2d, gate, hs, z, na, z, z, w_rnn_o, w_na_o, w_out)


def _ffn_kernel(x_ref, sh_ref, sc_ref, g_ref, w1_ref, w3_ref, w2_ref, o_ref, h_scr, acc_scr):
    f = pl.program_id(1)

    @pl.when(f == 0)
    def _():
        h_scr[...] = _modulated_norm(x_ref[...], sh_ref[...], sc_ref[...]).astype(BF16)
        acc_scr[...] = jnp.zeros_like(acc_scr)

    h = h_scr[...]
    a = jnp.dot(h, w1_ref[...], preferred_element_type=F32)
    b = jnp.dot(h, w3_ref[...], preferred_element_type=F32)
    acc_scr[...] += jnp.dot((_silu(a) * b).astype(BF16), w2_ref[...], preferred_element_type=F32)

    @pl.when(f == pl.num_programs(1) - 1)
    def _():
        o_ref[...] = x_ref[...] + g_ref[...] * acc_scr[...]


def _dense_ffn(x2d, shift, scale, gate, w1, w3, w2, rows_per_mod):
    m, d = x2d.shape
    dff = w1.shape[1]
    tm = _pick_tile(rows_per_mod, 512, 8)
    tf = _pick_tile(dff, 1536, LANES)
    tiles_per_mod = rows_per_mod // tm
    mod = pl.BlockSpec((None, 1, d), lambda i, f: (i // tiles_per_mod, 0, 0))
    return pl.pallas_call(
        _ffn_kernel,
        grid=(m // tm, dff // tf),
        in_specs=[pl.BlockSpec((tm, d), lambda i, f: (i, 0)), mod, mod, mod,
                  pl.BlockSpec((d, tf), lambda i, f: (0, f)),
                  pl.BlockSpec((d, tf), lambda i, f: (0, f)),
                  pl.BlockSpec((tf, d), lambda i, f: (f, 0))],
        out_specs=pl.BlockSpec((tm, d), lambda i, f: (i, 0)),
        out_shape=jax.ShapeDtypeStruct((m, d), F32),
        scratch_shapes=[pltpu.VMEM((tm, d), BF16), pltpu.VMEM((tm, d), F32)],
        compiler_params=_params("parallel", "arbitrary"),
        name="dense_swiglu",
    )(x2d, shift, scale, gate, w1, w3, w2)


SUBLANES = 8
GATHER_DMA_PRIORITY = 1


def _to_token_tiles(ref, x):
    n = x.shape[0]
    for s in range(SUBLANES):
        ref[pl.ds(s, n, stride=SUBLANES), :] = x[:, s * LANES:(s + 1) * LANES]


def _from_token_tiles(ref, n, s):
    return ref[pl.ds(s, n, stride=SUBLANES), :]


def _route_kernel(x_ref, sh_ref, sc_ref, wr_ref, wrl_ref, h_ref, r_ref, *, n_experts):
    h = _modulated_norm(x_ref[...], sh_ref[...], sc_ref[...])
    _to_token_tiles(h_ref, h)
    h_hi = h.astype(BF16)
    h_lo = (h - h_hi.astype(F32)).astype(BF16)
    logits = (jnp.dot(h_hi, wr_ref[...], preferred_element_type=F32)
              + jnp.dot(h_lo, wr_ref[...], preferred_element_type=F32)
              + jnp.dot(h_hi, wrl_ref[...], preferred_element_type=F32))
    lane = lax.broadcasted_iota(jnp.int32, logits.shape, 1).astype(F32)
    neg = -jnp.inf
    lg = jnp.where(lane < n_experts, logits, neg)
    m1 = jnp.max(lg, axis=-1, keepdims=True)
    i1 = jnp.min(jnp.where(lg == m1, lane, float(LANES)), axis=-1, keepdims=True)
    lg2 = jnp.where(lane == i1, neg, lg)
    m2 = jnp.max(lg2, axis=-1, keepdims=True)
    i2 = jnp.min(jnp.where(lg2 == m2, lane, float(LANES)), axis=-1, keepdims=True)
    e = jnp.exp(m2 - m1)
    w1 = 1.0 / (1.0 + e)
    w2 = e / (1.0 + e)
    r_ref[...] = jnp.where(lane == 0, i1, jnp.where(lane == 1, i2, jnp.where(lane == 2, w1,
                           jnp.where(lane == 3, w2, 0.0))))


def _route(x2d, shift, scale, router, rows_per_mod):
    m, d = x2d.shape
    assert d == SUBLANES * LANES
    n_experts = router.shape[1]
    wr32 = jnp.zeros((d, LANES), F32).at[:, :n_experts].set(router)
    wr = wr32.astype(BF16)
    wrl = (wr32 - wr.astype(F32)).astype(BF16)
    tm = _pick_tile(rows_per_mod, 512, 8)
    tiles_per_mod = rows_per_mod // tm
    mod = pl.BlockSpec((None, 1, d), lambda i: (i // tiles_per_mod, 0, 0))
    rspec = pl.BlockSpec((d, LANES), lambda i: (0, 0))
    return pl.pallas_call(
        functools.partial(_route_kernel, n_experts=n_experts),
        grid=(m // tm,),
        in_specs=[pl.BlockSpec((tm, d), lambda i: (i, 0)), mod, mod, rspec, rspec],
        out_specs=[pl.BlockSpec((tm * SUBLANES, LANES), lambda i: (i, 0)),
                   pl.BlockSpec((tm, LANES), lambda i: (i, 0))],
        out_shape=[jax.ShapeDtypeStruct((m * SUBLANES, LANES), F32), jax.ShapeDtypeStruct((m, LANES), F32)],
        compiler_params=_params("parallel"),
        name="moe_route",
    )(x2d, shift, scale, wr, wrl)


EXPERT_TILE_ROWS = 1024


def _expert_ffn_kernel(te_ref, nv_ref, src_ref, h_hbm, w1_ref, w3_ref, w2_ref, o_ref, xbuf, sem, h_scr, acc_scr):
    t = pl.program_id(0)
    f = pl.program_id(1)
    nv = nv_ref[0]
    tm = h_scr.shape[0]

    def issue_rows(tile, slot):
        base = tile * tm

        def body(i, carry):
            src_row = pl.multiple_of(src_ref[base + i] * SUBLANES, SUBLANES)
            dst_row = pl.multiple_of(i * SUBLANES, SUBLANES)
            pltpu.make_async_copy(h_hbm.at[pl.ds(src_row, SUBLANES), :],
                                  xbuf.at[slot, pl.ds(dst_row, SUBLANES), :],
                                  sem.at[slot]).start(priority=GATHER_DMA_PRIORITY)
            return carry

        lax.fori_loop(0, tm, body, 0, unroll=8)

    def wait_rows(slot):
        pltpu.make_async_copy(h_hbm.at[pl.ds(0, tm * SUBLANES), :], xbuf.at[slot], sem.at[slot]).wait()

    @pl.when(t < nv)
    def _():
        @pl.when(f == 0)
        def _():
            slot = t % 2

            @pl.when(t == 0)
            def _():
                issue_rows(0, 0)

            wait_rows(slot)
            for s in range(SUBLANES):
                h_scr[:, s * LANES:(s + 1) * LANES] = _from_token_tiles(xbuf.at[slot], tm, s).astype(BF16)
            acc_scr[...] = jnp.zeros_like(acc_scr)

            @pl.when(t + 1 < nv)
            def _():
                issue_rows(t + 1, 1 - slot)

        h = h_scr[...]
        a = jnp.dot(h, w1_ref[...].astype(BF16), preferred_element_type=F32)
        b = jnp.dot(h, w3_ref[...].astype(BF16), preferred_element_type=F32)
        acc_scr[...] += jnp.dot((_silu(a) * b).astype(BF16), w2_ref[...].astype(BF16),
                                preferred_element_type=F32)

        @pl.when(f == pl.num_programs(1) - 1)
        def _():
            _to_token_tiles(o_ref, acc_scr[...])

    @pl.when(jnp.logical_and(t >= nv, f == pl.num_programs(1) - 1))
    def _():
        o_ref[...] = jnp.zeros_like(o_ref)


def _expert_ffn(h, src, tile_expert, n_valid, w1, w3, w2):
    tm = EXPERT_TILE_ROWS
    n = src.shape[0]
    d = w1.shape[1]
    dfe = w1.shape[2]
    tf = _pick_tile(dfe, 512, LANES)
    nf = dfe // tf

    def fsel(t, f, nv):
        return jnp.where(t < nv[0], f, nf - 1)

    return pl.pallas_call(
        _expert_ffn_kernel,
        grid_spec=pltpu.PrefetchScalarGridSpec(
            num_scalar_prefetch=3,
            grid=(n // tm, nf),
            in_specs=[pl.BlockSpec(memory_space=pl.ANY),
                      pl.BlockSpec((None, d, tf), lambda t, f, te, nv, sr: (te[t], 0, fsel(t, f, nv))),
                      pl.BlockSpec((None, d, tf), lambda t, f, te, nv, sr: (te[t], 0, fsel(t, f, nv))),
                      pl.BlockSpec((None, tf, d), lambda t, f, te, nv, sr: (te[t], fsel(t, f, nv), 0))],
            out_specs=pl.BlockSpec((tm * SUBLANES, LANES), lambda t, f, te, nv, sr: (t, 0)),
            scratch_shapes=[pltpu.VMEM((2, tm * SUBLANES, LANES), F32), pltpu.SemaphoreType.DMA((2,)),
                            pltpu.VMEM((tm, d), BF16), pltpu.VMEM((tm, d), F32)],
        ),
        out_shape=jax.ShapeDtypeStruct((n * SUBLANES, LANES), F32),
        compiler_params=_params("arbitrary", "arbitrary"),
        name="expert_swiglu",
    )(tile_expert, n_valid, src, h, w1, w3, w2)


def _combine_kernel(pos_ref, x_ref, g_ref, r_ref, ys_hbm, o_ref, ybuf, sem):
    i = pl.program_id(0)
    n = pl.num_programs(0)
    tm = x_ref.shape[0]
    m = n * tm

    def issue_rows(tile, slot):
        for k in range(TOP_K):
            base = k * m + tile * tm

            def body(j, carry):
                src_row = pl.multiple_of(pos_ref[base + j] * SUBLANES, SUBLANES)
                dst_row = pl.multiple_of(j * SUBLANES, SUBLANES)
                pltpu.make_async_copy(ys_hbm.at[pl.ds(src_row, SUBLANES), :],
                                      ybuf.at[slot, k, pl.ds(dst_row, SUBLANES), :],
                                      sem.at[slot]).start(priority=GATHER_DMA_PRIORITY)
                return carry

            lax.fori_loop(0, tm, body, 0, unroll=8)

    slot = i % 2

    @pl.when(i == 0)
    def _():
        issue_rows(0, 0)

    pltpu.make_async_copy(ybuf.at[1 - slot], ybuf.at[slot], sem.at[slot]).wait()

    @pl.when(i + 1 < n)
    def _():
        issue_rows(i + 1, 1 - slot)

    r = r_ref[...]
    w1, w2 = r[:, 2:3], r[:, 3:4]
    for s in range(SUBLANES):
        lanes = slice(s * LANES, (s + 1) * LANES)
        mix = (w1 * _from_token_tiles(ybuf.at[slot, 0], tm, s) + w2 * _from_token_tiles(ybuf.at[slot, 1], tm, s))
        o_ref[:, lanes] = x_ref[:, lanes] + g_ref[:, lanes] * mix


def _combine(x2d, gate, route, ys, pos, rows_per_mod):
    m, d = x2d.shape
    tm = _pick_tile(rows_per_mod, 512, 8)
    tiles_per_mod = rows_per_mod // tm
    return pl.pallas_call(
        _combine_kernel,
        grid_spec=pltpu.PrefetchScalarGridSpec(
            num_scalar_prefetch=1,
            grid=(m // tm,),
            in_specs=[pl.BlockSpec((tm, d), lambda i, ps: (i, 0)),
                      pl.BlockSpec((None, 1, d), lambda i, ps: (i // tiles_per_mod, 0, 0)),
                      pl.BlockSpec((tm, LANES), lambda i, ps: (i, 0)),
                      pl.BlockSpec(memory_space=pl.ANY)],
            out_specs=pl.BlockSpec((tm, d), lambda i, ps: (i, 0)),
            scratch_shapes=[pltpu.VMEM((2, TOP_K, tm * SUBLANES, LANES), F32), pltpu.SemaphoreType.DMA((2,))],
        ),
        out_shape=jax.ShapeDtypeStruct((m, d), F32),
        compiler_params=_params("arbitrary"),
        name="moe_combine",
    )(pos, x2d, gate, route, ys)


def _invert_kernel(pos_ref, src_ref, *, m):
    def clear(i, carry):
        src_ref[i] = 0
        return carry

    lax.fori_loop(0, src_ref.shape[0], clear, 0, unroll=8)
    for k in range(pos_ref.shape[0] // m):
        def put(j, carry):
            src_ref[pos_ref[k * m + j]] = j
            return carry

        lax.fori_loop(0, m, put, 0, unroll=8)


def _invert_positions(pos, n_rows, m):
    return pl.pallas_call(
        functools.partial(_invert_kernel, m=m),
        in_specs=[pl.BlockSpec(memory_space=pltpu.SMEM)],
        out_specs=pl.BlockSpec(memory_space=pltpu.SMEM),
        out_shape=jax.ShapeDtypeStruct((n_rows,), jnp.int32),
        name="moe_invert_positions",
    )(pos)


def _moe_ffn(x2d, shift, scale, gate, router, w1, w3, w2, rows_per_mod):
    m, d = x2d.shape
    n_experts = router.shape[1]
    h, route = _route(x2d, shift, scale, router, rows_per_mod)
    tm = EXPERT_TILE_ROWS
    expert = route[:, :TOP_K].astype(jnp.int32).T.reshape(-1)
    onehot = (expert[:, None] == jnp.arange(n_experts)[None, :]).astype(jnp.int32)
    csum = jnp.cumsum(onehot, axis=0)
    rank = jnp.sum(onehot * (csum - 1), axis=1)
    tiles = (csum[-1] + tm - 1) // tm
    tile_end = jnp.cumsum(tiles)
    pos = (jnp.sum(onehot * ((tile_end - tiles) * tm)[None, :], axis=1) + rank).astype(jnp.int32)
    n_tiles = (TOP_K * m) // tm + n_experts
    src = _invert_positions(pos, n_tiles * tm, m)
    tile_expert = jnp.minimum(jnp.sum((jnp.arange(n_tiles)[:, None] >= tile_end[None, :]).astype(jnp.int32), axis=1),
                              n_experts - 1).astype(jnp.int32)
    n_valid = tile_end[-1:].astype(jnp.int32)
    ys = _expert_ffn(h, src, tile_expert, n_valid, w1, w3, w2)
    return _combine(x2d, gate, route, ys, pos, rows_per_mod)


def kernel(x, c, ctx, c_ctx, w_mod, b_mod, w_in, conv_w, conv_b, rg_lambda, rg_w, rg_b, q_gain, k_gain, rpb,
           w_rnn_o, w_na_o, w_out, ffn_w1, ffn_w3, ffn_w2, router, moe_w1, moe_w3, moe_w2):
    nb, seq, d = x.shape
    cl = ctx.shape[1]
    depth = w_mod.shape[0]
    c_rnn = conv_w.shape[2]
    d_att = rpb.shape[1] * HEAD_DIM
    ctx_cols = c_rnn + 2 * d_att
    rows = seq // GRID_W

    n_cond = -(-(nb + 1) // 8) * 8
    cond = jnp.zeros((n_cond, d), F32).at[:nb].set(c).at[nb].set(c_ctx)
    mods = _adaln(cond, w_mod, b_mod)

    x2 = x.reshape(nb * seq, d)
    xc2 = ctx.reshape(nb * cl, d)
    for l in range(depth):
        ctx_out = l < depth - 1
        lat = [mods[l, :nb, k * d:(k + 1) * d].reshape(nb, 1, d) for k in range(N_MOD)]
        cmod = [mods[l, nb:nb + 1, k * d:(k + 1) * d].reshape(1, 1, d) for k in range(N_MOD)]
        w_in_l = w_in[l].astype(BF16)
        tile_heads = lambda g: jnp.tile(g.astype(F32), c_rnn // HEAD_DIM).reshape(1, c_rnn)
        head_gains = jnp.stack([tile_heads(k_gain[l]), tile_heads(q_gain[l]) * Q_SCALE])

        z_lat = _norm_matmul(x2, lat[0], lat[1], w_in_l, seq, head_gains, c_rnn, (1, 4))
        if ctx_out:
            z_ctx = _norm_matmul(xc2, cmod[0], cmod[1], w_in_l, nb * cl, head_gains, c_rnn, (1, 4))
        else:
            z_ctx = _norm_matmul(xc2, cmod[0], cmod[1], w_in_l[:, :ctx_cols], nb * cl, head_gains, c_rnn, (1,))
        ncl, ncc = z_lat.shape[1], z_ctx.shape[1]
        z_lat3 = z_lat.reshape(nb, seq, ncl)
        z_ctx3 = z_ctx.reshape(nb, cl, ncc)

        xr = jnp.concatenate([jnp.transpose(z_ctx3[:, :, :c_rnn], (1, 0, 2)),
                              jnp.transpose(z_lat3[:, :, :c_rnn], (1, 0, 2))], axis=0)
        wg_f, bg_f = _gate_weights(rg_w[l, 0], rg_b[l, 0])
        wg_r, bg_r = _gate_weights(rg_w[l, 1], rg_b[l, 1])
        hs_r = _rnn_scan(xr, conv_w[l], conv_b[l], wg_r, bg_r, rg_lambda[l, 1], cl, True, None)
        hs = _rnn_scan(xr, conv_w[l], conv_b[l], wg_f, bg_f, rg_lambda[l, 0], cl, False, hs_r)
        hs_lat = jnp.transpose(hs[cl:], (1, 0, 2)).reshape(nb * seq, c_rnn)

        bias = _na_bias_table(rpb[l], rows)
        na_lat = _neighbourhood_attention(z_lat3, z_ctx3, bias, seq)

        wr, wn, wo = w_rnn_o[l].astype(BF16), w_na_o[l].astype(BF16), w_out[l].astype(BF16)
        x2 = _merge(x2, lat[2], hs_lat, z_lat, na_lat.reshape(nb * seq, d_att), wr, wn, wo, seq)
        if ctx_out:
            hs_ctx = jnp.transpose(hs[:cl], (1, 0, 2)).reshape(nb * cl, c_rnn)
            na_ctx = _context_attention(z_ctx3, d_att)
            xc2 = _merge(xc2, cmod[2], hs_ctx, z_ctx, na_ctx.reshape(nb * cl, d_att), wr, wn, wo, nb * cl)

        j = l // 2
        if l % 2 == 0:
            w1, w3, w2 = ffn_w1[j].astype(BF16), ffn_w3[j].astype(BF16), ffn_w2[j].astype(BF16)
            x2 = _dense_ffn(x2, lat[3], lat[4], lat[5], w1, w3, w2, seq)
            if ctx_out:
                xc2 = _dense_ffn(xc2, cmod[3], cmod[4], cmod[5], w1, w3, w2, nb * cl)
        else:
            w1, w3, w2 = moe_w1[j], moe_w3[j], moe_w2[j]
            x2 = _moe_ffn(x2, lat[3], lat[4], lat[5], router[j], w1, w3, w2, seq)
            if ctx_out:
                xc2 = _moe_ffn(xc2, cmod[3], cmod[4], cmod[5], router[j], w1, w3, w2, nb * cl)
    return x2.reshape(nb, seq, d)
```

```python
import functools

import numpy as np
import jax
import jax.numpy as jnp
from jax import lax
from jax.experimental import pallas as pl
from jax.experimental.pallas import tpu as pltpu

F32 = jnp.float32
BF16 = jnp.bfloat16

EPS = 1e-6
N_MOD = 6
GRID_W = 64
WIN_H = 8
WIN_W = 16
HEAD_DIM = 64
RNN_BW = 64
CONV_TAPS = 4
CONV_LEFT = 2
LRU_C = 8.0
TOP_K = 2
MASK_VALUE = -1e30
SQRT_FLOOR = 1e-30

LANES = 128
MXU_DIM = 256
VMEM_LIMIT_BYTES = 56 * 1024 * 1024

NA_GROUP_ROWS = 4
NA_KEY_ROWS = 12


def _params(*sem):
    return pltpu.CompilerParams(dimension_semantics=sem, vmem_limit_bytes=VMEM_LIMIT_BYTES)


def _sigmoid(x):
    return 0.5 * (jnp.tanh(0.5 * x) + 1.0)


def _silu(x):
    return x * _sigmoid(x)


def _gelu_tanh(x):
    return 0.5 * x * (1.0 + jnp.tanh(np.sqrt(2.0 / np.pi) * (x + 0.044715 * (x * x * x))))


def _modulated_norm(x, shift, scale):
    ms = jnp.mean(x * x, axis=-1, keepdims=True)
    return x * lax.rsqrt(ms + EPS) * (1.0 + scale) + shift


def _pick_tile(n, cap, mult):
    best = None
    for t in range(mult, min(n, cap) + 1, mult):
        if n % t == 0:
            best = t
    assert best is not None, (n, cap, mult)
    return best


def _adaln_kernel(c_ref, w_ref, b_ref, o_ref):
    s = _silu(c_ref[...]).astype(BF16)
    o_ref[...] = jnp.dot(s, w_ref[...].astype(BF16), preferred_element_type=F32) + b_ref[...]


def _adaln(cond, w_mod, b_mod):
    depth, d, n = w_mod.shape
    r = cond.shape[0]
    tn = _pick_tile(n, 1536, LANES)
    return pl.pallas_call(
        _adaln_kernel,
        grid=(depth, n // tn),
        in_specs=[
            pl.BlockSpec((r, d), lambda l, j: (0, 0)),
            pl.BlockSpec((None, d, tn), lambda l, j: (l, 0, j)),
            pl.BlockSpec((None, 1, tn), lambda l, j: (l, 0, j)),
        ],
        out_specs=pl.BlockSpec((None, r, tn), lambda l, j: (l, 0, j)),
        out_shape=jax.ShapeDtypeStruct((depth, r, n), F32),
        compiler_params=_params("parallel", "parallel"),
        name="adaln",
    )(cond, w_mod, b_mod.reshape(depth, 1, n))


def _head_norm(x, gain, in_a):
    sq = x * x
    sa = jnp.sum(jnp.where(in_a, sq, 0.0), axis=-1, keepdims=True)
    sb = jnp.sum(jnp.where(in_a, 0.0, sq), axis=-1, keepdims=True)
    ms = jnp.where(in_a, sa, sb) * (1.0 / HEAD_DIM)
    return x * lax.rsqrt(ms + EPS) * gain


def _norm_matmul_kernel(x_ref, sh_ref, sc_ref, w_ref, hg_ref, o_ref, *, seg, normed):
    h = _modulated_norm(x_ref[...], sh_ref[...], sc_ref[...]).astype(BF16)
    in_a = lax.broadcasted_iota(jnp.int32, (1, LANES), 1) < HEAD_DIM
    for j in range(w_ref.shape[1] // seg):
        acc = jnp.dot(h, w_ref[:, j * seg:(j + 1) * seg], preferred_element_type=F32)
        if j in normed:
            gain = hg_ref[normed.index(j)]
            for sl in range(seg // LANES):
                lanes = slice(sl * LANES, (sl + 1) * LANES)
                o_ref[:, j * seg + sl * LANES:j * seg + (sl + 1) * LANES] = _head_norm(
                    acc[:, lanes], gain[:, lanes], in_a).astype(o_ref.dtype)
        else:
            o_ref[:, j * seg:(j + 1) * seg] = acc.astype(o_ref.dtype)


def _norm_matmul(x2d, shift, scale, w, rows_per_mod, head_gains, seg, normed):
    m, d = x2d.shape
    n = w.shape[1]
    tm = _pick_tile(rows_per_mod, 512, 8)
    tiles_per_mod = rows_per_mod // tm
    mod_spec = pl.BlockSpec((None, 1, d), lambda i: (i // tiles_per_mod, 0, 0))
    return pl.pallas_call(
        functools.partial(_norm_matmul_kernel, seg=seg, normed=normed),
        grid=(m // tm,),
        in_specs=[
            pl.BlockSpec((tm, d), lambda i: (i, 0)),
            mod_spec,
            mod_spec,
            pl.BlockSpec((d, n), lambda i: (0, 0), pipeline_mode=pl.Buffered(1)),
            pl.BlockSpec(head_gains.shape, lambda i: (0, 0, 0)),
        ],
        out_specs=pl.BlockSpec((tm, n), lambda i: (i, 0)),
        out_shape=jax.ShapeDtypeStruct((m, n), BF16),
        compiler_params=_params("parallel"),
        name="norm_in_proj",
    )(x2d, shift, scale, w, head_gains)


def _scan_tile_index(i, n_ctx_tiles, n_tiles, reverse):
    if not reverse:
        return i
    return jnp.where(i < n_ctx_tiles, n_ctx_tiles - 1 - i, n_tiles - 1 - (i - n_ctx_tiles))


def _rnn_kernel(xp_ref, x_ref, xn_ref, cw_ref, cb_ref, wg_ref, bg_ref, lam_ref, *rest,
                n_ctx_tiles, n_tiles, reverse, add_prev):
    if add_prev:
        prev_ref, out_ref, a_scr, u_scr, h_scr = rest
    else:
        out_ref, a_scr, u_scr, h_scr = rest
    tl, nb, c = x_ref.shape
    i = pl.program_id(0)
    ti = _scan_tile_index(i, n_ctx_tiles, n_tiles, reverse)

    @pl.when(i == 0)
    def _():
        h_scr[...] = jnp.zeros_like(h_scr)

    at_start = jnp.logical_or(ti == 0, ti == n_ctx_tiles)
    at_end = jnp.logical_or(ti == n_ctx_tiles - 1, ti == n_tiles - 1)
    xp = jnp.where(at_start, 0.0, xp_ref[...].astype(F32))
    xn = jnp.where(at_end, 0.0, xn_ref[...].astype(F32))
    xe = jnp.concatenate([xp, x_ref[...].astype(F32), xn], axis=0)
    xc = cb_ref[...] + xe[0:tl] * cw_ref[0:1, :]
    for k in range(1, CONV_TAPS):
        xc = xc + xe[k:k + tl] * cw_ref[k:k + 1, :]
    xc = xc.reshape(tl * nb, c)

    lam = lam_ref[...]
    log_sig = jnp.minimum(lam, 0.0) - jnp.log(1.0 + jnp.exp(-jnp.abs(lam)))
    k2 = (0.5 * LRU_C * np.log2(np.e)) * log_sig
    for j in range(c // MXU_DIM):
        sl = slice(MXU_DIM * j, MXU_DIM * (j + 1))
        xj = xc[:, sl]
        t = jnp.tanh(jnp.dot(xj.astype(BF16), wg_ref[j], preferred_element_type=F32) + bg_ref[j])
        a = jnp.exp2(k2[:, sl] * t[:, :MXU_DIM] + k2[:, sl])
        v = 1.0 - a * a
        gated_x = (t[:, MXU_DIM:] + 1.0) * (0.5 * xj)
        a_scr[:, sl] = a
        u_scr[:, sl] = (v * lax.rsqrt(jnp.maximum(v, SQRT_FLOOR))) * gated_x

    def step(s, h):
        tt = (tl - 1 - s) if reverse else s
        row = pl.multiple_of(tt * nb, nb)
        h = a_scr[pl.ds(row, nb), :] * h + u_scr[pl.ds(row, nb), :]
        val = h + prev_ref[tt].astype(F32) if add_prev else h
        out_ref[tt] = val.astype(out_ref.dtype)
        return h

    h_scr[...] = lax.fori_loop(0, tl, step, h_scr[...], unroll=4)


def _rnn_scan(xr, conv_w, conv_b, wg, bg, lam, n_ctx, reverse, prev):
    t, nb, c = xr.shape
    tl = _pick_tile(n_ctx, 64, 2)
    assert t % tl == 0 and n_ctx % tl == 0
    n_tiles, n_ctx_tiles = t // tl, n_ctx // tl
    tile = functools.partial(_scan_tile_index, n_ctx_tiles=n_ctx_tiles, n_tiles=n_tiles, reverse=reverse)
    half = tl // CONV_LEFT
    full = lambda shape: pl.BlockSpec(shape, lambda i: (0,) * len(shape))
    in_specs = [
        pl.BlockSpec((CONV_LEFT, nb, c), lambda i: (jnp.maximum(tile(i) * half - 1, 0), 0, 0)),
        pl.BlockSpec((tl, nb, c), lambda i: (tile(i), 0, 0)),
        pl.BlockSpec((1, nb, c), lambda i: (jnp.minimum((tile(i) + 1) * tl, t - 1), 0, 0)),
        full((CONV_TAPS, c)),
        full((1, c)),
        full(wg.shape),
        full(bg.shape),
        full((1, c)),
    ]
    args = [xr, xr, xr, conv_w, conv_b.reshape(1, c), wg, bg, lam.reshape(1, c)]
    if prev is not None:
        in_specs.append(pl.BlockSpec((tl, nb, c), lambda i: (tile(i), 0, 0)))
        args.append(prev)
    kern = functools.partial(_rnn_kernel, n_ctx_tiles=n_ctx_tiles, n_tiles=n_tiles, reverse=reverse,
                             add_prev=prev is not None)
    return pl.pallas_call(
        kern,
        grid=(n_tiles,),
        in_specs=in_specs,
        out_specs=pl.BlockSpec((tl, nb, c), lambda i: (tile(i), 0, 0)),
        out_shape=jax.ShapeDtypeStruct((t, nb, c), BF16),
        scratch_shapes=[pltpu.VMEM((tl * nb, c), F32), pltpu.VMEM((tl * nb, c), F32), pltpu.VMEM((nb, c), F32)],
        compiler_params=_params("arbitrary"),
        name="rglru_scan_rev" if reverse else "rglru_scan_fwd",
    )(*args)


def _gate_weights(rg_w_d, rg_b_d):
    _, nblk, bw, _ = rg_w_d.shape
    per = MXU_DIM // bw
    ngrp = nblk // per
    w = rg_w_d.reshape(2, ngrp, per, bw, bw)
    eye = jnp.eye(per, dtype=rg_w_d.dtype)
    dense = w[:, :, :, :, None, :] * eye[None, None, :, None, :, None]
    dense = dense.reshape(2, ngrp, MXU_DIM, MXU_DIM)
    wg = jnp.concatenate([dense[0], dense[1]], axis=-1).astype(BF16)
    b = rg_b_d.reshape(2, ngrp, 1, MXU_DIM)
    bg = jnp.concatenate([b[0], b[1]], axis=-1)
    return wg * 0.5, bg * 0.5


Q_SCALE = (HEAD_DIM ** -0.5) * float(np.log2(np.e))


def _dot_nt(a, b):
    return lax.dot_general(a, b, (((1,), (1,)), ((), ())), preferred_element_type=F32)


def _stack_heads(q, in_a):
    zero = jnp.zeros_like(q)
    return jnp.concatenate([jnp.where(in_a, q, zero), jnp.where(in_a, zero, q)], axis=0)


def _na_kernel(q_ref, k_ref, v_ref, kc_ref, vc_ref, bias_ref, o_ref, sa_scr, sb_scr, *, rows):
    cl = kc_ref.shape[0]
    gq = NA_GROUP_ROWS * GRID_W
    gk = NA_KEY_ROWS * GRID_W
    ngrp = rows // NA_GROUP_ROWS
    in_a = lax.broadcasted_iota(jnp.int32, (1, LANES), 1) < HEAD_DIM

    def key_base(g):
        kb = jnp.clip(NA_GROUP_ROWS * g - WIN_H // 2, 0, rows - NA_KEY_ROWS)
        return pl.multiple_of(kb * GRID_W, GRID_W)

    def scores(g, s_scr):
        cls = jnp.where(g == 0, 0, jnp.where(g == ngrp - 1, 2, 1))
        q2 = _stack_heads(q_ref[pl.ds(pl.multiple_of(g * gq, gq), gq), :], in_a)
        s_scr[:, 0:gk] = _dot_nt(q2, k_ref[pl.ds(key_base(g), gk), :]) + bias_ref[cls]
        s_scr[:, gk:gk + cl] = _dot_nt(q2, kc_ref[...])

    def attend(g, s_scr):
        sc = s_scr[...]
        p = jnp.exp2(sc - jnp.max(sc, axis=-1, keepdims=True))
        den = jnp.sum(p, axis=-1, keepdims=True)
        pb = p.astype(BF16)
        o2 = (jnp.dot(pb[:, 0:gk], v_ref[pl.ds(key_base(g), gk), :], preferred_element_type=F32)
              + jnp.dot(pb[:, gk:gk + cl], vc_ref[...], preferred_element_type=F32)) / den
        o_ref[pl.ds(pl.multiple_of(g * gq, gq), gq), :] = jnp.where(in_a, o2[:gq], o2[gq:]).astype(o_ref.dtype)

    scores(0, sa_scr)

    def pair(it, carry):
        g = 2 * it
        scores(g + 1, sb_scr)
        attend(g, sa_scr)
        scores(g + 2, sa_scr)
        attend(g + 1, sb_scr)
        return carry

    lax.fori_loop(0, ngrp // 2 - 1, pair, 0)
    scores(ngrp - 1, sb_scr)
    attend(ngrp - 2, sa_scr)
    attend(ngrp - 1, sb_scr)


def _bias_table_kernel(idx_ref, t_ref, o_ref):
    c = pl.program_id(1)
    left = lax.broadcasted_iota(jnp.int32, (1, LANES), 1) < GRID_W
    per_class = NA_GROUP_ROWS * NA_KEY_ROWS
    for g in range(2):
        for a in range(NA_GROUP_ROWS):
            r0 = (g * NA_GROUP_ROWS + a) * GRID_W
            for j in range(NA_KEY_ROWS // 2):
                o_even = idx_ref[c * per_class + a * NA_KEY_ROWS + 2 * j]
                o_odd = idx_ref[c * per_class + a * NA_KEY_ROWS + 2 * j + 1]
                o_ref[r0:r0 + GRID_W, j * LANES:(j + 1) * LANES] = jnp.where(left, t_ref[g, o_even], t_ref[g, o_odd])


def _na_bias_table(rpb, rows):
    nh, n_ro, n_co = rpb.shape
    ngrp = rows // NA_GROUP_ROWS
    qc = np.arange(GRID_W)[:, None]
    kc = np.arange(GRID_W)[None, :]
    wstart = np.clip(qc - WIN_W // 2, 0, GRID_W - WIN_W)
    col_valid = (kc >= wstart) & (kc < wstart + WIN_W)
    col_onehot = ((kc - qc + WIN_W - 1)[None] == np.arange(n_co)[:, None, None])
    rpb2 = rpb.astype(F32).reshape(nh // 2, 2, n_ro, n_co)
    by_col = jnp.einsum("pgrj,jqk->pgrqk", rpb2, col_onehot.astype(np.float32),
                        precision=lax.Precision.HIGHEST) * np.float32(np.log2(np.e))
    by_col = jnp.where(col_valid, by_col, MASK_VALUE)
    by_col = jnp.concatenate([by_col, jnp.full((nh // 2, 2, 1, GRID_W, GRID_W), MASK_VALUE, F32)], axis=2)
    tables = jnp.concatenate([by_col, by_col], axis=-1)
    idx = np.full((3, NA_GROUP_ROWS, NA_KEY_ROWS), n_ro, np.int32)
    for ci, g in enumerate((0, 1, ngrp - 1)):
        kb = int(np.clip(NA_GROUP_ROWS * g - WIN_H // 2, 0, rows - NA_KEY_ROWS))
        for a in range(NA_GROUP_ROWS):
            r = NA_GROUP_ROWS * g + a
            start = int(np.clip(r - WIN_H // 2, 0, rows - WIN_H))
            for cr in range(NA_KEY_ROWS):
                if start <= kb + cr < start + WIN_H:
                    idx[ci, a, cr] = kb + cr - r + WIN_H - 1
    gq, gk = NA_GROUP_ROWS * GRID_W, NA_KEY_ROWS * GRID_W
    return pl.pallas_call(
        _bias_table_kernel,
        grid_spec=pltpu.PrefetchScalarGridSpec(
            num_scalar_prefetch=1,
            grid=(nh // 2, 3),
            in_specs=[pl.BlockSpec((None,) + tables.shape[1:], lambda p, c, ix: (p, 0, 0, 0, 0))],
            out_specs=pl.BlockSpec((None, None, 2 * gq, gk), lambda p, c, ix: (p, c, 0, 0)),
        ),
        out_shape=jax.ShapeDtypeStruct((nh // 2, 3, 2 * gq, gk), F32),
        compiler_params=_params("parallel", "parallel"),
        name="na_bias_table",
    )(jnp.asarray(idx.reshape(-1)), tables)


def _neighbourhood_attention(z_lat, z_ctx, bias, seq):
    nb, s, _ = z_lat.shape
    cl = z_ctx.shape[1]
    npair = bias.shape[0]
    d_att = npair * LANES
    cb = npair
    rows = s // GRID_W
    assert s % GRID_W == 0 and rows % (2 * NA_GROUP_ROWS) == 0 and rows >= NA_KEY_ROWS + NA_GROUP_ROWS
    s_shape = (2 * NA_GROUP_ROWS * GRID_W, NA_KEY_ROWS * GRID_W + cl)
    lat = lambda seg: pl.BlockSpec((None, s, LANES), lambda p, b: (b, 0, seg * cb + p))
    ctx = lambda seg: pl.BlockSpec((None, cl, LANES), lambda p, b: (b, 0, seg * cb + p))
    return pl.pallas_call(
        functools.partial(_na_kernel, rows=rows),
        grid=(npair, nb),
        in_specs=[lat(4), lat(1), lat(2), ctx(1), ctx(2),
                  pl.BlockSpec((None,) + bias.shape[1:], lambda p, b: (p, 0, 0, 0))],
        out_specs=pl.BlockSpec((None, s, LANES), lambda p, b: (b, 0, p)),
        out_shape=jax.ShapeDtypeStruct((nb, s, d_att), BF16),
        scratch_shapes=[pltpu.VMEM(s_shape, F32), pltpu.VMEM(s_shape, F32)],
        compiler_params=_params("parallel", "parallel"),
        name="neighbourhood_attention",
    )(z_lat, z_lat, z_lat, z_ctx, z_ctx, bias)


def _ctx_attn_kernel(q_ref, k_ref, v_ref, o_ref):
    in_a = lax.broadcasted_iota(jnp.int32, (1, LANES), 1) < HEAD_DIM
    n = q_ref.shape[0]
    sc = _dot_nt(_stack_heads(q_ref[...], in_a), k_ref[...])
    p = jnp.exp2(sc - jnp.max(sc, axis=-1, keepdims=True))
    den = jnp.sum(p, axis=-1, keepdims=True)
    o2 = jnp.dot(p.astype(BF16), v_ref[...], preferred_element_type=F32) / den
    o_ref[...] = jnp.where(in_a, o2[:n], o2[n:]).astype(o_ref.dtype)


def _context_attention(z_ctx, d_att):
    nb, cl, _ = z_ctx.shape
    cb = d_att // LANES
    ctx = lambda seg: pl.BlockSpec((None, cl, LANES), lambda p, b: (b, 0, seg * cb + p))
    return pl.pallas_call(
        _ctx_attn_kernel,
        grid=(cb, nb),
        in_specs=[ctx(4), ctx(1), ctx(2)],
        out_specs=pl.BlockSpec((None, cl, LANES), lambda p, b: (b, 0, p)),
        out_shape=jax.ShapeDtypeStruct((nb, cl, d_att), BF16),
        compiler_params=_params("parallel", "parallel"),
        name="context_attention",
    )(z_ctx, z_ctx, z_ctx)


def _merge_kernel(x_ref, g_ref, hs_ref, y_ref, na_ref, gr_ref, gn_ref, wr_ref, wn_ref, wo_ref, o_ref):
    y_rnn = (hs_ref[...].astype(F32) * _gelu_tanh(y_ref[...].astype(F32))).astype(BF16)
    t_rnn = jnp.dot(y_rnn, wr_ref[...], preferred_element_type=F32)
    t_na = jnp.dot(na_ref[...], wn_ref[...], preferred_element_type=F32)
    mix = _sigmoid(gr_ref[...].astype(F32)) * t_rnn + _sigmoid(gn_ref[...].astype(F32)) * t_na
    out = jnp.dot(mix.astype(BF16), wo_ref[...], preferred_element_type=F32)
    o_ref[...] = x_ref[...] + g_ref[...] * out


def _merge(x2d, gate, hs, z, na, w_rnn_o, w_na_o, w_out, rows_per_mod):
    m, d = x2d.shape
    c = hs.shape[1]
    da = na.shape[1]
    tm = _pick_tile(rows_per_mod, 512, 8)
    tiles_per_mod = rows_per_mod // tm
    y_blk = (c + 2 * da) // c
    gr_blk = (2 * c + 3 * da) // d
    gn_blk = gr_blk + 1
    row = lambda width, blk=0: pl.BlockSpec((tm, width), lambda i: (i, blk))
    whole = lambda w: pl.BlockSpec(w.shape, lambda i: (0, 0))
    return pl.pallas_call(
        _merge_kernel,
        grid=(m // tm,),
        in_specs=[row(d), pl.BlockSpec((None, 1, d), lambda i: (i // tiles_per_mod, 0, 0)),
                  row(c), row(c, y_blk), row(da), row(d, gr_blk), row(d, gn_blk),
                  whole(w_rnn_o), whole(w_na_o), whole(w_out)],
        out_specs=row(d),
        out_shape=jax.ShapeDtypeStruct((m, d), F32),
        compiler_params=_params("parallel"),
        name="merge_out_proj",
    )(x2d, gate, hs, z, na, z, z, w_rnn_o, w_na_o, w_out)


def _ffn_kernel(x_ref, sh_ref, sc_ref, g_ref, w1_ref, w3_ref, w2_ref, o_ref, *, tf):
    x = x_ref[...]
    h = _modulated_norm(x, sh_ref[...], sc_ref[...]).astype(BF16)
    acc = None
    for j in range(w1_ref.shape[1] // tf):
        cols = slice(j * tf, (j + 1) * tf)
        a = jnp.dot(h, w1_ref[:, cols], preferred_element_type=F32)
        b = jnp.dot(h, w3_ref[:, cols], preferred_element_type=F32)
        part = jnp.dot((_silu(a) * b).astype(BF16), w2_ref[cols, :], preferred_element_type=F32)
        acc = part if acc is None else acc + part
    o_ref[...] = x + g_ref[...] * acc


def _dense_ffn(x2d, shift, scale, gate, w1, w3, w2, rows_per_mod):
    m, d = x2d.shape
    dff = w1.shape[1]
    tm = _pick_tile(rows_per_mod, 512, 8)
    tf = _pick_tile(dff, 1536, LANES)
    tiles_per_mod = rows_per_mod // tm
    mod = pl.BlockSpec((None, 1, d), lambda i: (i // tiles_per_mod, 0, 0))
    whole = lambda w: pl.BlockSpec(w.shape, lambda i: (0, 0), pipeline_mode=pl.Buffered(1))
    return pl.pallas_call(
        functools.partial(_ffn_kernel, tf=tf),
        grid=(m // tm,),
        in_specs=[pl.BlockSpec((tm, d), lambda i: (i, 0)), mod, mod, mod, whole(w1), whole(w3), whole(w2)],
        out_specs=pl.BlockSpec((tm, d), lambda i: (i, 0)),
        out_shape=jax.ShapeDtypeStruct((m, d), F32),
        compiler_params=_params("parallel"),
        name="dense_swiglu",
    )(x2d, shift, scale, gate, w1, w3, w2)


SUBLANES = 8
GATHER_DMA_PRIORITY = 1


def _to_token_tiles(ref, x):
    n = x.shape[0]
    for s in range(SUBLANES):
        ref[pl.ds(s, n, stride=SUBLANES), :] = x[:, s * LANES:(s + 1) * LANES]


def _from_token_tiles(ref, n, s):
    return ref[pl.ds(s, n, stride=SUBLANES), :]


def _route_kernel(x_ref, sh_ref, sc_ref, wr_ref, wrl_ref, h_ref, r_ref, *, n_experts):
    h = _modulated_norm(x_ref[...], sh_ref[...], sc_ref[...])
    _to_token_tiles(h_ref, h)
    h_hi = h.astype(BF16)
    h_lo = (h - h_hi.astype(F32)).astype(BF16)
    logits = (jnp.dot(h_hi, wr_ref[...], preferred_element_type=F32)
              + jnp.dot(h_lo, wr_ref[...], preferred_element_type=F32)
              + jnp.dot(h_hi, wrl_ref[...], preferred_element_type=F32))
    lane = lax.broadcasted_iota(jnp.int32, logits.shape, 1).astype(F32)
    neg = -jnp.inf
    lg = jnp.where(lane < n_experts, logits, neg)
    m1 = jnp.max(lg, axis=-1, keepdims=True)
    i1 = jnp.min(jnp.where(lg == m1, lane, float(LANES)), axis=-1, keepdims=True)
    lg2 = jnp.where(lane == i1, neg, lg)
    m2 = jnp.max(lg2, axis=-1, keepdims=True)
    i2 = jnp.min(jnp.where(lg2 == m2, lane, float(LANES)), axis=-1, keepdims=True)
    e = jnp.exp(m2 - m1)
    w1 = 1.0 / (1.0 + e)
    w2 = e / (1.0 + e)
    r_ref[...] = jnp.where(lane == 0, i1, jnp.where(lane == 1, i2, jnp.where(lane == 2, w1,
                           jnp.where(lane == 3, w2, 0.0))))


def _route(x2d, shift, scale, router, rows_per_mod):
    m, d = x2d.shape
    assert d == SUBLANES * LANES
    n_experts = router.shape[1]
    wr32 = jnp.zeros((d, LANES), F32).at[:, :n_experts].set(router)
    wr = wr32.astype(BF16)
    wrl = (wr32 - wr.astype(F32)).astype(BF16)
    tm = _pick_tile(rows_per_mod, 512, 8)
    tiles_per_mod = rows_per_mod // tm
    mod = pl.BlockSpec((None, 1, d), lambda i: (i // tiles_per_mod, 0, 0))
    rspec = pl.BlockSpec((d, LANES), lambda i: (0, 0))
    return pl.pallas_call(
        functools.partial(_route_kernel, n_experts=n_experts),
        grid=(m // tm,),
        in_specs=[pl.BlockSpec((tm, d), lambda i: (i, 0)), mod, mod, rspec, rspec],
        out_specs=[pl.BlockSpec((tm * SUBLANES, LANES), lambda i: (i, 0)),
                   pl.BlockSpec((tm, LANES), lambda i: (i, 0))],
        out_shape=[jax.ShapeDtypeStruct((m * SUBLANES, LANES), F32), jax.ShapeDtypeStruct((m, LANES), F32)],
        compiler_params=_params("parallel"),
        name="moe_route",
    )(x2d, shift, scale, wr, wrl)


EXPERT_TILE_ROWS = 1024


def _expert_ffn_kernel(te_ref, nv_ref, src_ref, h_hbm, w1_ref, w3_ref, w2_ref, o_ref, xbuf, sem, h_scr, acc_scr):
    t = pl.program_id(0)
    f = pl.program_id(1)
    nv = nv_ref[0]
    tm = h_scr.shape[0]

    def issue_rows(tile, slot):
        base = tile * tm

        def body(i, carry):
            src_row = pl.multiple_of(src_ref[base + i] * SUBLANES, SUBLANES)
            dst_row = pl.multiple_of(i * SUBLANES, SUBLANES)
            pltpu.make_async_copy(h_hbm.at[pl.ds(src_row, SUBLANES), :],
                                  xbuf.at[slot, pl.ds(dst_row, SUBLANES), :],
                                  sem.at[slot]).start(priority=GATHER_DMA_PRIORITY)
            return carry

        lax.fori_loop(0, tm, body, 0, unroll=8)

    def wait_rows(slot):
        pltpu.make_async_copy(h_hbm.at[pl.ds(0, tm * SUBLANES), :], xbuf.at[slot], sem.at[slot]).wait()

    @pl.when(t < nv)
    def _():
        @pl.when(f == 0)
        def _():
            slot = t % 2

            @pl.when(t == 0)
            def _():
                issue_rows(0, 0)

            wait_rows(slot)
            for s in range(SUBLANES):
                h_scr[:, s * LANES:(s + 1) * LANES] = _from_token_tiles(xbuf.at[slot], tm, s).astype(BF16)
            acc_scr[...] = jnp.zeros_like(acc_scr)

            @pl.when(t + 1 < nv)
            def _():
                issue_rows(t + 1, 1 - slot)

        h = h_scr[...]
        a = jnp.dot(h, w1_ref[...].astype(BF16), preferred_element_type=F32)
        b = jnp.dot(h, w3_ref[...].astype(BF16), preferred_element_type=F32)
        acc_scr[...] += jnp.dot((_silu(a) * b).astype(BF16), w2_ref[...].astype(BF16),
                                preferred_element_type=F32)

        @pl.when(f == pl.num_programs(1) - 1)
        def _():
            _to_token_tiles(o_ref, acc_scr[...])

    @pl.when(jnp.logical_and(t >= nv, f == pl.num_programs(1) - 1))
    def _():
        o_ref[...] = jnp.zeros_like(o_ref)


def _expert_ffn(h, src, tile_expert, n_valid, w1, w3, w2):
    tm = EXPERT_TILE_ROWS
    n = src.shape[0]
    d = w1.shape[1]
    dfe = w1.shape[2]
    tf = _pick_tile(dfe, 512, LANES)
    nf = dfe // tf

    def fsel(t, f, nv):
        return jnp.where(t < nv[0], f, nf - 1)

    return pl.pallas_call(
        _expert_ffn_kernel,
        grid_spec=pltpu.PrefetchScalarGridSpec(
            num_scalar_prefetch=3,
            grid=(n // tm, nf),
            in_specs=[pl.BlockSpec(memory_space=pl.ANY),
                      pl.BlockSpec((None, d, tf), lambda t, f, te, nv, sr: (te[t], 0, fsel(t, f, nv))),
                      pl.BlockSpec((None, d, tf), lambda t, f, te, nv, sr: (te[t], 0, fsel(t, f, nv))),
                      pl.BlockSpec((None, tf, d), lambda t, f, te, nv, sr: (te[t], fsel(t, f, nv), 0))],
            out_specs=pl.BlockSpec((tm * SUBLANES, LANES), lambda t, f, te, nv, sr: (t, 0)),
            scratch_shapes=[pltpu.VMEM((2, tm * SUBLANES, LANES), F32), pltpu.SemaphoreType.DMA((2,)),
                            pltpu.VMEM((tm, d), BF16), pltpu.VMEM((tm, d), F32)],
        ),
        out_shape=jax.ShapeDtypeStruct((n * SUBLANES, LANES), F32),
        compiler_params=_params("arbitrary", "arbitrary"),
        name="expert_swiglu",
    )(tile_expert, n_valid, src, h, w1, w3, w2)


def _combine_kernel(pos_ref, x_ref, g_ref, r_ref, ys_hbm, o_ref, ybuf, sem):
    i = pl.program_id(0)
    n = pl.num_programs(0)
    tm = x_ref.shape[0]
    m = n * tm

    def issue_rows(tile, slot):
        for k in range(TOP_K):
            base = k * m + tile * tm

            def body(j, carry):
                src_row = pl.multiple_of(pos_ref[base + j] * SUBLANES, SUBLANES)
                dst_row = pl.multiple_of(j * SUBLANES, SUBLANES)
                pltpu.make_async_copy(ys_hbm.at[pl.ds(src_row, SUBLANES), :],
                                      ybuf.at[slot, k, pl.ds(dst_row, SUBLANES), :],
                                      sem.at[slot]).start(priority=GATHER_DMA_PRIORITY)
                return carry

            lax.fori_loop(0, tm, body, 0, unroll=8)

    slot = i % 2

    @pl.when(i == 0)
    def _():
        issue_rows(0, 0)

    pltpu.make_async_copy(ybuf.at[1 - slot], ybuf.at[slot], sem.at[slot]).wait()

    @pl.when(i + 1 < n)
    def _():
        issue_rows(i + 1, 1 - slot)

    r = r_ref[...]
    w1, w2 = r[:, 2:3], r[:, 3:4]
    for s in range(SUBLANES):
        lanes = slice(s * LANES, (s + 1) * LANES)
        mix = (w1 * _from_token_tiles(ybuf.at[slot, 0], tm, s) + w2 * _from_token_tiles(ybuf.at[slot, 1], tm, s))
        o_ref[:, lanes] = x_ref[:, lanes] + g_ref[:, lanes] * mix


def _combine(x2d, gate, route, ys, pos, rows_per_mod):
    m, d = x2d.shape
    tm = _pick_tile(rows_per_mod, 512, 8)
    tiles_per_mod = rows_per_mod // tm
    return pl.pallas_call(
        _combine_kernel,
        grid_spec=pltpu.PrefetchScalarGridSpec(
            num_scalar_prefetch=1,
            grid=(m // tm,),
            in_specs=[pl.BlockSpec((tm, d), lambda i, ps: (i, 0)),
                      pl.BlockSpec((None, 1, d), lambda i, ps: (i // tiles_per_mod, 0, 0)),
                      pl.BlockSpec((tm, LANES), lambda i, ps: (i, 0)),
                      pl.BlockSpec(memory_space=pl.ANY)],
            out_specs=pl.BlockSpec((tm, d), lambda i, ps: (i, 0)),
            scratch_shapes=[pltpu.VMEM((2, TOP_K, tm * SUBLANES, LANES), F32), pltpu.SemaphoreType.DMA((2,))],
        ),
        out_shape=jax.ShapeDtypeStruct((m, d), F32),
        compiler_params=_params("arbitrary"),
        name="moe_combine",
    )(pos, x2d, gate, route, ys)


def _moe_ffn(x2d, shift, scale, gate, router, w1, w3, w2, rows_per_mod):
    m, d = x2d.shape
    n_experts = router.shape[1]
    h, route = _route(x2d, shift, scale, router, rows_per_mod)
    tm = EXPERT_TILE_ROWS
    expert = route[:, :TOP_K].astype(jnp.int32).T.reshape(-1)
    onehot = (expert[:, None] == jnp.arange(n_experts)[None, :]).astype(jnp.int32)
    csum = jnp.cumsum(onehot, axis=0)
    rank = jnp.sum(onehot * (csum - 1), axis=1)
    tiles = (csum[-1] + tm - 1) // tm
    tile_end = jnp.cumsum(tiles)
    pos = (jnp.sum(onehot * ((tile_end - tiles) * tm)[None, :], axis=1) + rank).astype(jnp.int32)
    n_tiles = (TOP_K * m) // tm + n_experts
    src = jnp.zeros((n_tiles * tm,), jnp.int32).at[pos].set(jnp.arange(TOP_K * m, dtype=jnp.int32) % m)
    tile_expert = jnp.minimum(jnp.sum((jnp.arange(n_tiles)[:, None] >= tile_end[None, :]).astype(jnp.int32), axis=1),
                              n_experts - 1).astype(jnp.int32)
    n_valid = tile_end[-1:].astype(jnp.int32)
    ys = _expert_ffn(h, src, tile_expert, n_valid, w1, w3, w2)
    return _combine(x2d, gate, route, ys, pos, rows_per_mod)


def kernel(x, c, ctx, c_ctx, w_mod, b_mod, w_in, conv_w, conv_b, rg_lambda, rg_w, rg_b, q_gain, k_gain, rpb,
           w_rnn_o, w_na_o, w_out, ffn_w1, ffn_w3, ffn_w2, router, moe_w1, moe_w3, moe_w2):
    nb, seq, d = x.shape
    cl = ctx.shape[1]
    depth = w_mod.shape[0]
    c_rnn = conv_w.shape[2]
    d_att = rpb.shape[1] * HEAD_DIM
    ctx_cols = c_rnn + 2 * d_att
    rows = seq // GRID_W

    n_cond = -(-(nb + 1) // 8) * 8
    cond = jnp.zeros((n_cond, d), F32).at[:nb].set(c).at[nb].set(c_ctx)
    mods = _adaln(cond, w_mod, b_mod)

    x2 = x.reshape(nb * seq, d)
    xc2 = ctx.reshape(nb * cl, d)
    for l in range(depth):
        ctx_out = l < depth - 1
        lat = [mods[l, :nb, k * d:(k + 1) * d].reshape(nb, 1, d) for k in range(N_MOD)]
        cmod = [mods[l, nb:nb + 1, k * d:(k + 1) * d].reshape(1, 1, d) for k in range(N_MOD)]
        w_in_l = w_in[l].astype(BF16)
        tile_heads = lambda g: jnp.tile(g.astype(F32), c_rnn // HEAD_DIM).reshape(1, c_rnn)
        head_gains = jnp.stack([tile_heads(k_gain[l]), tile_heads(q_gain[l]) * Q_SCALE])

        z_lat = _norm_matmul(x2, lat[0], lat[1], w_in_l, seq, head_gains, c_rnn, (1, 4))
        if ctx_out:
            z_ctx = _norm_matmul(xc2, cmod[0], cmod[1], w_in_l, nb * cl, head_gains, c_rnn, (1, 4))
        else:
            z_ctx = _norm_matmul(xc2, cmod[0], cmod[1], w_in_l[:, :ctx_cols], nb * cl, head_gains, c_rnn, (1,))
        ncl, ncc = z_lat.shape[1], z_ctx.shape[1]
        z_lat3 = z_lat.reshape(nb, seq, ncl)
        z_ctx3 = z_ctx.reshape(nb, cl, ncc)

        xr = jnp.concatenate([jnp.transpose(z_ctx3[:, :, :c_rnn], (1, 0, 2)),
                              jnp.transpose(z_lat3[:, :, :c_rnn], (1, 0, 2))], axis=0)
        wg_f, bg_f = _gate_weights(rg_w[l, 0], rg_b[l, 0])
        wg_r, bg_r = _gate_weights(rg_w[l, 1], rg_b[l, 1])
        hs_r = _rnn_scan(xr, conv_w[l], conv_b[l], wg_r, bg_r, rg_lambda[l, 1], cl, True, None)
        hs = _rnn_scan(xr, conv_w[l], conv_b[l], wg_f, bg_f, rg_lambda[l, 0], cl, False, hs_r)
        hs_lat = jnp.transpose(hs[cl:], (1, 0, 2)).reshape(nb * seq, c_rnn)

        bias = _na_bias_table(rpb[l], rows)
        na_lat = _neighbourhood_attention(z_lat3, z_ctx3, bias, seq)

        wr, wn, wo = w_rnn_o[l].astype(BF16), w_na_o[l].astype(BF16), w_out[l].astype(BF16)
        x2 = _merge(x2, lat[2], hs_lat, z_lat, na_lat.reshape(nb * seq, d_att), wr, wn, wo, seq)
        if ctx_out:
            hs_ctx = jnp.transpose(hs[:cl], (1, 0, 2)).reshape(nb * cl, c_rnn)
            na_ctx = _context_attention(z_ctx3, d_att)
            xc2 = _merge(xc2, cmod[2], hs_ctx, z_ctx, na_ctx.reshape(nb * cl, d_att), wr, wn, wo, nb * cl)

        j = l // 2
        if l % 2 == 0:
            w1, w3, w2 = ffn_w1[j].astype(BF16), ffn_w3[j].astype(BF16), ffn_w2[j].astype(BF16)
            x2 = _dense_ffn(x2, lat[3], lat[4], lat[5], w1, w3, w2, seq)
            if ctx_out:
                xc2 = _dense_ffn(xc2, cmod[3], cmod[4], cmod[5], w1, w3, w2, nb * cl)
        else:
            w1, w3, w2 = moe_w1[j], moe_w3[j], moe_w2[j]
            x2 = _moe_ffn(x2, lat[3], lat[4], lat[5], router[j], w1, w3, w2, seq)
            if ctx_out:
                xc2 = _moe_ffn(xc2, cmod[3], cmod[4], cmod[5], router[j], w1, w3, w2, nb * cl)
    return x2.reshape(nb, seq, d)
```

```python
import functools

import numpy as np
import jax
import jax.numpy as jnp
from jax import lax
from jax.experimental import pallas as pl
from jax.experimental.pallas import tpu as pltpu

F32 = jnp.float32
BF16 = jnp.bfloat16

EPS = 1e-6
N_MOD = 6
GRID_W = 64
WIN_H = 8
WIN_W = 16
HEAD_DIM = 64
RNN_BW = 64
CONV_TAPS = 4
CONV_LEFT = 2
LRU_C = 8.0
TOP_K = 2
MASK_VALUE = -1e30
SQRT_FLOOR = 1e-30

LANES = 128
MXU_DIM = 256
VMEM_LIMIT_BYTES = 56 * 1024 * 1024

NA_GROUP_ROWS = 4
NA_KEY_ROWS = 12


def _params(*sem):
    return pltpu.CompilerParams(dimension_semantics=sem, vmem_limit_bytes=VMEM_LIMIT_BYTES)


def _sigmoid(x):
    return 0.5 * (jnp.tanh(0.5 * x) + 1.0)


def _silu(x):
    return x * _sigmoid(x)


def _gelu_tanh(x):
    return 0.5 * x * (1.0 + jnp.tanh(np.sqrt(2.0 / np.pi) * (x + 0.044715 * (x * x * x))))


def _modulated_norm(x, shift, scale):
    ms = jnp.mean(x * x, axis=-1, keepdims=True)
    return x * lax.rsqrt(ms + EPS) * (1.0 + scale) + shift


def _pick_tile(n, cap, mult):
    best = None
    for t in range(mult, min(n, cap) + 1, mult):
        if n % t == 0:
            best = t
    assert best is not None, (n, cap, mult)
    return best


def _adaln_kernel(c_ref, w_ref, b_ref, o_ref):
    s = _silu(c_ref[...]).astype(BF16)
    o_ref[...] = jnp.dot(s, w_ref[...].astype(BF16), preferred_element_type=F32) + b_ref[...]


def _adaln(cond, w_mod, b_mod):
    depth, d, n = w_mod.shape
    r = cond.shape[0]
    tn = _pick_tile(n, 1536, LANES)
    return pl.pallas_call(
        _adaln_kernel,
        grid=(depth, n // tn),
        in_specs=[
            pl.BlockSpec((r, d), lambda l, j: (0, 0)),
            pl.BlockSpec((None, d, tn), lambda l, j: (l, 0, j)),
            pl.BlockSpec((None, 1, tn), lambda l, j: (l, 0, j)),
        ],
        out_specs=pl.BlockSpec((None, r, tn), lambda l, j: (l, 0, j)),
        out_shape=jax.ShapeDtypeStruct((depth, r, n), F32),
        compiler_params=_params("parallel", "parallel"),
        name="adaln",
    )(cond, w_mod, b_mod.reshape(depth, 1, n))


def _head_norm(x, gain, in_a):
    sq = x * x
    sa = jnp.sum(jnp.where(in_a, sq, 0.0), axis=-1, keepdims=True)
    sb = jnp.sum(jnp.where(in_a, 0.0, sq), axis=-1, keepdims=True)
    ms = jnp.where(in_a, sa, sb) * (1.0 / HEAD_DIM)
    return x * lax.rsqrt(ms + EPS) * gain


IN_PROJ_STEPS = 32
XR_PITCH = 40


def _norm_matmul_kernel(x_ref, sh_ref, sc_ref, w_ref, hg_ref, o_ref, xr_ref, t_scr, *, seg, normed):
    nb, ts, d = x_ref.shape
    h = _modulated_norm(x_ref[...], sh_ref[...], sc_ref[...]).reshape(nb * ts, d).astype(BF16)
    in_a = lax.broadcasted_iota(jnp.int32, (1, LANES), 1) < HEAD_DIM
    for j in range(w_ref.shape[1] // seg):
        acc = jnp.dot(h, w_ref[:, j * seg:(j + 1) * seg], preferred_element_type=F32)
        if j == 0:
            for sl in range(seg // LANES):
                for b in range(nb):
                    t_scr[sl, b * XR_PITCH:b * XR_PITCH + ts, :] = acc[b * ts:(b + 1) * ts, sl * LANES:(sl + 1) * LANES]
            for t in range(ts):
                xr_ref[t] = jnp.concatenate([t_scr.at[sl][pl.ds(t, nb, stride=XR_PITCH), :]
                                             for sl in range(seg // LANES)], axis=-1).astype(xr_ref.dtype)
        if j in normed:
            gain = hg_ref[normed.index(j)]
            for sl in range(seg // LANES):
                lanes = slice(sl * LANES, (sl + 1) * LANES)
                o_ref[:, :, j * seg + sl * LANES:j * seg + (sl + 1) * LANES] = _head_norm(
                    acc[:, lanes], gain[:, lanes], in_a).reshape(nb, ts, LANES).astype(o_ref.dtype)
        else:
            o_ref[:, :, j * seg:(j + 1) * seg] = acc.reshape(nb, ts, seg).astype(o_ref.dtype)


def _norm_matmul(x3d, shift, scale, w, head_gains, seg, normed):
    nb, length, d = x3d.shape
    n = w.shape[1]
    ts = _pick_tile(length, IN_PROJ_STEPS, 16)
    assert ts + 8 <= XR_PITCH
    whole = lambda a: pl.BlockSpec(a.shape, lambda i: (0,) * a.ndim)
    return pl.pallas_call(
        functools.partial(_norm_matmul_kernel, seg=seg, normed=normed),
        grid=(length // ts,),
        in_specs=[
            pl.BlockSpec((nb, ts, d), lambda i: (0, i, 0)),
            whole(shift),
            whole(scale),
            pl.BlockSpec((d, n), lambda i: (0, 0), pipeline_mode=pl.Buffered(1)),
            whole(head_gains),
        ],
        out_specs=[pl.BlockSpec((nb, ts, n), lambda i: (0, i, 0)),
                   pl.BlockSpec((ts, nb, seg), lambda i: (i, 0, 0))],
        out_shape=[jax.ShapeDtypeStruct((nb, length, n), BF16), jax.ShapeDtypeStruct((length, nb, seg), BF16)],
        scratch_shapes=[pltpu.VMEM((seg // LANES, nb * XR_PITCH, LANES), F32)],
        compiler_params=_params("parallel"),
        name="norm_in_proj",
    )(x3d, shift, scale, w, head_gains)


def _scan_tile_index(i, n_ctx_tiles, n_tiles, reverse):
    if not reverse:
        return i
    return jnp.where(i < n_ctx_tiles, n_ctx_tiles - 1 - i, n_tiles - 1 - (i - n_ctx_tiles))


def _rnn_kernel(cp_ref, c_ref, cn_ref, lp_ref, l_ref, ln_ref, cw_ref, cb_ref, wg_ref, bg_ref, lam_ref, *rest,
                n_ctx_tiles, n_tiles, reverse, add_prev):
    if add_prev:
        prev_ref, out_ref, a_scr, u_scr, h_scr = rest
    else:
        out_ref, a_scr, u_scr, h_scr = rest
    tl, nb, c = c_ref.shape
    i = pl.program_id(0)
    ti = _scan_tile_index(i, n_ctx_tiles, n_tiles, reverse)

    @pl.when(i == 0)
    def _():
        h_scr[...] = jnp.zeros_like(h_scr)

    in_ctx = ti < n_ctx_tiles
    pick = lambda ctx_ref, lat_ref: jnp.where(in_ctx, ctx_ref[...], lat_ref[...]).astype(F32)
    at_start = jnp.logical_or(ti == 0, ti == n_ctx_tiles)
    at_end = jnp.logical_or(ti == n_ctx_tiles - 1, ti == n_tiles - 1)
    xp = jnp.where(at_start, 0.0, pick(cp_ref, lp_ref))
    xn = jnp.where(at_end, 0.0, pick(cn_ref, ln_ref))
    xe = jnp.concatenate([xp, pick(c_ref, l_ref), xn], axis=0)
    xc = cb_ref[...] + xe[0:tl] * cw_ref[0:1, :]
    for k in range(1, CONV_TAPS):
        xc = xc + xe[k:k + tl] * cw_ref[k:k + 1, :]
    xc = xc.reshape(tl * nb, c)

    lam = lam_ref[...]
    log_sig = jnp.minimum(lam, 0.0) - jnp.log(1.0 + jnp.exp(-jnp.abs(lam)))
    k2 = (0.5 * LRU_C * np.log2(np.e)) * log_sig
    for j in range(c // MXU_DIM):
        sl = slice(MXU_DIM * j, MXU_DIM * (j + 1))
        xj = xc[:, sl]
        t = jnp.tanh(jnp.dot(xj.astype(BF16), wg_ref[j], preferred_element_type=F32) + bg_ref[j])
        a = jnp.exp2(k2[:, sl] * t[:, :MXU_DIM] + k2[:, sl])
        v = 1.0 - a * a
        gated_x = (t[:, MXU_DIM:] + 1.0) * (0.5 * xj)
        a_scr[:, sl] = a
        u_scr[:, sl] = (v * lax.rsqrt(jnp.maximum(v, SQRT_FLOOR))) * gated_x

    def step(s, h):
        tt = (tl - 1 - s) if reverse else s
        row = pl.multiple_of(tt * nb, nb)
        h = a_scr[pl.ds(row, nb), :] * h + u_scr[pl.ds(row, nb), :]
        val = h + prev_ref[tt].astype(F32) if add_prev else h
        out_ref[tt] = val.astype(out_ref.dtype)
        return h

    h_scr[...] = lax.fori_loop(0, tl, step, h_scr[...], unroll=4)


def _rnn_scan(xr_ctx, xr_lat, conv_w, conv_b, wg, bg, lam, reverse, prev):
    n_ctx, nb, c = xr_ctx.shape
    n_lat = xr_lat.shape[0]
    t = n_ctx + n_lat
    tl = _pick_tile(n_ctx, 64, 2)
    assert n_lat % tl == 0 and n_ctx % tl == 0
    n_tiles, n_ctx_tiles = t // tl, n_ctx // tl
    tile = functools.partial(_scan_tile_index, n_ctx_tiles=n_ctx_tiles, n_tiles=n_tiles, reverse=reverse)
    half = tl // CONV_LEFT
    full = lambda shape: pl.BlockSpec(shape, lambda i: (0,) * len(shape))

    def seq_specs(first_tile, length):
        last = length // tl - 1
        local = lambda i: jnp.clip(tile(i) - first_tile, 0, last)
        return [pl.BlockSpec((CONV_LEFT, nb, c), lambda i: (jnp.maximum(local(i) * half - 1, 0), 0, 0)),
                pl.BlockSpec((tl, nb, c), lambda i: (local(i), 0, 0)),
                pl.BlockSpec((1, nb, c), lambda i: (jnp.minimum((local(i) + 1) * tl, length - 1), 0, 0))]

    in_specs = seq_specs(0, n_ctx) + seq_specs(n_ctx_tiles, n_lat) + [
        full((CONV_TAPS, c)),
        full((1, c)),
        full(wg.shape),
        full(bg.shape),
        full((1, c)),
    ]
    args = [xr_ctx, xr_ctx, xr_ctx, xr_lat, xr_lat, xr_lat, conv_w, conv_b.reshape(1, c), wg, bg, lam.reshape(1, c)]
    if prev is not None:
        in_specs.append(pl.BlockSpec((tl, nb, c), lambda i: (tile(i), 0, 0)))
        args.append(prev)
    kern = functools.partial(_rnn_kernel, n_ctx_tiles=n_ctx_tiles, n_tiles=n_tiles, reverse=reverse,
                             add_prev=prev is not None)
    return pl.pallas_call(
        kern,
        grid=(n_tiles,),
        in_specs=in_specs,
        out_specs=pl.BlockSpec((tl, nb, c), lambda i: (tile(i), 0, 0)),
        out_shape=jax.ShapeDtypeStruct((t, nb, c), BF16),
        scratch_shapes=[pltpu.VMEM((tl * nb, c), F32), pltpu.VMEM((tl * nb, c), F32), pltpu.VMEM((nb, c), F32)],
        compiler_params=_params("arbitrary"),
        name="rglru_scan_rev" if reverse else "rglru_scan_fwd",
    )(*args)


def _gate_weights(rg_w_d, rg_b_d):
    _, nblk, bw, _ = rg_w_d.shape
    per = MXU_DIM // bw
    ngrp = nblk // per
    w = rg_w_d.reshape(2, ngrp, per, bw, bw)
    eye = jnp.eye(per, dtype=rg_w_d.dtype)
    dense = w[:, :, :, :, None, :] * eye[None, None, :, None, :, None]
    dense = dense.reshape(2, ngrp, MXU_DIM, MXU_DIM)
    wg = jnp.concatenate([dense[0], dense[1]], axis=-1).astype(BF16)
    b = rg_b_d.reshape(2, ngrp, 1, MXU_DIM)
    bg = jnp.concatenate([b[0], b[1]], axis=-1)
    return wg * 0.5, bg * 0.5


Q_SCALE = (HEAD_DIM ** -0.5) * float(np.log2(np.e))


def _dot_nt(a, b):
    return lax.dot_general(a, b, (((1,), (1,)), ((), ())), preferred_element_type=F32)


def _stack_heads(q, in_a):
    zero = jnp.zeros_like(q)
    return jnp.concatenate([jnp.where(in_a, q, zero), jnp.where(in_a, zero, q)], axis=0)


def _na_kernel(q_ref, k_ref, v_ref, kc_ref, vc_ref, bias_ref, o_ref, sa_scr, sb_scr, *, rows):
    cl = kc_ref.shape[0]
    gq = NA_GROUP_ROWS * GRID_W
    gk = NA_KEY_ROWS * GRID_W
    ngrp = rows // NA_GROUP_ROWS
    in_a = lax.broadcasted_iota(jnp.int32, (1, LANES), 1) < HEAD_DIM

    def key_base(g):
        kb = jnp.clip(NA_GROUP_ROWS * g - WIN_H // 2, 0, rows - NA_KEY_ROWS)
        return pl.multiple_of(kb * GRID_W, GRID_W)

    def scores(g, s_scr):
        cls = jnp.where(g == 0, 0, jnp.where(g == ngrp - 1, 2, 1))
        q2 = _stack_heads(q_ref[pl.ds(pl.multiple_of(g * gq, gq), gq), :], in_a)
        s_scr[:, 0:gk] = _dot_nt(q2, k_ref[pl.ds(key_base(g), gk), :]) + bias_ref[cls]
        s_scr[:, gk:gk + cl] = _dot_nt(q2, kc_ref[...])

    def attend(g, s_scr):
        sc = s_scr[...]
        p = jnp.exp2(sc - jnp.max(sc, axis=-1, keepdims=True))
        den = jnp.sum(p, axis=-1, keepdims=True)
        pb = p.astype(BF16)
        o2 = (jnp.dot(pb[:, 0:gk], v_ref[pl.ds(key_base(g), gk), :], preferred_element_type=F32)
              + jnp.dot(pb[:, gk:gk + cl], vc_ref[...], preferred_element_type=F32)) / den
        o_ref[pl.ds(pl.multiple_of(g * gq, gq), gq), :] = jnp.where(in_a, o2[:gq], o2[gq:]).astype(o_ref.dtype)

    scores(0, sa_scr)

    def pair(it, carry):
        g = 2 * it
        scores(g + 1, sb_scr)
        attend(g, sa_scr)
        scores(g + 2, sa_scr)
        attend(g + 1, sb_scr)
        return carry

    lax.fori_loop(0, ngrp // 2 - 1, pair, 0)
    scores(ngrp - 1, sb_scr)
    attend(ngrp - 2, sa_scr)
    attend(ngrp - 1, sb_scr)


def _bias_table_kernel(idx_ref, t_ref, o_ref):
    c = pl.program_id(1)
    left = lax.broadcasted_iota(jnp.int32, (1, LANES), 1) < GRID_W
    per_class = NA_GROUP_ROWS * NA_KEY_ROWS
    for g in range(2):
        for a in range(NA_GROUP_ROWS):
            r0 = (g * NA_GROUP_ROWS + a) * GRID_W
            for j in range(NA_KEY_ROWS // 2):
                o_even = idx_ref[c * per_class + a * NA_KEY_ROWS + 2 * j]
                o_odd = idx_ref[c * per_class + a * NA_KEY_ROWS + 2 * j + 1]
                o_ref[r0:r0 + GRID_W, j * LANES:(j + 1) * LANES] = jnp.where(left, t_ref[g, o_even], t_ref[g, o_odd])


def _na_bias_table(rpb, rows):
    nh, n_ro, n_co = rpb.shape
    ngrp = rows // NA_GROUP_ROWS
    qc = np.arange(GRID_W)[:, None]
    kc = np.arange(GRID_W)[None, :]
    wstart = np.clip(qc - WIN_W // 2, 0, GRID_W - WIN_W)
    col_valid = (kc >= wstart) & (kc < wstart + WIN_W)
    col_onehot = ((kc - qc + WIN_W - 1)[None] == np.arange(n_co)[:, None, None])
    rpb2 = rpb.astype(F32).reshape(nh // 2, 2, n_ro, n_co)
    by_col = jnp.einsum("pgrj,jqk->pgrqk", rpb2, col_onehot.astype(np.float32),
                        precision=lax.Precision.HIGHEST) * np.float32(np.log2(np.e))
    by_col = jnp.where(col_valid, by_col, MASK_VALUE)
    by_col = jnp.concatenate([by_col, jnp.full((nh // 2, 2, 1, GRID_W, GRID_W), MASK_VALUE, F32)], axis=2)
    tables = jnp.concatenate([by_col, by_col], axis=-1)
    idx = np.full((3, NA_GROUP_ROWS, NA_KEY_ROWS), n_ro, np.int32)
    for ci, g in enumerate((0, 1, ngrp - 1)):
        kb = int(np.clip(NA_GROUP_ROWS * g - WIN_H // 2, 0, rows - NA_KEY_ROWS))
        for a in range(NA_GROUP_ROWS):
            r = NA_GROUP_ROWS * g + a
            start = int(np.clip(r - WIN_H // 2, 0, rows - WIN_H))
            for cr in range(NA_KEY_ROWS):
                if start <= kb + cr < start + WIN_H:
                    idx[ci, a, cr] = kb + cr - r + WIN_H - 1
    gq, gk = NA_GROUP_ROWS * GRID_W, NA_KEY_ROWS * GRID_W
    return pl.pallas_call(
        _bias_table_kernel,
        grid_spec=pltpu.PrefetchScalarGridSpec(
            num_scalar_prefetch=1,
            grid=(nh // 2, 3),
            in_specs=[pl.BlockSpec((None,) + tables.shape[1:], lambda p, c, ix: (p, 0, 0, 0, 0))],
            out_specs=pl.BlockSpec((None, None, 2 * gq, gk), lambda p, c, ix: (p, c, 0, 0)),
        ),
        out_shape=jax.ShapeDtypeStruct((nh // 2, 3, 2 * gq, gk), F32),
        compiler_params=_params("parallel", "parallel"),
        name="na_bias_table",
    )(jnp.asarray(idx.reshape(-1)), tables)


def _neighbourhood_attention(z_lat, z_ctx, bias, seq):
    nb, s, _ = z_lat.shape
    cl = z_ctx.shape[1]
    npair = bias.shape[0]
    d_att = npair * LANES
    cb = npair
    rows = s // GRID_W
    assert s % GRID_W == 0 and rows % (2 * NA_GROUP_ROWS) == 0 and rows >= NA_KEY_ROWS + NA_GROUP_ROWS
    s_shape = (2 * NA_GROUP_ROWS * GRID_W, NA_KEY_ROWS * GRID_W + cl)
    lat = lambda seg: pl.BlockSpec((None, s, LANES), lambda p, b: (b, 0, seg * cb + p))
    ctx = lambda seg: pl.BlockSpec((None, cl, LANES), lambda p, b: (b, 0, seg * cb + p))
    return pl.pallas_call(
        functools.partial(_na_kernel, rows=rows),
        grid=(npair, nb),
        in_specs=[lat(4), lat(1), lat(2), ctx(1), ctx(2),
                  pl.BlockSpec((None,) + bias.shape[1:], lambda p, b: (p, 0, 0, 0))],
        out_specs=pl.BlockSpec((None, s, LANES), lambda p, b: (b, 0, p)),
        out_shape=jax.ShapeDtypeStruct((nb, s, d_att), BF16),
        scratch_shapes=[pltpu.VMEM(s_shape, F32), pltpu.VMEM(s_shape, F32)],
        compiler_params=_params("parallel", "parallel"),
        name="neighbourhood_attention",
    )(z_lat, z_lat, z_lat, z_ctx, z_ctx, bias)


def _ctx_attn_kernel(q_ref, k_ref, v_ref, o_ref):
    in_a = lax.broadcasted_iota(jnp.int32, (1, LANES), 1) < HEAD_DIM
    n = q_ref.shape[0]
    sc = _dot_nt(_stack_heads(q_ref[...], in_a), k_ref[...])
    p = jnp.exp2(sc - jnp.max(sc, axis=-1, keepdims=True))
    den = jnp.sum(p, axis=-1, keepdims=True)
    o2 = jnp.dot(p.astype(BF16), v_ref[...], preferred_element_type=F32) / den
    o_ref[...] = jnp.where(in_a, o2[:n], o2[n:]).astype(o_ref.dtype)


def _context_attention(z_ctx, d_att):
    nb, cl, _ = z_ctx.shape
    cb = d_att // LANES
    ctx = lambda seg: pl.BlockSpec((None, cl, LANES), lambda p, b: (b, 0, seg * cb + p))
    return pl.pallas_call(
        _ctx_attn_kernel,
        grid=(cb, nb),
        in_specs=[ctx(4), ctx(1), ctx(2)],
        out_specs=pl.BlockSpec((None, cl, LANES), lambda p, b: (b, 0, p)),
        out_shape=jax.ShapeDtypeStruct((nb, cl, d_att), BF16),
        compiler_params=_params("parallel", "parallel"),
        name="context_attention",
    )(z_ctx, z_ctx, z_ctx)


def _merge_kernel(x_ref, g_ref, hs_ref, y_ref, na_ref, gr_ref, gn_ref, wr_ref, wn_ref, wo_ref, o_ref, t_scr):
    nb, ts, d = x_ref.shape
    c = hs_ref.shape[2]
    rows2d = lambda ref: ref[...].reshape(nb * ts, ref.shape[2])
    for t in range(ts):
        hs_t = hs_ref[t].astype(F32)
        for sl in range(c // LANES):
            t_scr.at[sl][pl.ds(t, nb, stride=XR_PITCH), :] = hs_t[:, sl * LANES:(sl + 1) * LANES]
    hs = jnp.concatenate([jnp.concatenate([t_scr[sl, b * XR_PITCH:b * XR_PITCH + ts, :] for sl in range(c // LANES)],
                                          axis=-1) for b in range(nb)], axis=0)
    y_rnn = (hs * _gelu_tanh(rows2d(y_ref).astype(F32))).astype(BF16)
    t_rnn = jnp.dot(y_rnn, wr_ref[...], preferred_element_type=F32)
    t_na = jnp.dot(rows2d(na_ref), wn_ref[...], preferred_element_type=F32)
    mix = _sigmoid(rows2d(gr_ref).astype(F32)) * t_rnn + _sigmoid(rows2d(gn_ref).astype(F32)) * t_na
    out = jnp.dot(mix.astype(BF16), wo_ref[...], preferred_element_type=F32)
    o_ref[...] = x_ref[...] + g_ref[...] * out.reshape(nb, ts, d)


def _merge(x3d, gate, hs_tm, t_off, z3, na3, w_rnn_o, w_na_o, w_out):
    nb, length, d = x3d.shape
    c = hs_tm.shape[2]
    da = na3.shape[2]
    ts = _pick_tile(length, IN_PROJ_STEPS, 16)
    assert ts + 8 <= XR_PITCH and t_off % ts == 0
    y_blk = (c + 2 * da) // c
    gr_blk = (2 * c + 3 * da) // d
    gn_blk = gr_blk + 1
    row = lambda width, blk=0: pl.BlockSpec((nb, ts, width), lambda i: (0, i, blk))
    whole = lambda a: pl.BlockSpec(a.shape, lambda i: (0,) * a.ndim, pipeline_mode=pl.Buffered(1))
    return pl.pallas_call(
        _merge_kernel,
        grid=(length // ts,),
        in_specs=[row(d), pl.BlockSpec(gate.shape, lambda i: (0, 0, 0)),
                  pl.BlockSpec((ts, nb, c), lambda i: (i + t_off // ts, 0, 0)),
                  row(c, y_blk), row(da), row(d, gr_blk), row(d, gn_blk),
                  whole(w_rnn_o), whole(w_na_o), whole(w_out)],
        out_specs=row(d),
        out_shape=jax.ShapeDtypeStruct((nb, length, d), F32),
        scratch_shapes=[pltpu.VMEM((c // LANES, nb * XR_PITCH, LANES), F32)],
        compiler_params=_params("parallel"),
        name="merge_out_proj",
    )(x3d, gate, hs_tm, z3, na3, z3, z3, w_rnn_o, w_na_o, w_out)


def _ffn_kernel(x_ref, sh_ref, sc_ref, g_ref, w1_ref, w3_ref, w2_ref, o_ref, *, tf):
    x = x_ref[...]
    h = _modulated_norm(x, sh_ref[...], sc_ref[...]).astype(BF16)
    acc = None
    for j in range(w1_ref.shape[1] // tf):
        cols = slice(j * tf, (j + 1) * tf)
        a = jnp.dot(h, w1_ref[:, cols], preferred_element_type=F32)
        b = jnp.dot(h, w3_ref[:, cols], preferred_element_type=F32)
        part = jnp.dot((_silu(a) * b).astype(BF16), w2_ref[cols, :], preferred_element_type=F32)
        acc = part if acc is None else acc + part
    o_ref[...] = x + g_ref[...] * acc


def _dense_ffn(x2d, shift, scale, gate, w1, w3, w2, rows_per_mod):
    m, d = x2d.shape
    dff = w1.shape[1]
    tm = _pick_tile(rows_per_mod, 512, 8)
    tf = _pick_tile(dff, 1536, LANES)
    tiles_per_mod = rows_per_mod // tm
    mod = pl.BlockSpec((None, 1, d), lambda i: (i // tiles_per_mod, 0, 0))
    whole = lambda w: pl.BlockSpec(w.shape, lambda i: (0, 0), pipeline_mode=pl.Buffered(1))
    return pl.pallas_call(
        functools.partial(_ffn_kernel, tf=tf),
        grid=(m // tm,),
        in_specs=[pl.BlockSpec((tm, d), lambda i: (i, 0)), mod, mod, mod, whole(w1), whole(w3), whole(w2)],
        out_specs=pl.BlockSpec((tm, d), lambda i: (i, 0)),
        out_shape=jax.ShapeDtypeStruct((m, d), F32),
        compiler_params=_params("parallel"),
        name="dense_swiglu",
    )(x2d, shift, scale, gate, w1, w3, w2)


SUBLANES = 8
GATHER_DMA_PRIORITY = 1


def _to_token_tiles(ref, x):
    n = x.shape[0]
    for s in range(SUBLANES):
        ref[pl.ds(s, n, stride=SUBLANES), :] = x[:, s * LANES:(s + 1) * LANES]


def _from_token_tiles(ref, n, s):
    return ref[pl.ds(s, n, stride=SUBLANES), :]


def _route_kernel(x_ref, sh_ref, sc_ref, wr_ref, wrl_ref, h_ref, r_ref, *, n_experts):
    h = _modulated_norm(x_ref[...], sh_ref[...], sc_ref[...])
    _to_token_tiles(h_ref, h)
    h_hi = h.astype(BF16)
    h_lo = (h - h_hi.astype(F32)).astype(BF16)
    logits = (jnp.dot(h_hi, wr_ref[...], preferred_element_type=F32)
              + jnp.dot(h_lo, wr_ref[...], preferred_element_type=F32)
              + jnp.dot(h_hi, wrl_ref[...], preferred_element_type=F32))
    lane = lax.broadcasted_iota(jnp.int32, logits.shape, 1).astype(F32)
    neg = -jnp.inf
    lg = jnp.where(lane < n_experts, logits, neg)
    m1 = jnp.max(lg, axis=-1, keepdims=True)
    i1 = jnp.min(jnp.where(lg == m1, lane, float(LANES)), axis=-1, keepdims=True)
    lg2 = jnp.where(lane == i1, neg, lg)
    m2 = jnp.max(lg2, axis=-1, keepdims=True)
    i2 = jnp.min(jnp.where(lg2 == m2, lane, float(LANES)), axis=-1, keepdims=True)
    e = jnp.exp(m2 - m1)
    w1 = 1.0 / (1.0 + e)
    w2 = e / (1.0 + e)
    r_ref[...] = jnp.where(lane == 0, i1, jnp.where(lane == 1, i2, jnp.where(lane == 2, w1,
                           jnp.where(lane == 3, w2, 0.0))))


def _route(x2d, shift, scale, router, rows_per_mod):
    m, d = x2d.shape
    assert d == SUBLANES * LANES
    n_experts = router.shape[1]
    wr32 = jnp.zeros((d, LANES), F32).at[:, :n_experts].set(router)
    wr = wr32.astype(BF16)
    wrl = (wr32 - wr.astype(F32)).astype(BF16)
    tm = _pick_tile(rows_per_mod, 512, 8)
    tiles_per_mod = rows_per_mod // tm
    mod = pl.BlockSpec((None, 1, d), lambda i: (i // tiles_per_mod, 0, 0))
    rspec = pl.BlockSpec((d, LANES), lambda i: (0, 0))
    return pl.pallas_call(
        functools.partial(_route_kernel, n_experts=n_experts),
        grid=(m // tm,),
        in_specs=[pl.BlockSpec((tm, d), lambda i: (i, 0)), mod, mod, rspec, rspec],
        out_specs=[pl.BlockSpec((tm * SUBLANES, LANES), lambda i: (i, 0)),
                   pl.BlockSpec((tm, LANES), lambda i: (i, 0))],
        out_shape=[jax.ShapeDtypeStruct((m * SUBLANES, LANES), F32), jax.ShapeDtypeStruct((m, LANES), F32)],
        compiler_params=_params("parallel"),
        name="moe_route",
    )(x2d, shift, scale, wr, wrl)


EXPERT_TILE_ROWS = 1024


def _expert_ffn_kernel(te_ref, nv_ref, src_ref, h_hbm, w1_ref, w3_ref, w2_ref, o_ref, xbuf, sem, h_scr, acc_scr):
    t = pl.program_id(0)
    f = pl.program_id(1)
    nv = nv_ref[0]
    tm = h_scr.shape[0]

    def issue_rows(tile, slot):
        base = tile * tm

        def body(i, carry):
            src_row = pl.multiple_of(src_ref[base + i] * SUBLANES, SUBLANES)
            dst_row = pl.multiple_of(i * SUBLANES, SUBLANES)
            pltpu.make_async_copy(h_hbm.at[pl.ds(src_row, SUBLANES), :],
                                  xbuf.at[slot, pl.ds(dst_row, SUBLANES), :],
                                  sem.at[slot]).start(priority=GATHER_DMA_PRIORITY)
            return carry

        lax.fori_loop(0, tm, body, 0, unroll=8)

    def wait_rows(slot):
        pltpu.make_async_copy(h_hbm.at[pl.ds(0, tm * SUBLANES), :], xbuf.at[slot], sem.at[slot]).wait()

    @pl.when(t < nv)
    def _():
        @pl.when(f == 0)
        def _():
            slot = t % 2

            @pl.when(t == 0)
            def _():
                issue_rows(0, 0)

            wait_rows(slot)
            for s in range(SUBLANES):
                h_scr[:, s * LANES:(s + 1) * LANES] = _from_token_tiles(xbuf.at[slot], tm, s).astype(BF16)
            acc_scr[...] = jnp.zeros_like(acc_scr)

            @pl.when(t + 1 < nv)
            def _():
                issue_rows(t + 1, 1 - slot)

        h = h_scr[...]
        a = jnp.dot(h, w1_ref[...].astype(BF16), preferred_element_type=F32)
        b = jnp.dot(h, w3_ref[...].astype(BF16), preferred_element_type=F32)
        acc_scr[...] += jnp.dot((_silu(a) * b).astype(BF16), w2_ref[...].astype(BF16),
                                preferred_element_type=F32)

        @pl.when(f == pl.num_programs(1) - 1)
        def _():
            _to_token_tiles(o_ref, acc_scr[...])

    @pl.when(jnp.logical_and(t >= nv, f == pl.num_programs(1) - 1))
    def _():
        o_ref[...] = jnp.zeros_like(o_ref)


def _expert_ffn(h, src, tile_expert, n_valid, w1, w3, w2):
    tm = EXPERT_TILE_ROWS
    n = src.shape[0]
    d = w1.shape[1]
    dfe = w1.shape[2]
    tf = _pick_tile(dfe, 512, LANES)
    nf = dfe // tf

    def fsel(t, f, nv):
        return jnp.where(t < nv[0], f, nf - 1)

    return pl.pallas_call(
        _expert_ffn_kernel,
        grid_spec=pltpu.PrefetchScalarGridSpec(
            num_scalar_prefetch=3,
            grid=(n // tm, nf),
            in_specs=[pl.BlockSpec(memory_space=pl.ANY),
                      pl.BlockSpec((None, d, tf), lambda t, f, te, nv, sr: (te[t], 0, fsel(t, f, nv))),
                      pl.BlockSpec((None, d, tf), lambda t, f, te, nv, sr: (te[t], 0, fsel(t, f, nv))),
                      pl.BlockSpec((None, tf, d), lambda t, f, te, nv, sr: (te[t], fsel(t, f, nv), 0))],
            out_specs=pl.BlockSpec((tm * SUBLANES, LANES), lambda t, f, te, nv, sr: (t, 0)),
            scratch_shapes=[pltpu.VMEM((2, tm * SUBLANES, LANES), F32), pltpu.SemaphoreType.DMA((2,)),
                            pltpu.VMEM((tm, d), BF16), pltpu.VMEM((tm, d), F32)],
        ),
        out_shape=jax.ShapeDtypeStruct((n * SUBLANES, LANES), F32),
        compiler_params=_params("arbitrary", "arbitrary"),
        name="expert_swiglu",
    )(tile_expert, n_valid, src, h, w1, w3, w2)


def _combine_kernel(pos_ref, x_ref, g_ref, r_ref, ys_hbm, o_ref, ybuf, sem):
    i = pl.program_id(0)
    n = pl.num_programs(0)
    tm = x_ref.shape[0]
    m = n * tm

    def issue_rows(tile, slot):
        for k in range(TOP_K):
            base = k * m + tile * tm

            def body(j, carry):
                src_row = pl.multiple_of(pos_ref[base + j] * SUBLANES, SUBLANES)
                dst_row = pl.multiple_of(j * SUBLANES, SUBLANES)
                pltpu.make_async_copy(ys_hbm.at[pl.ds(src_row, SUBLANES), :],
                                      ybuf.at[slot, k, pl.ds(dst_row, SUBLANES), :],
                                      sem.at[slot]).start(priority=GATHER_DMA_PRIORITY)
                return carry

            lax.fori_loop(0, tm, body, 0, unroll=8)

    slot = i % 2

    @pl.when(i == 0)
    def _():
        issue_rows(0, 0)

    pltpu.make_async_copy(ybuf.at[1 - slot], ybuf.at[slot], sem.at[slot]).wait()

    @pl.when(i + 1 < n)
    def _():
        issue_rows(i + 1, 1 - slot)

    r = r_ref[...]
    w1, w2 = r[:, 2:3], r[:, 3:4]
    for s in range(SUBLANES):
        lanes = slice(s * LANES, (s + 1) * LANES)
        mix = (w1 * _from_token_tiles(ybuf.at[slot, 0], tm, s) + w2 * _from_token_tiles(ybuf.at[slot, 1], tm, s))
        o_ref[:, lanes] = x_ref[:, lanes] + g_ref[:, lanes] * mix


def _combine(x2d, gate, route, ys, pos, rows_per_mod):
    m, d = x2d.shape
    tm = _pick_tile(rows_per_mod, 512, 8)
    tiles_per_mod = rows_per_mod // tm
    return pl.pallas_call(
        _combine_kernel,
        grid_spec=pltpu.PrefetchScalarGridSpec(
            num_scalar_prefetch=1,
            grid=(m // tm,),
            in_specs=[pl.BlockSpec((tm, d), lambda i, ps: (i, 0)),
                      pl.BlockSpec((None, 1, d), lambda i, ps: (i // tiles_per_mod, 0, 0)),
                      pl.BlockSpec((tm, LANES), lambda i, ps: (i, 0)),
                      pl.BlockSpec(memory_space=pl.ANY)],
            out_specs=pl.BlockSpec((tm, d), lambda i, ps: (i, 0)),
            scratch_shapes=[pltpu.VMEM((2, TOP_K, tm * SUBLANES, LANES), F32), pltpu.SemaphoreType.DMA((2,))],
        ),
        out_shape=jax.ShapeDtypeStruct((m, d), F32),
        compiler_params=_params("arbitrary"),
        name="moe_combine",
    )(pos, x2d, gate, route, ys)


def _moe_ffn(x2d, shift, scale, gate, router, w1, w3, w2, rows_per_mod):
    m, d = x2d.shape
    n_experts = router.shape[1]
    h, route = _route(x2d, shift, scale, router, rows_per_mod)
    tm = EXPERT_TILE_ROWS
    expert = route[:, :TOP_K].astype(jnp.int32).T.reshape(-1)
    onehot = (expert[:, None] == jnp.arange(n_experts)[None, :]).astype(jnp.int32)
    csum = jnp.cumsum(onehot, axis=0)
    rank = jnp.sum(onehot * (csum - 1), axis=1)
    tiles = (csum[-1] + tm - 1) // tm
    tile_end = jnp.cumsum(tiles)
    pos = (jnp.sum(onehot * ((tile_end - tiles) * tm)[None, :], axis=1) + rank).astype(jnp.int32)
    n_tiles = (TOP_K * m) // tm + n_experts
    src = jnp.zeros((n_tiles * tm,), jnp.int32).at[pos].set(jnp.arange(TOP_K * m, dtype=jnp.int32) % m)
    tile_expert = jnp.minimum(jnp.sum((jnp.arange(n_tiles)[:, None] >= tile_end[None, :]).astype(jnp.int32), axis=1),
                              n_experts - 1).astype(jnp.int32)
    n_valid = tile_end[-1:].astype(jnp.int32)
    ys = _expert_ffn(h, src, tile_expert, n_valid, w1, w3, w2)
    return _combine(x2d, gate, route, ys, pos, rows_per_mod)


def kernel(x, c, ctx, c_ctx, w_mod, b_mod, w_in, conv_w, conv_b, rg_lambda, rg_w, rg_b, q_gain, k_gain, rpb,
           w_rnn_o, w_na_o, w_out, ffn_w1, ffn_w3, ffn_w2, router, moe_w1, moe_w3, moe_w2):
    nb, seq, d = x.shape
    cl = ctx.shape[1]
    depth = w_mod.shape[0]
    c_rnn = conv_w.shape[2]
    d_att = rpb.shape[1] * HEAD_DIM
    ctx_cols = c_rnn + 2 * d_att
    rows = seq // GRID_W

    n_cond = -(-(nb + 1) // 8) * 8
    cond = jnp.zeros((n_cond, d), F32).at[:nb].set(c).at[nb].set(c_ctx)
    mods = _adaln(cond, w_mod, b_mod)

    x2 = x.reshape(nb * seq, d)
    xc2 = ctx.reshape(nb * cl, d)
    for l in range(depth):
        ctx_out = l < depth - 1
        lat = [mods[l, :nb, k * d:(k + 1) * d].reshape(nb, 1, d) for k in range(N_MOD)]
        cmod = [mods[l, nb:nb + 1, k * d:(k + 1) * d].reshape(1, 1, d) for k in range(N_MOD)]
        w_in_l = w_in[l].astype(BF16)
        tile_heads = lambda g: jnp.tile(g.astype(F32), c_rnn // HEAD_DIM).reshape(1, c_rnn)
        head_gains = jnp.stack([tile_heads(k_gain[l]), tile_heads(q_gain[l]) * Q_SCALE])

        z_lat3, xr_lat = _norm_matmul(x2.reshape(nb, seq, d), lat[0], lat[1], w_in_l, head_gains, c_rnn, (1, 4))
        if ctx_out:
            z_ctx3, xr_ctx = _norm_matmul(xc2.reshape(nb, cl, d), cmod[0], cmod[1], w_in_l, head_gains, c_rnn, (1, 4))
        else:
            z_ctx3, xr_ctx = _norm_matmul(xc2.reshape(nb, cl, d), cmod[0], cmod[1], w_in_l[:, :ctx_cols],
                                          head_gains, c_rnn, (1,))

        wg_f, bg_f = _gate_weights(rg_w[l, 0], rg_b[l, 0])
        wg_r, bg_r = _gate_weights(rg_w[l, 1], rg_b[l, 1])
        hs_r = _rnn_scan(xr_ctx, xr_lat, conv_w[l], conv_b[l], wg_r, bg_r, rg_lambda[l, 1], True, None)
        hs = _rnn_scan(xr_ctx, xr_lat, conv_w[l], conv_b[l], wg_f, bg_f, rg_lambda[l, 0], False, hs_r)

        bias = _na_bias_table(rpb[l], rows)
        na_lat = _neighbourhood_attention(z_lat3, z_ctx3, bias, seq)

        wr, wn, wo = w_rnn_o[l].astype(BF16), w_na_o[l].astype(BF16), w_out[l].astype(BF16)
        x2 = _merge(x2.reshape(nb, seq, d), lat[2], hs, cl, z_lat3, na_lat, wr, wn, wo).reshape(nb * seq, d)
        if ctx_out:
            na_ctx = _context_attention(z_ctx3, d_att)
            xc2 = _merge(xc2.reshape(nb, cl, d), cmod[2], hs, 0, z_ctx3, na_ctx, wr, wn, wo).reshape(nb * cl, d)

        j = l // 2
        if l % 2 == 0:
            w1, w3, w2 = ffn_w1[j].astype(BF16), ffn_w3[j].astype(BF16), ffn_w2[j].astype(BF16)
            x2 = _dense_ffn(x2, lat[3], lat[4], lat[5], w1, w3, w2, seq)
            if ctx_out:
                xc2 = _dense_ffn(xc2, cmod[3], cmod[4], cmod[5], w1, w3, w2, nb * cl)
        else:
            w1, w3, w2 = moe_w1[j], moe_w3[j], moe_w2[j]
            x2 = _moe_ffn(x2, lat[3], lat[4], lat[5], router[j], w1, w3, w2, seq)
            if ctx_out:
                xc2 = _moe_ffn(xc2, cmod[3], cmod[4], cmod[5], router[j], w1, w3, w2, nb * cl)
    return x2.reshape(nb, seq, d)
```

```python
import functools

import numpy as np
import jax
import jax.numpy as jnp
from jax import lax
from jax.experimental import pallas as pl
from jax.experimental.pallas import tpu as pltpu

F32 = jnp.float32
BF16 = jnp.bfloat16

EPS = 1e-6
N_MOD = 6
GRID_W = 64
WIN_H = 8
WIN_W = 16
HEAD_DIM = 64
RNN_BW = 64
CONV_TAPS = 4
CONV_LEFT = 2
LRU_C = 8.0
TOP_K = 2
MASK_VALUE = -1e30
SQRT_FLOOR = 1e-30

LANES = 128
MXU_DIM = 256
VMEM_LIMIT_BYTES = 56 * 1024 * 1024

NA_GROUP_ROWS = 4
NA_KEY_ROWS = 12


def _params(*sem):
    return pltpu.CompilerParams(dimension_semantics=sem, vmem_limit_bytes=VMEM_LIMIT_BYTES)


def _sigmoid(x):
    return 0.5 * (jnp.tanh(0.5 * x) + 1.0)


def _silu(x):
    return x * _sigmoid(x)


def _gelu_tanh(x):
    return 0.5 * x * (1.0 + jnp.tanh(np.sqrt(2.0 / np.pi) * (x + 0.044715 * (x * x * x))))


def _modulated_norm(x, shift, scale):
    ms = jnp.mean(x * x, axis=-1, keepdims=True)
    return x * lax.rsqrt(ms + EPS) * (1.0 + scale) + shift


def _pick_tile(n, cap, mult):
    best = None
    for t in range(mult, min(n, cap) + 1, mult):
        if n % t == 0:
            best = t
    assert best is not None, (n, cap, mult)
    return best


def _adaln_kernel(c_ref, w_ref, b_ref, o_ref):
    s = _silu(c_ref[...]).astype(BF16)
    o_ref[...] = jnp.dot(s, w_ref[...].astype(BF16), preferred_element_type=F32) + b_ref[...]


def _adaln(cond, w_mod, b_mod):
    depth, d, n = w_mod.shape
    r = cond.shape[0]
    tn = _pick_tile(n, 1536, LANES)
    return pl.pallas_call(
        _adaln_kernel,
        grid=(depth, n // tn),
        in_specs=[
            pl.BlockSpec((r, d), lambda l, j: (0, 0)),
            pl.BlockSpec((None, d, tn), lambda l, j: (l, 0, j)),
            pl.BlockSpec((None, 1, tn), lambda l, j: (l, 0, j)),
        ],
        out_specs=pl.BlockSpec((None, r, tn), lambda l, j: (l, 0, j)),
        out_shape=jax.ShapeDtypeStruct((depth, r, n), F32),
        compiler_params=_params("parallel", "parallel"),
        name="adaln",
    )(cond, w_mod, b_mod.reshape(depth, 1, n))


def _head_norm(x, gain, in_a):
    sq = x * x
    sa = jnp.sum(jnp.where(in_a, sq, 0.0), axis=-1, keepdims=True)
    sb = jnp.sum(jnp.where(in_a, 0.0, sq), axis=-1, keepdims=True)
    ms = jnp.where(in_a, sa, sb) * (1.0 / HEAD_DIM)
    return x * lax.rsqrt(ms + EPS) * gain


IN_PROJ_STEPS = 32
XR_PITCH = 40


def _norm_matmul_kernel(x_ref, sh_ref, sc_ref, w_ref, hg_ref, o_ref, xr_ref, t_scr, *, seg, normed):
    nb, ts, d = x_ref.shape
    h = _modulated_norm(x_ref[...], sh_ref[...], sc_ref[...]).reshape(nb * ts, d).astype(BF16)
    in_a = lax.broadcasted_iota(jnp.int32, (1, LANES), 1) < HEAD_DIM
    for j in range(w_ref.shape[1] // seg):
        acc = jnp.dot(h, w_ref[:, j * seg:(j + 1) * seg], preferred_element_type=F32)
        if j == 0:
            for sl in range(seg // LANES):
                for b in range(nb):
                    t_scr[sl, b * XR_PITCH:b * XR_PITCH + ts, :] = acc[b * ts:(b + 1) * ts, sl * LANES:(sl + 1) * LANES]
            for t in range(ts):
                xr_ref[t] = jnp.concatenate([t_scr.at[sl][pl.ds(t, nb, stride=XR_PITCH), :]
                                             for sl in range(seg // LANES)], axis=-1).astype(xr_ref.dtype)
        if j in normed:
            gain = hg_ref[normed.index(j)]
            for sl in range(seg // LANES):
                lanes = slice(sl * LANES, (sl + 1) * LANES)
                o_ref[:, :, j * seg + sl * LANES:j * seg + (sl + 1) * LANES] = _head_norm(
                    acc[:, lanes], gain[:, lanes], in_a).reshape(nb, ts, LANES).astype(o_ref.dtype)
        else:
            o_ref[:, :, j * seg:(j + 1) * seg] = acc.reshape(nb, ts, seg).astype(o_ref.dtype)


def _norm_matmul(x3d, shift, scale, w, head_gains, seg, normed):
    nb, length, d = x3d.shape
    n = w.shape[1]
    ts = _pick_tile(length, IN_PROJ_STEPS, 16)
    assert ts + 8 <= XR_PITCH
    whole = lambda a: pl.BlockSpec(a.shape, lambda i: (0,) * a.ndim)
    return pl.pallas_call(
        functools.partial(_norm_matmul_kernel, seg=seg, normed=normed),
        grid=(length // ts,),
        in_specs=[
            pl.BlockSpec((nb, ts, d), lambda i: (0, i, 0)),
            whole(shift),
            whole(scale),
            pl.BlockSpec((d, n), lambda i: (0, 0), pipeline_mode=pl.Buffered(1)),
            whole(head_gains),
        ],
        out_specs=[pl.BlockSpec((nb, ts, n), lambda i: (0, i, 0)),
                   pl.BlockSpec((ts, nb, seg), lambda i: (i, 0, 0))],
        out_shape=[jax.ShapeDtypeStruct((nb, length, n), BF16), jax.ShapeDtypeStruct((length, nb, seg), BF16)],
        scratch_shapes=[pltpu.VMEM((seg // LANES, nb * XR_PITCH, LANES), F32)],
        compiler_params=_params("parallel"),
        name="norm_in_proj",
    )(x3d, shift, scale, w, head_gains)


def _scan_tile_index(i, n_ctx_tiles, n_tiles, reverse):
    if not reverse:
        return i
    return jnp.where(i < n_ctx_tiles, n_ctx_tiles - 1 - i, n_tiles - 1 - (i - n_ctx_tiles))


def _rnn_kernel(cp_ref, c_ref, cn_ref, lp_ref, l_ref, ln_ref, cw_ref, cb_ref, wg_ref, bg_ref, lam_ref, *rest,
                n_ctx_tiles, n_tiles, reverse, add_prev):
    if add_prev:
        prev_ref, out_ref, a_scr, u_scr, h_scr = rest
    else:
        out_ref, a_scr, u_scr, h_scr = rest
    tl, nb, c = c_ref.shape
    i = pl.program_id(0)
    ti = _scan_tile_index(i, n_ctx_tiles, n_tiles, reverse)

    @pl.when(i == 0)
    def _():
        h_scr[...] = jnp.zeros_like(h_scr)

    in_ctx = ti < n_ctx_tiles
    pick = lambda ctx_ref, lat_ref: jnp.where(in_ctx, ctx_ref[...], lat_ref[...]).astype(F32)
    at_start = jnp.logical_or(ti == 0, ti == n_ctx_tiles)
    at_end = jnp.logical_or(ti == n_ctx_tiles - 1, ti == n_tiles - 1)
    xp = jnp.where(at_start, 0.0, pick(cp_ref, lp_ref))
    xn = jnp.where(at_end, 0.0, pick(cn_ref, ln_ref))
    xe = jnp.concatenate([xp, pick(c_ref, l_ref), xn], axis=0)
    xc = cb_ref[...] + xe[0:tl] * cw_ref[0:1, :]
    for k in range(1, CONV_TAPS):
        xc = xc + xe[k:k + tl] * cw_ref[k:k + 1, :]
    xc = xc.reshape(tl * nb, c)

    lam = lam_ref[...]
    log_sig = jnp.minimum(lam, 0.0) - jnp.log(1.0 + jnp.exp(-jnp.abs(lam)))
    k2 = (0.5 * LRU_C * np.log2(np.e)) * log_sig
    for j in range(c // MXU_DIM):
        sl = slice(MXU_DIM * j, MXU_DIM * (j + 1))
        xj = xc[:, sl]
        t = jnp.tanh(jnp.dot(xj.astype(BF16), wg_ref[j], preferred_element_type=F32) + bg_ref[j])
        a = jnp.exp2(k2[:, sl] * t[:, :MXU_DIM] + k2[:, sl])
        v = 1.0 - a * a
        gated_x = (t[:, MXU_DIM:] + 1.0) * (0.5 * xj)
        a_scr[:, sl] = a
        u_scr[:, sl] = (v * lax.rsqrt(jnp.maximum(v, SQRT_FLOOR))) * gated_x

    def step(s, h):
        tt = (tl - 1 - s) if reverse else s
        row = pl.multiple_of(tt * nb, nb)
        h = a_scr[pl.ds(row, nb), :] * h + u_scr[pl.ds(row, nb), :]
        val = h + prev_ref[tt].astype(F32) if add_prev else h
        out_ref[tt] = val.astype(out_ref.dtype)
        return h

    h_scr[...] = lax.fori_loop(0, tl, step, h_scr[...], unroll=4)


def _rnn_scan(xr_ctx, xr_lat, conv_w, conv_b, wg, bg, lam, reverse, prev):
    n_ctx, nb, c = xr_ctx.shape
    n_lat = xr_lat.shape[0]
    t = n_ctx + n_lat
    tl = _pick_tile(n_ctx, 64, 2)
    assert n_lat % tl == 0 and n_ctx % tl == 0
    n_tiles, n_ctx_tiles = t // tl, n_ctx // tl
    tile = functools.partial(_scan_tile_index, n_ctx_tiles=n_ctx_tiles, n_tiles=n_tiles, reverse=reverse)
    half = tl // CONV_LEFT
    full = lambda shape: pl.BlockSpec(shape, lambda i: (0,) * len(shape))

    def seq_specs(first_tile, length):
        last = length // tl - 1
        local = lambda i: jnp.clip(tile(i) - first_tile, 0, last)
        return [pl.BlockSpec((CONV_LEFT, nb, c), lambda i: (jnp.maximum(local(i) * half - 1, 0), 0, 0)),
                pl.BlockSpec((tl, nb, c), lambda i: (local(i), 0, 0)),
                pl.BlockSpec((1, nb, c), lambda i: (jnp.minimum((local(i) + 1) * tl, length - 1), 0, 0))]

    in_specs = seq_specs(0, n_ctx) + seq_specs(n_ctx_tiles, n_lat) + [
        full((CONV_TAPS, c)),
        full((1, c)),
        full(wg.shape),
        full(bg.shape),
        full((1, c)),
    ]
    args = [xr_ctx, xr_ctx, xr_ctx, xr_lat, xr_lat, xr_lat, conv_w, conv_b.reshape(1, c), wg, bg, lam.reshape(1, c)]
    if prev is not None:
        in_specs.append(pl.BlockSpec((tl, nb, c), lambda i: (tile(i), 0, 0)))
        args.append(prev)
    kern = functools.partial(_rnn_kernel, n_ctx_tiles=n_ctx_tiles, n_tiles=n_tiles, reverse=reverse,
                             add_prev=prev is not None)
    return pl.pallas_call(
        kern,
        grid=(n_tiles,),
        in_specs=in_specs,
        out_specs=pl.BlockSpec((tl, nb, c), lambda i: (tile(i), 0, 0)),
        out_shape=jax.ShapeDtypeStruct((t, nb, c), BF16),
        scratch_shapes=[pltpu.VMEM((tl * nb, c), F32), pltpu.VMEM((tl * nb, c), F32), pltpu.VMEM((nb, c), F32)],
        compiler_params=_params("arbitrary"),
        name="rglru_scan_rev" if reverse else "rglru_scan_fwd",
    )(*args)


def _gate_weights(rg_w_d, rg_b_d):
    _, nblk, bw, _ = rg_w_d.shape
    per = MXU_DIM // bw
    ngrp = nblk // per
    w = rg_w_d.reshape(2, ngrp, per, bw, bw)
    eye = jnp.eye(per, dtype=rg_w_d.dtype)
    dense = w[:, :, :, :, None, :] * eye[None, None, :, None, :, None]
    dense = dense.reshape(2, ngrp, MXU_DIM, MXU_DIM)
    wg = jnp.concatenate([dense[0], dense[1]], axis=-1).astype(BF16)
    b = rg_b_d.reshape(2, ngrp, 1, MXU_DIM)
    bg = jnp.concatenate([b[0], b[1]], axis=-1)
    return wg * 0.5, bg * 0.5


Q_SCALE = (HEAD_DIM ** -0.5) * float(np.log2(np.e))


def _dot_nt(a, b):
    return lax.dot_general(a, b, (((1,), (1,)), ((), ())), preferred_element_type=F32)


def _stack_heads(q, in_a):
    zero = jnp.zeros_like(q)
    return jnp.concatenate([jnp.where(in_a, q, zero), jnp.where(in_a, zero, q)], axis=0)


def _na_kernel(q_ref, k_ref, v_ref, kc_ref, vc_ref, bias_ref, o_ref, sa_scr, sb_scr, *, rows):
    cl = kc_ref.shape[0]
    gq = NA_GROUP_ROWS * GRID_W
    gk = NA_KEY_ROWS * GRID_W
    ngrp = rows // NA_GROUP_ROWS
    in_a = lax.broadcasted_iota(jnp.int32, (1, LANES), 1) < HEAD_DIM

    def key_base(g):
        kb = jnp.clip(NA_GROUP_ROWS * g - WIN_H // 2, 0, rows - NA_KEY_ROWS)
        return pl.multiple_of(kb * GRID_W, GRID_W)

    def scores(g, s_scr):
        cls = jnp.where(g == 0, 0, jnp.where(g == ngrp - 1, 2, 1))
        q2 = _stack_heads(q_ref[pl.ds(pl.multiple_of(g * gq, gq), gq), :], in_a)
        s_scr[:, 0:gk] = _dot_nt(q2, k_ref[pl.ds(key_base(g), gk), :]) + bias_ref[cls]
        s_scr[:, gk:gk + cl] = _dot_nt(q2, kc_ref[...])

    def attend(g, s_scr):
        sc = s_scr[...]
        p = jnp.exp2(sc - jnp.max(sc, axis=-1, keepdims=True))
        den = jnp.sum(p, axis=-1, keepdims=True)
        pb = p.astype(BF16)
        o2 = (jnp.dot(pb[:, 0:gk], v_ref[pl.ds(key_base(g), gk), :], preferred_element_type=F32)
              + jnp.dot(pb[:, gk:gk + cl], vc_ref[...], preferred_element_type=F32)) / den
        o_ref[pl.ds(pl.multiple_of(g * gq, gq), gq), :] = jnp.where(in_a, o2[:gq], o2[gq:]).astype(o_ref.dtype)

    scores(0, sa_scr)

    def pair(it, carry):
        g = 2 * it
        scores(g + 1, sb_scr)
        attend(g, sa_scr)
        scores(g + 2, sa_scr)
        attend(g + 1, sb_scr)
        return carry

    lax.fori_loop(0, ngrp // 2 - 1, pair, 0)
    scores(ngrp - 1, sb_scr)
    attend(ngrp - 2, sa_scr)
    attend(ngrp - 1, sb_scr)


def _bias_table_kernel(idx_ref, t_ref, o_ref):
    c = pl.program_id(1)
    left = lax.broadcasted_iota(jnp.int32, (1, LANES), 1) < GRID_W
    per_class = NA_GROUP_ROWS * NA_KEY_ROWS
    for g in range(2):
        for a in range(NA_GROUP_ROWS):
            r0 = (g * NA_GROUP_ROWS + a) * GRID_W
            for j in range(NA_KEY_ROWS // 2):
                o_even = idx_ref[c * per_class + a * NA_KEY_ROWS + 2 * j]
                o_odd = idx_ref[c * per_class + a * NA_KEY_ROWS + 2 * j + 1]
                o_ref[r0:r0 + GRID_W, j * LANES:(j + 1) * LANES] = jnp.where(left, t_ref[g, o_even], t_ref[g, o_odd])


def _na_bias_table(rpb, rows):
    nh, n_ro, n_co = rpb.shape
    ngrp = rows // NA_GROUP_ROWS
    qc = np.arange(GRID_W)[:, None]
    kc = np.arange(GRID_W)[None, :]
    wstart = np.clip(qc - WIN_W // 2, 0, GRID_W - WIN_W)
    col_valid = (kc >= wstart) & (kc < wstart + WIN_W)
    col_onehot = ((kc - qc + WIN_W - 1)[None] == np.arange(n_co)[:, None, None])
    rpb2 = rpb.astype(F32).reshape(nh // 2, 2, n_ro, n_co)
    by_col = jnp.einsum("pgrj,jqk->pgrqk", rpb2, col_onehot.astype(np.float32),
                        precision=lax.Precision.HIGHEST) * np.float32(np.log2(np.e))
    by_col = jnp.where(col_valid, by_col, MASK_VALUE)
    by_col = jnp.concatenate([by_col, jnp.full((nh // 2, 2, 1, GRID_W, GRID_W), MASK_VALUE, F32)], axis=2)
    tables = jnp.concatenate([by_col, by_col], axis=-1)
    idx = np.full((3, NA_GROUP_ROWS, NA_KEY_ROWS), n_ro, np.int32)
    for ci, g in enumerate((0, 1, ngrp - 1)):
        kb = int(np.clip(NA_GROUP_ROWS * g - WIN_H // 2, 0, rows - NA_KEY_ROWS))
        for a in range(NA_GROUP_ROWS):
            r = NA_GROUP_ROWS * g + a
            start = int(np.clip(r - WIN_H // 2, 0, rows - WIN_H))
            for cr in range(NA_KEY_ROWS):
                if start <= kb + cr < start + WIN_H:
                    idx[ci, a, cr] = kb + cr - r + WIN_H - 1
    gq, gk = NA_GROUP_ROWS * GRID_W, NA_KEY_ROWS * GRID_W
    return pl.pallas_call(
        _bias_table_kernel,
        grid_spec=pltpu.PrefetchScalarGridSpec(
            num_scalar_prefetch=1,
            grid=(nh // 2, 3),
            in_specs=[pl.BlockSpec((None,) + tables.shape[1:], lambda p, c, ix: (p, 0, 0, 0, 0))],
            out_specs=pl.BlockSpec((None, None, 2 * gq, gk), lambda p, c, ix: (p, c, 0, 0)),
        ),
        out_shape=jax.ShapeDtypeStruct((nh // 2, 3, 2 * gq, gk), F32),
        compiler_params=_params("parallel", "parallel"),
        name="na_bias_table",
    )(jnp.asarray(idx.reshape(-1)), tables)


def _neighbourhood_attention(z_lat, z_ctx, bias, seq):
    nb, s, _ = z_lat.shape
    cl = z_ctx.shape[1]
    npair = bias.shape[0]
    d_att = npair * LANES
    cb = npair
    rows = s // GRID_W
    assert s % GRID_W == 0 and rows % (2 * NA_GROUP_ROWS) == 0 and rows >= NA_KEY_ROWS + NA_GROUP_ROWS
    s_shape = (2 * NA_GROUP_ROWS * GRID_W, NA_KEY_ROWS * GRID_W + cl)
    lat = lambda seg: pl.BlockSpec((None, s, LANES), lambda p, b: (b, 0, seg * cb + p))
    ctx = lambda seg: pl.BlockSpec((None, cl, LANES), lambda p, b: (b, 0, seg * cb + p))
    return pl.pallas_call(
        functools.partial(_na_kernel, rows=rows),
        grid=(npair, nb),
        in_specs=[lat(4), lat(1), lat(2), ctx(1), ctx(2),
                  pl.BlockSpec((None,) + bias.shape[1:], lambda p, b: (p, 0, 0, 0))],
        out_specs=pl.BlockSpec((None, s, LANES), lambda p, b: (b, 0, p)),
        out_shape=jax.ShapeDtypeStruct((nb, s, d_att), BF16),
        scratch_shapes=[pltpu.VMEM(s_shape, F32), pltpu.VMEM(s_shape, F32)],
        compiler_params=_params("parallel", "parallel"),
        name="neighbourhood_attention",
    )(z_lat, z_lat, z_lat, z_ctx, z_ctx, bias)


def _ctx_attn_kernel(q_ref, k_ref, v_ref, o_ref):
    in_a = lax.broadcasted_iota(jnp.int32, (1, LANES), 1) < HEAD_DIM
    n = q_ref.shape[0]
    sc = _dot_nt(_stack_heads(q_ref[...], in_a), k_ref[...])
    p = jnp.exp2(sc - jnp.max(sc, axis=-1, keepdims=True))
    den = jnp.sum(p, axis=-1, keepdims=True)
    o2 = jnp.dot(p.astype(BF16), v_ref[...], preferred_element_type=F32) / den
    o_ref[...] = jnp.where(in_a, o2[:n], o2[n:]).astype(o_ref.dtype)


def _context_attention(z_ctx, d_att):
    nb, cl, _ = z_ctx.shape
    cb = d_att // LANES
    ctx = lambda seg: pl.BlockSpec((None, cl, LANES), lambda p, b: (b, 0, seg * cb + p))
    return pl.pallas_call(
        _ctx_attn_kernel,
        grid=(cb, nb),
        in_specs=[ctx(4), ctx(1), ctx(2)],
        out_specs=pl.BlockSpec((None, cl, LANES), lambda p, b: (b, 0, p)),
        out_shape=jax.ShapeDtypeStruct((nb, cl, d_att), BF16),
        compiler_params=_params("parallel", "parallel"),
        name="context_attention",
    )(z_ctx, z_ctx, z_ctx)


def _merge_kernel(x_ref, g_ref, hs_ref, y_ref, na_ref, gr_ref, gn_ref, wr_ref, wn_ref, wo_ref, o_ref, t_scr):
    nb, ts, d = x_ref.shape
    c = hs_ref.shape[2]
    rows2d = lambda ref: ref[...].reshape(nb * ts, ref.shape[2])
    for t in range(ts):
        hs_t = hs_ref[t].astype(F32)
        for sl in range(c // LANES):
            t_scr.at[sl][pl.ds(t, nb, stride=XR_PITCH), :] = hs_t[:, sl * LANES:(sl + 1) * LANES]
    hs = jnp.concatenate([jnp.concatenate([t_scr[sl, b * XR_PITCH:b * XR_PITCH + ts, :] for sl in range(c // LANES)],
                                          axis=-1) for b in range(nb)], axis=0)
    y_rnn = (hs * _gelu_tanh(rows2d(y_ref).astype(F32))).astype(BF16)
    t_rnn = jnp.dot(y_rnn, wr_ref[...], preferred_element_type=F32)
    t_na = jnp.dot(rows2d(na_ref), wn_ref[...], preferred_element_type=F32)
    mix = _sigmoid(rows2d(gr_ref).astype(F32)) * t_rnn + _sigmoid(rows2d(gn_ref).astype(F32)) * t_na
    out = jnp.dot(mix.astype(BF16), wo_ref[...], preferred_element_type=F32)
    o_ref[...] = x_ref[...] + g_ref[...] * out.reshape(nb, ts, d)


def _merge(x3d, gate, hs_tm, t_off, z3, na3, w_rnn_o, w_na_o, w_out):
    nb, length, d = x3d.shape
    c = hs_tm.shape[2]
    da = na3.shape[2]
    ts = _pick_tile(length, IN_PROJ_STEPS, 16)
    assert ts + 8 <= XR_PITCH and t_off % ts == 0
    y_blk = (c + 2 * da) // c
    gr_blk = (2 * c + 3 * da) // d
    gn_blk = gr_blk + 1
    row = lambda width, blk=0: pl.BlockSpec((nb, ts, width), lambda i: (0, i, blk))
    whole = lambda a: pl.BlockSpec(a.shape, lambda i: (0,) * a.ndim, pipeline_mode=pl.Buffered(1))
    return pl.pallas_call(
        _merge_kernel,
        grid=(length // ts,),
        in_specs=[row(d), pl.BlockSpec(gate.shape, lambda i: (0, 0, 0)),
                  pl.BlockSpec((ts, nb, c), lambda i: (i + t_off // ts, 0, 0)),
                  row(c, y_blk), row(da), row(d, gr_blk), row(d, gn_blk),
                  whole(w_rnn_o), whole(w_na_o), whole(w_out)],
        out_specs=row(d),
        out_shape=jax.ShapeDtypeStruct((nb, length, d), F32),
        scratch_shapes=[pltpu.VMEM((c // LANES, nb * XR_PITCH, LANES), F32)],
        compiler_params=_params("parallel"),
        name="merge_out_proj",
    )(x3d, gate, hs_tm, z3, na3, z3, z3, w_rnn_o, w_na_o, w_out)


def _ffn_kernel(x_ref, sh_ref, sc_ref, g_ref, w1_ref, w3_ref, w2_ref, o_ref, *, tf):
    x = x_ref[...]
    h = _modulated_norm(x, sh_ref[...], sc_ref[...]).astype(BF16)
    acc = None
    for j in range(w1_ref.shape[1] // tf):
        cols = slice(j * tf, (j + 1) * tf)
        a = jnp.dot(h, w1_ref[:, cols], preferred_element_type=F32)
        b = jnp.dot(h, w3_ref[:, cols], preferred_element_type=F32)
        part = jnp.dot((_silu(a) * b).astype(BF16), w2_ref[cols, :], preferred_element_type=F32)
        acc = part if acc is None else acc + part
    o_ref[...] = x + g_ref[...] * acc


def _dense_ffn(x2d, shift, scale, gate, w1, w3, w2, rows_per_mod):
    m, d = x2d.shape
    dff = w1.shape[1]
    tm = _pick_tile(rows_per_mod, 512, 8)
    tf = _pick_tile(dff, 1536, LANES)
    tiles_per_mod = rows_per_mod // tm
    mod = pl.BlockSpec((None, 1, d), lambda i: (i // tiles_per_mod, 0, 0))
    whole = lambda w: pl.BlockSpec(w.shape, lambda i: (0, 0), pipeline_mode=pl.Buffered(1))
    return pl.pallas_call(
        functools.partial(_ffn_kernel, tf=tf),
        grid=(m // tm,),
        in_specs=[pl.BlockSpec((tm, d), lambda i: (i, 0)), mod, mod, mod, whole(w1), whole(w3), whole(w2)],
        out_specs=pl.BlockSpec((tm, d), lambda i: (i, 0)),
        out_shape=jax.ShapeDtypeStruct((m, d), F32),
        compiler_params=_params("parallel"),
        name="dense_swiglu",
    )(x2d, shift, scale, gate, w1, w3, w2)


SUBLANES = 8
GATHER_DMA_PRIORITY = 1


def _to_token_tiles(ref, x):
    n = x.shape[0]
    for s in range(SUBLANES):
        ref[pl.ds(s, n, stride=SUBLANES), :] = x[:, s * LANES:(s + 1) * LANES]


def _from_token_tiles(ref, n, s):
    return ref[pl.ds(s, n, stride=SUBLANES), :]


def _route_kernel(x_ref, sh_ref, sc_ref, wr_ref, wrl_ref, r_ref, *, n_experts):
    h = _modulated_norm(x_ref[...], sh_ref[...], sc_ref[...])
    h_hi = h.astype(BF16)
    h_lo = (h - h_hi.astype(F32)).astype(BF16)
    logits = (jnp.dot(h_hi, wr_ref[...], preferred_element_type=F32)
              + jnp.dot(h_lo, wr_ref[...], preferred_element_type=F32)
              + jnp.dot(h_hi, wrl_ref[...], preferred_element_type=F32))
    lane = lax.broadcasted_iota(jnp.int32, logits.shape, 1).astype(F32)
    neg = -jnp.inf
    lg = jnp.where(lane < n_experts, logits, neg)
    m1 = jnp.max(lg, axis=-1, keepdims=True)
    i1 = jnp.min(jnp.where(lg == m1, lane, float(LANES)), axis=-1, keepdims=True)
    lg2 = jnp.where(lane == i1, neg, lg)
    m2 = jnp.max(lg2, axis=-1, keepdims=True)
    i2 = jnp.min(jnp.where(lg2 == m2, lane, float(LANES)), axis=-1, keepdims=True)
    e = jnp.exp(m2 - m1)
    w1 = 1.0 / (1.0 + e)
    w2 = e / (1.0 + e)
    r_ref[...] = jnp.where(lane == 0, i1, jnp.where(lane == 1, i2, jnp.where(lane == 2, w1,
                           jnp.where(lane == 3, w2, 0.0))))


def _route(x2d, shift, scale, router, rows_per_mod):
    m, d = x2d.shape
    n_experts = router.shape[1]
    wr32 = jnp.zeros((d, LANES), F32).at[:, :n_experts].set(router)
    wr = wr32.astype(BF16)
    wrl = (wr32 - wr.astype(F32)).astype(BF16)
    tm = _pick_tile(rows_per_mod, 512, 8)
    tiles_per_mod = rows_per_mod // tm
    mod = pl.BlockSpec((None, 1, d), lambda i: (i // tiles_per_mod, 0, 0))
    rspec = pl.BlockSpec((d, LANES), lambda i: (0, 0))
    return pl.pallas_call(
        functools.partial(_route_kernel, n_experts=n_experts),
        grid=(m // tm,),
        in_specs=[pl.BlockSpec((tm, d), lambda i: (i, 0)), mod, mod, rspec, rspec],
        out_specs=pl.BlockSpec((tm, LANES), lambda i: (i, 0)),
        out_shape=jax.ShapeDtypeStruct((m, LANES), F32),
        compiler_params=_params("parallel"),
        name="moe_route",
    )(x2d, shift, scale, wr, wrl)


def _tile_rows(tile):
    start = tile * SUBLANES
    return pl.ds(start if isinstance(start, int) else pl.multiple_of(start, SUBLANES), SUBLANES)


def _tile_copy(src_ref, src_tile, dst_ref, dst_tile, sem):
    return pltpu.make_async_copy(src_ref.at[_tile_rows(src_tile), :], dst_ref.at[_tile_rows(dst_tile), :], sem)


def _dispatch_kernel(pos_ref, pad_ref, x_ref, sh_ref, sc_ref, xs_hbm, hbuf, zbuf, sem, zsem):
    i = pl.program_id(0)
    n = pl.num_programs(0)
    tm = x_ref.shape[0]
    m = n * tm
    pads_per_step = pad_ref.shape[0] // n
    slot = i % 2

    def wait_slot(sl):
        rows = TOP_K * tm * SUBLANES
        pltpu.make_async_copy(xs_hbm.at[pl.ds(0, rows), :], xs_hbm.at[pl.ds(0, rows), :], sem.at[sl]).wait()

    @pl.when(i >= 2)
    def _():
        wait_slot(slot)

    _to_token_tiles(hbuf.at[slot], _modulated_norm(x_ref[...], sh_ref[...], sc_ref[...]))
    for k in range(TOP_K):
        base = k * m + i * tm

        def put(j, carry):
            _tile_copy(hbuf.at[slot], j, xs_hbm, pos_ref[base + j], sem.at[slot]).start(priority=GATHER_DMA_PRIORITY)
            return carry

        lax.fori_loop(0, tm, put, 0, unroll=8)

    @pl.when(i == 0)
    def _():
        zbuf[...] = jnp.zeros_like(zbuf)

    def clear(q, carry):
        _tile_copy(zbuf, 0, xs_hbm, pad_ref[i * pads_per_step + q], zsem).start(priority=GATHER_DMA_PRIORITY)
        return carry

    lax.fori_loop(0, pads_per_step, clear, 0, unroll=8)

    @pl.when(i == n - 1)
    def _():
        wait_slot(slot)

        @pl.when(n >= 2)
        def _():
            wait_slot(1 - slot)

        rows = pad_ref.shape[0] * SUBLANES
        pltpu.make_async_copy(xs_hbm.at[pl.ds(0, rows), :], xs_hbm.at[pl.ds(0, rows), :], zsem).wait()


def _dispatch(x2d, shift, scale, pos, pad_rows, n_rows, rows_per_mod):
    m, d = x2d.shape
    assert d == SUBLANES * LANES
    tm = _pick_tile(rows_per_mod, 512, 8)
    tiles_per_mod = rows_per_mod // tm
    assert pad_rows.shape[0] % (m // tm) == 0
    mod = pl.BlockSpec((None, 1, d), lambda i, ps, pd: (i // tiles_per_mod, 0, 0))
    return pl.pallas_call(
        _dispatch_kernel,
        grid_spec=pltpu.PrefetchScalarGridSpec(
            num_scalar_prefetch=2,
            grid=(m // tm,),
            in_specs=[pl.BlockSpec((tm, d), lambda i, ps, pd: (i, 0)), mod, mod],
            out_specs=pl.BlockSpec(memory_space=pl.ANY),
            scratch_shapes=[pltpu.VMEM((2, tm * SUBLANES, LANES), F32), pltpu.VMEM((SUBLANES, LANES), F32),
                            pltpu.SemaphoreType.DMA((2,)), pltpu.SemaphoreType.DMA(())],
        ),
        out_shape=jax.ShapeDtypeStruct((n_rows * SUBLANES, LANES), F32),
        compiler_params=_params("arbitrary"),
        name="moe_dispatch",
    )(pos, pad_rows, x2d, shift, scale)


EXPERT_TILE_ROWS = 1024


def _expert_ffn_kernel(te_ref, nv_ref, x_ref, w1_ref, w3_ref, w2_ref, o_ref, h_scr, acc_scr):
    t = pl.program_id(0)
    f = pl.program_id(1)
    nv = nv_ref[0]
    tm = h_scr.shape[0]

    @pl.when(t < nv)
    def _():
        @pl.when(f == 0)
        def _():
            for s in range(SUBLANES):
                h_scr[:, s * LANES:(s + 1) * LANES] = _from_token_tiles(x_ref, tm, s).astype(BF16)
            acc_scr[...] = jnp.zeros_like(acc_scr)

        h = h_scr[...]
        a = jnp.dot(h, w1_ref[...].astype(BF16), preferred_element_type=F32)
        b = jnp.dot(h, w3_ref[...].astype(BF16), preferred_element_type=F32)
        acc_scr[...] += jnp.dot((_silu(a) * b).astype(BF16), w2_ref[...].astype(BF16),
                                preferred_element_type=F32)

        @pl.when(f == pl.num_programs(1) - 1)
        def _():
            _to_token_tiles(o_ref, acc_scr[...])

    @pl.when(jnp.logical_and(t >= nv, f == pl.num_programs(1) - 1))
    def _():
        o_ref[...] = jnp.zeros_like(o_ref)


def _expert_ffn(xs, tile_expert, n_valid, w1, w3, w2):
    tm = EXPERT_TILE_ROWS
    n = xs.shape[0] // SUBLANES
    d = w1.shape[1]
    dfe = w1.shape[2]
    tf = _pick_tile(dfe, 512, LANES)
    nf = dfe // tf

    def fsel(t, f, nv):
        return jnp.where(t < nv[0], f, nf - 1)

    return pl.pallas_call(
        _expert_ffn_kernel,
        grid_spec=pltpu.PrefetchScalarGridSpec(
            num_scalar_prefetch=2,
            grid=(n // tm, nf),
            in_specs=[pl.BlockSpec((tm * SUBLANES, LANES), lambda t, f, te, nv: (t, 0)),
                      pl.BlockSpec((None, d, tf), lambda t, f, te, nv: (te[t], 0, fsel(t, f, nv))),
                      pl.BlockSpec((None, d, tf), lambda t, f, te, nv: (te[t], 0, fsel(t, f, nv))),
                      pl.BlockSpec((None, tf, d), lambda t, f, te, nv: (te[t], fsel(t, f, nv), 0))],
            out_specs=pl.BlockSpec((tm * SUBLANES, LANES), lambda t, f, te, nv: (t, 0)),
            scratch_shapes=[pltpu.VMEM((tm, d), BF16), pltpu.VMEM((tm, d), F32)],
        ),
        out_shape=jax.ShapeDtypeStruct((n * SUBLANES, LANES), F32),
        compiler_params=_params("arbitrary", "arbitrary"),
        name="expert_swiglu",
    )(tile_expert, n_valid, xs, w1, w3, w2)


def _combine_kernel(pos_ref, x_ref, g_ref, r_ref, ys_hbm, o_ref, ybuf, sem):
    i = pl.program_id(0)
    n = pl.num_programs(0)
    tm = x_ref.shape[0]
    m = n * tm

    def issue_rows(tile, slot):
        for k in range(TOP_K):
            base = k * m + tile * tm

            def body(j, carry):
                src_row = pl.multiple_of(pos_ref[base + j] * SUBLANES, SUBLANES)
                dst_row = pl.multiple_of(j * SUBLANES, SUBLANES)
                pltpu.make_async_copy(ys_hbm.at[pl.ds(src_row, SUBLANES), :],
                                      ybuf.at[slot, k, pl.ds(dst_row, SUBLANES), :],
                                      sem.at[slot]).start(priority=GATHER_DMA_PRIORITY)
                return carry

            lax.fori_loop(0, tm, body, 0, unroll=8)

    slot = i % 2

    @pl.when(i == 0)
    def _():
        issue_rows(0, 0)

    pltpu.make_async_copy(ybuf.at[1 - slot], ybuf.at[slot], sem.at[slot]).wait()

    @pl.when(i + 1 < n)
    def _():
        issue_rows(i + 1, 1 - slot)

    r = r_ref[...]
    w1, w2 = r[:, 2:3], r[:, 3:4]
    for s in range(SUBLANES):
        lanes = slice(s * LANES, (s + 1) * LANES)
        mix = (w1 * _from_token_tiles(ybuf.at[slot, 0], tm, s) + w2 * _from_token_tiles(ybuf.at[slot, 1], tm, s))
        o_ref[:, lanes] = x_ref[:, lanes] + g_ref[:, lanes] * mix


def _combine(x2d, gate, route, ys, pos, rows_per_mod):
    m, d = x2d.shape
    tm = _pick_tile(rows_per_mod, 512, 8)
    tiles_per_mod = rows_per_mod // tm
    return pl.pallas_call(
        _combine_kernel,
        grid_spec=pltpu.PrefetchScalarGridSpec(
            num_scalar_prefetch=1,
            grid=(m // tm,),
            in_specs=[pl.BlockSpec((tm, d), lambda i, ps: (i, 0)),
                      pl.BlockSpec((None, 1, d), lambda i, ps: (i // tiles_per_mod, 0, 0)),
                      pl.BlockSpec((tm, LANES), lambda i, ps: (i, 0)),
                      pl.BlockSpec(memory_space=pl.ANY)],
            out_specs=pl.BlockSpec((tm, d), lambda i, ps: (i, 0)),
            scratch_shapes=[pltpu.VMEM((2, TOP_K, tm * SUBLANES, LANES), F32), pltpu.SemaphoreType.DMA((2,))],
        ),
        out_shape=jax.ShapeDtypeStruct((m, d), F32),
        compiler_params=_params("arbitrary"),
        name="moe_combine",
    )(pos, x2d, gate, route, ys)


def _moe_ffn(x2d, shift, scale, gate, router, w1, w3, w2, rows_per_mod):
    m, d = x2d.shape
    n_experts = router.shape[1]
    route = _route(x2d, shift, scale, router, rows_per_mod)
    tm = EXPERT_TILE_ROWS
    expert = route[:, :TOP_K].astype(jnp.int32).T.reshape(-1)
    onehot = (expert[:, None] == jnp.arange(n_experts)[None, :]).astype(jnp.int32)
    csum = jnp.cumsum(onehot, axis=0)
    rank = jnp.sum(onehot * (csum - 1), axis=1)
    counts = csum[-1]
    tiles = (counts + tm - 1) // tm
    tile_end = jnp.cumsum(tiles)
    first_row = (tile_end - tiles) * tm
    pos = (jnp.sum(onehot * first_row[None, :], axis=1) + rank).astype(jnp.int32)
    n_tiles = (TOP_K * m) // tm + n_experts
    n_rows = n_tiles * tm
    gap_len = jnp.concatenate([tiles * tm - counts, (n_rows - tile_end[-1:] * tm)])
    gap_row = jnp.concatenate([first_row + counts, tile_end[-1:] * tm])
    gap_end = jnp.cumsum(gap_len)
    q = jnp.arange(n_rows - TOP_K * m)
    in_gap = jnp.logical_and(q[:, None] >= (gap_end - gap_len)[None, :], q[:, None] < gap_end[None, :])
    pad_rows = jnp.sum(jnp.where(in_gap, (gap_row - (gap_end - gap_len))[None, :] + q[:, None], 0),
                       axis=1).astype(jnp.int32)
    tile_expert = jnp.minimum(jnp.sum((jnp.arange(n_tiles)[:, None] >= tile_end[None, :]).astype(jnp.int32), axis=1),
                              n_experts - 1).astype(jnp.int32)
    n_valid = tile_end[-1:].astype(jnp.int32)
    xs = _dispatch(x2d, shift, scale, pos, pad_rows, n_rows, rows_per_mod)
    ys = _expert_ffn(xs, tile_expert, n_valid, w1, w3, w2)
    return _combine(x2d, gate, route, ys, pos, rows_per_mod)


def kernel(x, c, ctx, c_ctx, w_mod, b_mod, w_in, conv_w, conv_b, rg_lambda, rg_w, rg_b, q_gain, k_gain, rpb,
           w_rnn_o, w_na_o, w_out, ffn_w1, ffn_w3, ffn_w2, router, moe_w1, moe_w3, moe_w2):
    nb, seq, d = x.shape
    cl = ctx.shape[1]
    depth = w_mod.shape[0]
    c_rnn = conv_w.shape[2]
    d_att = rpb.shape[1] * HEAD_DIM
    ctx_cols = c_rnn + 2 * d_att
    rows = seq // GRID_W

    n_cond = -(-(nb + 1) // 8) * 8
    cond = jnp.zeros((n_cond, d), F32).at[:nb].set(c).at[nb].set(c_ctx)
    mods = _adaln(cond, w_mod, b_mod)

    x2 = x.reshape(nb * seq, d)
    xc2 = ctx.reshape(nb * cl, d)
    for l in range(depth):
        ctx_out = l < depth - 1
        lat = [mods[l, :nb, k * d:(k + 1) * d].reshape(nb, 1, d) for k in range(N_MOD)]
        cmod = [mods[l, nb:nb + 1, k * d:(k + 1) * d].reshape(1, 1, d) for k in range(N_MOD)]
        w_in_l = w_in[l].astype(BF16)
        tile_heads = lambda g: jnp.tile(g.astype(F32), c_rnn // HEAD_DIM).reshape(1, c_rnn)
        head_gains = jnp.stack([tile_heads(k_gain[l]), tile_heads(q_gain[l]) * Q_SCALE])

        z_lat3, xr_lat = _norm_matmul(x2.reshape(nb, seq, d), lat[0], lat[1], w_in_l, head_gains, c_rnn, (1, 4))
        if ctx_out:
            z_ctx3, xr_ctx = _norm_matmul(xc2.reshape(nb, cl, d), cmod[0], cmod[1], w_in_l, head_gains, c_rnn, (1, 4))
        else:
            z_ctx3, xr_ctx = _norm_matmul(xc2.reshape(nb, cl, d), cmod[0], cmod[1], w_in_l[:, :ctx_cols],
                                          head_gains, c_rnn, (1,))

        wg_f, bg_f = _gate_weights(rg_w[l, 0], rg_b[l, 0])
        wg_r, bg_r = _gate_weights(rg_w[l, 1], rg_b[l, 1])
        hs_r = _rnn_scan(xr_ctx, xr_lat, conv_w[l], conv_b[l], wg_r, bg_r, rg_lambda[l, 1], True, None)
        hs = _rnn_scan(xr_ctx, xr_lat, conv_w[l], conv_b[l], wg_f, bg_f, rg_lambda[l, 0], False, hs_r)

        bias = _na_bias_table(rpb[l], rows)
        na_lat = _neighbourhood_attention(z_lat3, z_ctx3, bias, seq)

        wr, wn, wo = w_rnn_o[l].astype(BF16), w_na_o[l].astype(BF16), w_out[l].astype(BF16)
        x2 = _merge(x2.reshape(nb, seq, d), lat[2], hs, cl, z_lat3, na_lat, wr, wn, wo).reshape(nb * seq, d)
        if ctx_out:
            na_ctx = _context_attention(z_ctx3, d_att)
            xc2 = _merge(xc2.reshape(nb, cl, d), cmod[2], hs, 0, z_ctx3, na_ctx, wr, wn, wo).reshape(nb * cl, d)

        j = l // 2
        if l % 2 == 0:
            w1, w3, w2 = ffn_w1[j].astype(BF16), ffn_w3[j].astype(BF16), ffn_w2[j].astype(BF16)
            x2 = _dense_ffn(x2, lat[3], lat[4], lat[5], w1, w3, w2, seq)
            if ctx_out:
                xc2 = _dense_ffn(xc2, cmod[3], cmod[4], cmod[5], w1, w3, w2, nb * cl)
        else:
            w1, w3, w2 = moe_w1[j], moe_w3[j], moe_w2[j]
            x2 = _moe_ffn(x2, lat[3], lat[4], lat[5], router[j], w1, w3, w2, seq)
            if ctx_out:
                xc2 = _moe_ffn(xc2, cmod[3], cmod[4], cmod[5], router[j], w1, w3, w2, nb * cl)
    return x2.reshape(nb, seq, d)
```

```python
import functools

import numpy as np
import jax
import jax.numpy as jnp
from jax import lax
from jax.experimental import pallas as pl
from jax.experimental.pallas import tpu as pltpu

F32 = jnp.float32
BF16 = jnp.bfloat16

EPS = 1e-6
N_MOD = 6
GRID_W = 64
WIN_H = 8
WIN_W = 16
HEAD_DIM = 64
RNN_BW = 64
CONV_TAPS = 4
CONV_LEFT = 2
LRU_C = 8.0
TOP_K = 2
MASK_VALUE = -1e30
SQRT_FLOOR = 1e-30

LANES = 128
MXU_DIM = 256
VMEM_LIMIT_BYTES = 56 * 1024 * 1024

NA_GROUP_ROWS = 4
NA_KEY_ROWS = 12


def _params(*sem):
    return pltpu.CompilerParams(dimension_semantics=sem, vmem_limit_bytes=VMEM_LIMIT_BYTES)


def _sigmoid(x):
    return 0.5 * (jnp.tanh(0.5 * x) + 1.0)


def _silu(x):
    return x * _sigmoid(x)


def _gelu_tanh(x):
    return 0.5 * x * (1.0 + jnp.tanh(np.sqrt(2.0 / np.pi) * (x + 0.044715 * (x * x * x))))


def _modulated_norm(x, shift, scale):
    ms = jnp.mean(x * x, axis=-1, keepdims=True)
    return x * lax.rsqrt(ms + EPS) * (1.0 + scale) + shift


def _pick_tile(n, cap, mult):
    best = None
    for t in range(mult, min(n, cap) + 1, mult):
        if n % t == 0:
            best = t
    assert best is not None, (n, cap, mult)
    return best


def _adaln_kernel(c_ref, w_ref, b_ref, o_ref):
    s = _silu(c_ref[...]).astype(BF16)
    o_ref[...] = jnp.dot(s, w_ref[...].astype(BF16), preferred_element_type=F32) + b_ref[...]


def _adaln(cond, w_mod, b_mod):
    depth, d, n = w_mod.shape
    r = cond.shape[0]
    tn = _pick_tile(n, 1536, LANES)
    return pl.pallas_call(
        _adaln_kernel,
        grid=(depth, n // tn),
        in_specs=[
            pl.BlockSpec((r, d), lambda l, j: (0, 0)),
            pl.BlockSpec((None, d, tn), lambda l, j: (l, 0, j)),
            pl.BlockSpec((None, 1, tn), lambda l, j: (l, 0, j)),
        ],
        out_specs=pl.BlockSpec((None, r, tn), lambda l, j: (l, 0, j)),
        out_shape=jax.ShapeDtypeStruct((depth, r, n), F32),
        compiler_params=_params("parallel", "parallel"),
        name="adaln",
    )(cond, w_mod, b_mod.reshape(depth, 1, n))


def _head_norm(x, gain, in_a):
    sq = x * x
    sa = jnp.sum(jnp.where(in_a, sq, 0.0), axis=-1, keepdims=True)
    sb = jnp.sum(jnp.where(in_a, 0.0, sq), axis=-1, keepdims=True)
    ms = jnp.where(in_a, sa, sb) * (1.0 / HEAD_DIM)
    return x * lax.rsqrt(ms + EPS) * gain


IN_PROJ_STEPS = 32
XR_PITCH = 40


def _norm_matmul_kernel(x_ref, sh_ref, sc_ref, w_ref, hg_ref, o_ref, xr_ref, t_scr, *, seg, normed):
    nb, ts, d = x_ref.shape
    h = _modulated_norm(x_ref[...], sh_ref[...], sc_ref[...]).reshape(nb * ts, d).astype(BF16)
    in_a = lax.broadcasted_iota(jnp.int32, (1, LANES), 1) < HEAD_DIM
    for j in range(w_ref.shape[1] // seg):
        acc = jnp.dot(h, w_ref[:, j * seg:(j + 1) * seg], preferred_element_type=F32)
        if j == 0:
            for sl in range(seg // LANES):
                for b in range(nb):
                    t_scr[sl, b * XR_PITCH:b * XR_PITCH + ts, :] = acc[b * ts:(b + 1) * ts, sl * LANES:(sl + 1) * LANES]
            for t in range(ts):
                xr_ref[t] = jnp.concatenate([t_scr.at[sl][pl.ds(t, nb, stride=XR_PITCH), :]
                                             for sl in range(seg // LANES)], axis=-1).astype(xr_ref.dtype)
        if j in normed:
            gain = hg_ref[normed.index(j)]
            for sl in range(seg // LANES):
                lanes = slice(sl * LANES, (sl + 1) * LANES)
                o_ref[:, :, j * seg + sl * LANES:j * seg + (sl + 1) * LANES] = _head_norm(
                    acc[:, lanes], gain[:, lanes], in_a).reshape(nb, ts, LANES).astype(o_ref.dtype)
        else:
            o_ref[:, :, j * seg:(j + 1) * seg] = acc.reshape(nb, ts, seg).astype(o_ref.dtype)


def _norm_matmul(x3d, shift, scale, w, head_gains, seg, normed):
    nb, length, d = x3d.shape
    n = w.shape[1]
    ts = _pick_tile(length, IN_PROJ_STEPS, 16)
    assert ts + 8 <= XR_PITCH
    whole = lambda a: pl.BlockSpec(a.shape, lambda i: (0,) * a.ndim)
    return pl.pallas_call(
        functools.partial(_norm_matmul_kernel, seg=seg, normed=normed),
        grid=(length // ts,),
        in_specs=[
            pl.BlockSpec((nb, ts, d), lambda i: (0, i, 0)),
            whole(shift),
            whole(scale),
            pl.BlockSpec((d, n), lambda i: (0, 0), pipeline_mode=pl.Buffered(1)),
            whole(head_gains),
        ],
        out_specs=[pl.BlockSpec((nb, ts, n), lambda i: (0, i, 0)),
                   pl.BlockSpec((ts, nb, seg), lambda i: (i, 0, 0))],
        out_shape=[jax.ShapeDtypeStruct((nb, length, n), BF16), jax.ShapeDtypeStruct((length, nb, seg), BF16)],
        scratch_shapes=[pltpu.VMEM((seg // LANES, nb * XR_PITCH, LANES), F32)],
        compiler_params=_params("parallel"),
        name="norm_in_proj",
    )(x3d, shift, scale, w, head_gains)


def _scan_tile_index(i, n_ctx_tiles, n_tiles, reverse):
    if not reverse:
        return i
    return jnp.where(i < n_ctx_tiles, n_ctx_tiles - 1 - i, n_tiles - 1 - (i - n_ctx_tiles))


def _gates_and_scan(xc, wg_ref, bg_ref, lam_ref, prev_ref, out_ref, a_scr, u_scr, h_scr, reverse):
    tl, nb, c = out_ref.shape

    @pl.when(pl.program_id(0) == 0)
    def _():
        h_scr[...] = jnp.zeros_like(h_scr)

    lam = lam_ref[...]
    log_sig = jnp.minimum(lam, 0.0) - jnp.log(1.0 + jnp.exp(-jnp.abs(lam)))
    k2 = (0.5 * LRU_C * np.log2(np.e)) * log_sig
    for j in range(c // MXU_DIM):
        sl = slice(MXU_DIM * j, MXU_DIM * (j + 1))
        xj = xc[:, sl]
        t = jnp.tanh(jnp.dot(xj.astype(BF16), wg_ref[j], preferred_element_type=F32) + bg_ref[j])
        a = jnp.exp2(k2[:, sl] * t[:, :MXU_DIM] + k2[:, sl])
        v = 1.0 - a * a
        gated_x = (t[:, MXU_DIM:] + 1.0) * (0.5 * xj)
        a_scr[:, sl] = a
        u_scr[:, sl] = (v * lax.rsqrt(jnp.maximum(v, SQRT_FLOOR))) * gated_x

    def step(s, h):
        tt = (tl - 1 - s) if reverse else s
        row = pl.multiple_of(tt * nb, nb)
        h = a_scr[pl.ds(row, nb), :] * h + u_scr[pl.ds(row, nb), :]
        val = h if prev_ref is None else h + prev_ref[tt].astype(F32)
        out_ref[tt] = val.astype(out_ref.dtype)
        return h

    h_scr[...] = lax.fori_loop(0, tl, step, h_scr[...], unroll=4)


def _rnn_conv_kernel(cp_ref, c_ref, cn_ref, lp_ref, l_ref, ln_ref, cw_ref, cb_ref, wg_ref, bg_ref, lam_ref,
                     out_ref, xc_ref, a_scr, u_scr, h_scr, *, n_ctx_tiles, n_tiles, reverse):
    tl, nb, c = c_ref.shape
    ti = _scan_tile_index(pl.program_id(0), n_ctx_tiles, n_tiles, reverse)
    in_ctx = ti < n_ctx_tiles
    pick = lambda ctx_ref, lat_ref: jnp.where(in_ctx, ctx_ref[...], lat_ref[...]).astype(F32)
    at_start = jnp.logical_or(ti == 0, ti == n_ctx_tiles)
    at_end = jnp.logical_or(ti == n_ctx_tiles - 1, ti == n_tiles - 1)
    xp = jnp.where(at_start, 0.0, pick(cp_ref, lp_ref))
    xn = jnp.where(at_end, 0.0, pick(cn_ref, ln_ref))
    xe = jnp.concatenate([xp, pick(c_ref, l_ref), xn], axis=0)
    xc = cb_ref[...] + xe[0:tl] * cw_ref[0:1, :]
    for k in range(1, CONV_TAPS):
        xc = xc + xe[k:k + tl] * cw_ref[k:k + 1, :]
    xc_ref[...] = xc.astype(xc_ref.dtype)
    _gates_and_scan(xc.reshape(tl * nb, c), wg_ref, bg_ref, lam_ref, None, out_ref, a_scr, u_scr, h_scr, reverse)


def _rnn_reuse_kernel(xc_ref, wg_ref, bg_ref, lam_ref, prev_ref, out_ref, a_scr, u_scr, h_scr, *, reverse):
    tl, nb, c = xc_ref.shape
    _gates_and_scan(xc_ref[...].astype(F32).reshape(tl * nb, c), wg_ref, bg_ref, lam_ref, prev_ref, out_ref,
                    a_scr, u_scr, h_scr, reverse)


def _scan_scratch(tl, nb, c):
    return [pltpu.VMEM((tl * nb, c), F32), pltpu.VMEM((tl * nb, c), F32), pltpu.VMEM((nb, c), F32)]


def _rnn_scan_conv(xr_ctx, xr_lat, conv_w, conv_b, wg, bg, lam, reverse):
    n_ctx, nb, c = xr_ctx.shape
    n_lat = xr_lat.shape[0]
    t = n_ctx + n_lat
    tl = _pick_tile(n_ctx, 64, 2)
    assert n_lat % tl == 0 and n_ctx % tl == 0
    n_tiles, n_ctx_tiles = t // tl, n_ctx // tl
    tile = functools.partial(_scan_tile_index, n_ctx_tiles=n_ctx_tiles, n_tiles=n_tiles, reverse=reverse)
    half = tl // CONV_LEFT
    full = lambda shape: pl.BlockSpec(shape, lambda i: (0,) * len(shape))

    def seq_specs(first_tile, length):
        last = length // tl - 1
        local = lambda i: jnp.clip(tile(i) - first_tile, 0, last)
        return [pl.BlockSpec((CONV_LEFT, nb, c), lambda i: (jnp.maximum(local(i) * half - 1, 0), 0, 0)),
                pl.BlockSpec((tl, nb, c), lambda i: (local(i), 0, 0)),
                pl.BlockSpec((1, nb, c), lambda i: (jnp.minimum((local(i) + 1) * tl, length - 1), 0, 0))]

    by_tile = pl.BlockSpec((tl, nb, c), lambda i: (tile(i), 0, 0))
    return pl.pallas_call(
        functools.partial(_rnn_conv_kernel, n_ctx_tiles=n_ctx_tiles, n_tiles=n_tiles, reverse=reverse),
        grid=(n_tiles,),
        in_specs=seq_specs(0, n_ctx) + seq_specs(n_ctx_tiles, n_lat) + [
            full((CONV_TAPS, c)), full((1, c)), full(wg.shape), full(bg.shape), full((1, c))],
        out_specs=[by_tile, by_tile],
        out_shape=[jax.ShapeDtypeStruct((t, nb, c), BF16), jax.ShapeDtypeStruct((t, nb, c), BF16)],
        scratch_shapes=_scan_scratch(tl, nb, c),
        compiler_params=_params("arbitrary"),
        name="rglru_scan_rev" if reverse else "rglru_scan_fwd",
    )(xr_ctx, xr_ctx, xr_ctx, xr_lat, xr_lat, xr_lat, conv_w, conv_b.reshape(1, c), wg, bg, lam.reshape(1, c))


def _rnn_scan_reuse(xc, n_ctx, wg, bg, lam, reverse, prev):
    t, nb, c = xc.shape
    tl = _pick_tile(n_ctx, 64, 2)
    n_tiles, n_ctx_tiles = t // tl, n_ctx // tl
    tile = functools.partial(_scan_tile_index, n_ctx_tiles=n_ctx_tiles, n_tiles=n_tiles, reverse=reverse)
    full = lambda shape: pl.BlockSpec(shape, lambda i: (0,) * len(shape))
    by_tile = pl.BlockSpec((tl, nb, c), lambda i: (tile(i), 0, 0))
    return pl.pallas_call(
        functools.partial(_rnn_reuse_kernel, reverse=reverse),
        grid=(n_tiles,),
        in_specs=[by_tile, full(wg.shape), full(bg.shape), full((1, c)), by_tile],
        out_specs=by_tile,
        out_shape=jax.ShapeDtypeStruct((t, nb, c), BF16),
        scratch_shapes=_scan_scratch(tl, nb, c),
        compiler_params=_params("arbitrary"),
        name="rglru_scan_rev" if reverse else "rglru_scan_fwd",
    )(xc, wg, bg, lam.reshape(1, c), prev)


def _gate_weights(rg_w_d, rg_b_d):
    _, nblk, bw, _ = rg_w_d.shape
    per = MXU_DIM // bw
    ngrp = nblk // per
    w = rg_w_d.reshape(2, ngrp, per, bw, bw)
    eye = jnp.eye(per, dtype=rg_w_d.dtype)
    dense = w[:, :, :, :, None, :] * eye[None, None, :, None, :, None]
    dense = dense.reshape(2, ngrp, MXU_DIM, MXU_DIM)
    wg = jnp.concatenate([dense[0], dense[1]], axis=-1).astype(BF16)
    b = rg_b_d.reshape(2, ngrp, 1, MXU_DIM)
    bg = jnp.concatenate([b[0], b[1]], axis=-1)
    return wg * 0.5, bg * 0.5


Q_SCALE = (HEAD_DIM ** -0.5) * float(np.log2(np.e))


def _dot_nt(a, b):
    return lax.dot_general(a, b, (((1,), (1,)), ((), ())), preferred_element_type=F32)


def _stack_heads(q, in_a):
    zero = jnp.zeros_like(q)
    return jnp.concatenate([jnp.where(in_a, q, zero), jnp.where(in_a, zero, q)], axis=0)


def _na_kernel(q_ref, k_ref, v_ref, kc_ref, vc_ref, bias_ref, o_ref, sa_scr, sb_scr, *, rows):
    cl = kc_ref.shape[0]
    gq = NA_GROUP_ROWS * GRID_W
    gk = NA_KEY_ROWS * GRID_W
    ngrp = rows // NA_GROUP_ROWS
    in_a = lax.broadcasted_iota(jnp.int32, (1, LANES), 1) < HEAD_DIM

    def key_base(g):
        kb = jnp.clip(NA_GROUP_ROWS * g - WIN_H // 2, 0, rows - NA_KEY_ROWS)
        return pl.multiple_of(kb * GRID_W, GRID_W)

    def scores(g, s_scr):
        cls = jnp.where(g == 0, 0, jnp.where(g == ngrp - 1, 2, 1))
        q2 = _stack_heads(q_ref[pl.ds(pl.multiple_of(g * gq, gq), gq), :], in_a)
        s_scr[:, 0:gk] = _dot_nt(q2, k_ref[pl.ds(key_base(g), gk), :]) + bias_ref[cls]
        s_scr[:, gk:gk + cl] = _dot_nt(q2, kc_ref[...])

    def attend(g, s_scr):
        sc = s_scr[...]
        p = jnp.exp2(sc - jnp.max(sc, axis=-1, keepdims=True))
        den = jnp.sum(p, axis=-1, keepdims=True)
        pb = p.astype(BF16)
        o2 = (jnp.dot(pb[:, 0:gk], v_ref[pl.ds(key_base(g), gk), :], preferred_element_type=F32)
              + jnp.dot(pb[:, gk:gk + cl], vc_ref[...], preferred_element_type=F32)) / den
        o_ref[pl.ds(pl.multiple_of(g * gq, gq), gq), :] = jnp.where(in_a, o2[:gq], o2[gq:]).astype(o_ref.dtype)

    scores(0, sa_scr)

    def pair(it, carry):
        g = 2 * it
        scores(g + 1, sb_scr)
        attend(g, sa_scr)
        scores(g + 2, sa_scr)
        attend(g + 1, sb_scr)
        return carry

    lax.fori_loop(0, ngrp // 2 - 1, pair, 0)
    scores(ngrp - 1, sb_scr)
    attend(ngrp - 2, sa_scr)
    attend(ngrp - 1, sb_scr)


def _bias_table_kernel(idx_ref, t_ref, o_ref):
    c = pl.program_id(1)
    left = lax.broadcasted_iota(jnp.int32, (1, LANES), 1) < GRID_W
    per_class = NA_GROUP_ROWS * NA_KEY_ROWS
    for g in range(2):
        for a in range(NA_GROUP_ROWS):
            r0 = (g * NA_GROUP_ROWS + a) * GRID_W
            for j in range(NA_KEY_ROWS // 2):
                o_even = idx_ref[c * per_class + a * NA_KEY_ROWS + 2 * j]
                o_odd = idx_ref[c * per_class + a * NA_KEY_ROWS + 2 * j + 1]
                o_ref[r0:r0 + GRID_W, j * LANES:(j + 1) * LANES] = jnp.where(left, t_ref[g, o_even], t_ref[g, o_odd])


def _na_bias_table(rpb, rows):
    nh, n_ro, n_co = rpb.shape
    ngrp = rows // NA_GROUP_ROWS
    qc = np.arange(GRID_W)[:, None]
    kc = np.arange(GRID_W)[None, :]
    wstart = np.clip(qc - WIN_W // 2, 0, GRID_W - WIN_W)
    col_valid = (kc >= wstart) & (kc < wstart + WIN_W)
    col_onehot = ((kc - qc + WIN_W - 1)[None] == np.arange(n_co)[:, None, None])
    rpb2 = rpb.astype(F32).reshape(nh // 2, 2, n_ro, n_co)
    by_col = jnp.einsum("pgrj,jqk->pgrqk", rpb2, col_onehot.astype(np.float32),
                        precision=lax.Precision.HIGHEST) * np.float32(np.log2(np.e))
    by_col = jnp.where(col_valid, by_col, MASK_VALUE)
    by_col = jnp.concatenate([by_col, jnp.full((nh // 2, 2, 1, GRID_W, GRID_W), MASK_VALUE, F32)], axis=2)
    tables = jnp.concatenate([by_col, by_col], axis=-1)
    idx = np.full((3, NA_GROUP_ROWS, NA_KEY_ROWS), n_ro, np.int32)
    for ci, g in enumerate((0, 1, ngrp - 1)):
        kb = int(np.clip(NA_GROUP_ROWS * g - WIN_H // 2, 0, rows - NA_KEY_ROWS))
        for a in range(NA_GROUP_ROWS):
            r = NA_GROUP_ROWS * g + a
            start = int(np.clip(r - WIN_H // 2, 0, rows - WIN_H))
            for cr in range(NA_KEY_ROWS):
                if start <= kb + cr < start + WIN_H:
                    idx[ci, a, cr] = kb + cr - r + WIN_H - 1
    gq, gk = NA_GROUP_ROWS * GRID_W, NA_KEY_ROWS * GRID_W
    return pl.pallas_call(
        _bias_table_kernel,
        grid_spec=pltpu.PrefetchScalarGridSpec(
            num_scalar_prefetch=1,
            grid=(nh // 2, 3),
            in_specs=[pl.BlockSpec((None,) + tables.shape[1:], lambda p, c, ix: (p, 0, 0, 0, 0))],
            out_specs=pl.BlockSpec((None, None, 2 * gq, gk), lambda p, c, ix: (p, c, 0, 0)),
        ),
        out_shape=jax.ShapeDtypeStruct((nh // 2, 3, 2 * gq, gk), F32),
        compiler_params=_params("parallel", "parallel"),
        name="na_bias_table",
    )(jnp.asarray(idx.reshape(-1)), tables)


def _neighbourhood_attention(z_lat, z_ctx, bias, seq):
    nb, s, _ = z_lat.shape
    cl = z_ctx.shape[1]
    npair = bias.shape[0]
    d_att = npair * LANES
    cb = npair
    rows = s // GRID_W
    assert s % GRID_W == 0 and rows % (2 * NA_GROUP_ROWS) == 0 and rows >= NA_KEY_ROWS + NA_GROUP_ROWS
    s_shape = (2 * NA_GROUP_ROWS * GRID_W, NA_KEY_ROWS * GRID_W + cl)
    lat = lambda seg: pl.BlockSpec((None, s, LANES), lambda p, b: (b, 0, seg * cb + p))
    ctx = lambda seg: pl.BlockSpec((None, cl, LANES), lambda p, b: (b, 0, seg * cb + p))
    return pl.pallas_call(
        functools.partial(_na_kernel, rows=rows),
        grid=(npair, nb),
        in_specs=[lat(4), lat(1), lat(2), ctx(1), ctx(2),
                  pl.BlockSpec((None,) + bias.shape[1:], lambda p, b: (p, 0, 0, 0))],
        out_specs=pl.BlockSpec((None, s, LANES), lambda p, b: (b, 0, p)),
        out_shape=jax.ShapeDtypeStruct((nb, s, d_att), BF16),
        scratch_shapes=[pltpu.VMEM(s_shape, F32), pltpu.VMEM(s_shape, F32)],
        compiler_params=_params("parallel", "parallel"),
        name="neighbourhood_attention",
    )(z_lat, z_lat, z_lat, z_ctx, z_ctx, bias)


def _ctx_attn_kernel(q_ref, k_ref, v_ref, o_ref):
    in_a = lax.broadcasted_iota(jnp.int32, (1, LANES), 1) < HEAD_DIM
    n = q_ref.shape[0]
    sc = _dot_nt(_stack_heads(q_ref[...], in_a), k_ref[...])
    p = jnp.exp2(sc - jnp.max(sc, axis=-1, keepdims=True))
    den = jnp.sum(p, axis=-1, keepdims=True)
    o2 = jnp.dot(p.astype(BF16), v_ref[...], preferred_element_type=F32) / den
    o_ref[...] = jnp.where(in_a, o2[:n], o2[n:]).astype(o_ref.dtype)


def _context_attention(z_ctx, d_att):
    nb, cl, _ = z_ctx.shape
    cb = d_att // LANES
    ctx = lambda seg: pl.BlockSpec((None, cl, LANES), lambda p, b: (b, 0, seg * cb + p))
    return pl.pallas_call(
        _ctx_attn_kernel,
        grid=(cb, nb),
        in_specs=[ctx(4), ctx(1), ctx(2)],
        out_specs=pl.BlockSpec((None, cl, LANES), lambda p, b: (b, 0, p)),
        out_shape=jax.ShapeDtypeStruct((nb, cl, d_att), BF16),
        compiler_params=_params("parallel", "parallel"),
        name="context_attention",
    )(z_ctx, z_ctx, z_ctx)


def _merge_kernel(x_ref, g_ref, hs_ref, y_ref, na_ref, gr_ref, gn_ref, wr_ref, wn_ref, wo_ref, o_ref, t_scr):
    nb, ts, d = x_ref.shape
    c = hs_ref.shape[2]
    rows2d = lambda ref: ref[...].reshape(nb * ts, ref.shape[2])
    for t in range(ts):
        hs_t = hs_ref[t].astype(F32)
        for sl in range(c // LANES):
            t_scr.at[sl][pl.ds(t, nb, stride=XR_PITCH), :] = hs_t[:, sl * LANES:(sl + 1) * LANES]
    hs = jnp.concatenate([jnp.concatenate([t_scr[sl, b * XR_PITCH:b * XR_PITCH + ts, :] for sl in range(c // LANES)],
                                          axis=-1) for b in range(nb)], axis=0)
    y_rnn = (hs * _gelu_tanh(rows2d(y_ref).astype(F32))).astype(BF16)
    t_rnn = jnp.dot(y_rnn, wr_ref[...], preferred_element_type=F32)
    t_na = jnp.dot(rows2d(na_ref), wn_ref[...], preferred_element_type=F32)
    mix = _sigmoid(rows2d(gr_ref).astype(F32)) * t_rnn + _sigmoid(rows2d(gn_ref).astype(F32)) * t_na
    out = jnp.dot(mix.astype(BF16), wo_ref[...], preferred_element_type=F32)
    o_ref[...] = x_ref[...] + g_ref[...] * out.reshape(nb, ts, d)


def _merge(x3d, gate, hs_tm, t_off, z3, na3, w_rnn_o, w_na_o, w_out):
    nb, length, d = x3d.shape
    c = hs_tm.shape[2]
    da = na3.shape[2]
    ts = _pick_tile(length, IN_PROJ_STEPS, 16)
    assert ts + 8 <= XR_PITCH and t_off % ts == 0
    y_blk = (c + 2 * da) // c
    gr_blk = (2 * c + 3 * da) // d
    gn_blk = gr_blk + 1
    row = lambda width, blk=0: pl.BlockSpec((nb, ts, width), lambda i: (0, i, blk))
    whole = lambda a: pl.BlockSpec(a.shape, lambda i: (0,) * a.ndim, pipeline_mode=pl.Buffered(1))
    return pl.pallas_call(
        _merge_kernel,
        grid=(length // ts,),
        in_specs=[row(d), pl.BlockSpec(gate.shape, lambda i: (0, 0, 0)),
                  pl.BlockSpec((ts, nb, c), lambda i: (i + t_off // ts, 0, 0)),
                  row(c, y_blk), row(da), row(d, gr_blk), row(d, gn_blk),
                  whole(w_rnn_o), whole(w_na_o), whole(w_out)],
        out_specs=row(d),
        out_shape=jax.ShapeDtypeStruct((nb, length, d), F32),
        scratch_shapes=[pltpu.VMEM((c // LANES, nb * XR_PITCH, LANES), F32)],
        compiler_params=_params("parallel"),
        name="merge_out_proj",
    )(x3d, gate, hs_tm, z3, na3, z3, z3, w_rnn_o, w_na_o, w_out)


def _ffn_kernel(x_ref, sh_ref, sc_ref, g_ref, w1_ref, w3_ref, w2_ref, o_ref, *, tf):
    x = x_ref[...]
    h = _modulated_norm(x, sh_ref[...], sc_ref[...]).astype(BF16)
    acc = None
    for j in range(w1_ref.shape[1] // tf):
        cols = slice(j * tf, (j + 1) * tf)
        a = jnp.dot(h, w1_ref[:, cols], preferred_element_type=F32)
        b = jnp.dot(h, w3_ref[:, cols], preferred_element_type=F32)
        part = jnp.dot((_silu(a) * b).astype(BF16), w2_ref[cols, :], preferred_element_type=F32)
        acc = part if acc is None else acc + part
    o_ref[...] = x + g_ref[...] * acc


def _dense_ffn(x2d, shift, scale, gate, w1, w3, w2, rows_per_mod):
    m, d = x2d.shape
    dff = w1.shape[1]
    tm = _pick_tile(rows_per_mod, 512, 8)
    tf = _pick_tile(dff, 1536, LANES)
    tiles_per_mod = rows_per_mod // tm
    mod = pl.BlockSpec((None, 1, d), lambda i: (i // tiles_per_mod, 0, 0))
    whole = lambda w: pl.BlockSpec(w.shape, lambda i: (0, 0), pipeline_mode=pl.Buffered(1))
    return pl.pallas_call(
        functools.partial(_ffn_kernel, tf=tf),
        grid=(m // tm,),
        in_specs=[pl.BlockSpec((tm, d), lambda i: (i, 0)), mod, mod, mod, whole(w1), whole(w3), whole(w2)],
        out_specs=pl.BlockSpec((tm, d), lambda i: (i, 0)),
        out_shape=jax.ShapeDtypeStruct((m, d), F32),
        compiler_params=_params("parallel"),
        name="dense_swiglu",
    )(x2d, shift, scale, gate, w1, w3, w2)


SUBLANES = 8
GATHER_DMA_PRIORITY = 1


def _to_token_tiles(ref, x):
    n = x.shape[0]
    for s in range(SUBLANES):
        ref[pl.ds(s, n, stride=SUBLANES), :] = x[:, s * LANES:(s + 1) * LANES]


def _from_token_tiles(ref, n, s):
    return ref[pl.ds(s, n, stride=SUBLANES), :]


def _route_kernel(x_ref, sh_ref, sc_ref, wr_ref, wrl_ref, r_ref, *, n_experts):
    h = _modulated_norm(x_ref[...], sh_ref[...], sc_ref[...])
    h_hi = h.astype(BF16)
    h_lo = (h - h_hi.astype(F32)).astype(BF16)
    logits = (jnp.dot(h_hi, wr_ref[...], preferred_element_type=F32)
              + jnp.dot(h_lo, wr_ref[...], preferred_element_type=F32)
              + jnp.dot(h_hi, wrl_ref[...], preferred_element_type=F32))
    lane = lax.broadcasted_iota(jnp.int32, logits.shape, 1).astype(F32)
    neg = -jnp.inf
    lg = jnp.where(lane < n_experts, logits, neg)
    m1 = jnp.max(lg, axis=-1, keepdims=True)
    i1 = jnp.min(jnp.where(lg == m1, lane, float(LANES)), axis=-1, keepdims=True)
    lg2 = jnp.where(lane == i1, neg, lg)
    m2 = jnp.max(lg2, axis=-1, keepdims=True)
    i2 = jnp.min(jnp.where(lg2 == m2, lane, float(LANES)), axis=-1, keepdims=True)
    e = jnp.exp(m2 - m1)
    w1 = 1.0 / (1.0 + e)
    w2 = e / (1.0 + e)
    r_ref[...] = jnp.where(lane == 0, i1, jnp.where(lane == 1, i2, jnp.where(lane == 2, w1,
                           jnp.where(lane == 3, w2, 0.0))))


def _route(x2d, shift, scale, router, rows_per_mod):
    m, d = x2d.shape
    n_experts = router.shape[1]
    wr32 = jnp.zeros((d, LANES), F32).at[:, :n_experts].set(router)
    wr = wr32.astype(BF16)
    wrl = (wr32 - wr.astype(F32)).astype(BF16)
    tm = _pick_tile(rows_per_mod, 512, 8)
    tiles_per_mod = rows_per_mod // tm
    mod = pl.BlockSpec((None, 1, d), lambda i: (i // tiles_per_mod, 0, 0))
    rspec = pl.BlockSpec((d, LANES), lambda i: (0, 0))
    return pl.pallas_call(
        functools.partial(_route_kernel, n_experts=n_experts),
        grid=(m // tm,),
        in_specs=[pl.BlockSpec((tm, d), lambda i: (i, 0)), mod, mod, rspec, rspec],
        out_specs=pl.BlockSpec((tm, LANES), lambda i: (i, 0)),
        out_shape=jax.ShapeDtypeStruct((m, LANES), F32),
        compiler_params=_params("parallel"),
        name="moe_route",
    )(x2d, shift, scale, wr, wrl)


def _tile_rows(tile):
    start = tile * SUBLANES
    return pl.ds(start if isinstance(start, int) else pl.multiple_of(start, SUBLANES), SUBLANES)


def _tile_copy(src_ref, src_tile, dst_ref, dst_tile, sem):
    return pltpu.make_async_copy(src_ref.at[_tile_rows(src_tile), :], dst_ref.at[_tile_rows(dst_tile), :], sem)


def _dispatch_kernel(pos_ref, pad_ref, x_ref, sh_ref, sc_ref, xs_hbm, hbuf, zbuf, sem, zsem):
    i = pl.program_id(0)
    n = pl.num_programs(0)
    tm = x_ref.shape[0]
    m = n * tm
    pads_per_step = pad_ref.shape[0] // n
    slot = i % 2

    def wait_slot(sl):
        rows = TOP_K * tm * SUBLANES
        pltpu.make_async_copy(xs_hbm.at[pl.ds(0, rows), :], xs_hbm.at[pl.ds(0, rows), :], sem.at[sl]).wait()

    @pl.when(i >= 2)
    def _():
        wait_slot(slot)

    _to_token_tiles(hbuf.at[slot], _modulated_norm(x_ref[...], sh_ref[...], sc_ref[...]))
    for k in range(TOP_K):
        base = k * m + i * tm

        def put(j, carry):
            _tile_copy(hbuf.at[slot], j, xs_hbm, pos_ref[base + j], sem.at[slot]).start(priority=GATHER_DMA_PRIORITY)
            return carry

        lax.fori_loop(0, tm, put, 0, unroll=8)

    @pl.when(i == 0)
    def _():
        zbuf[...] = jnp.zeros_like(zbuf)

    def clear(q, carry):
        _tile_copy(zbuf, 0, xs_hbm, pad_ref[i * pads_per_step + q], zsem).start(priority=GATHER_DMA_PRIORITY)
        return carry

    lax.fori_loop(0, pads_per_step, clear, 0, unroll=8)

    @pl.when(i == n - 1)
    def _():
        wait_slot(slot)

        @pl.when(n >= 2)
        def _():
            wait_slot(1 - slot)

        rows = pad_ref.shape[0] * SUBLANES
        pltpu.make_async_copy(xs_hbm.at[pl.ds(0, rows), :], xs_hbm.at[pl.ds(0, rows), :], zsem).wait()


def _dispatch(x2d, shift, scale, pos, pad_rows, n_rows, rows_per_mod):
    m, d = x2d.shape
    assert d == SUBLANES * LANES
    tm = _pick_tile(rows_per_mod, 512, 8)
    tiles_per_mod = rows_per_mod // tm
    assert pad_rows.shape[0] % (m // tm) == 0
    mod = pl.BlockSpec((None, 1, d), lambda i, ps, pd: (i // tiles_per_mod, 0, 0))
    return pl.pallas_call(
        _dispatch_kernel,
        grid_spec=pltpu.PrefetchScalarGridSpec(
            num_scalar_prefetch=2,
            grid=(m // tm,),
            in_specs=[pl.BlockSpec((tm, d), lambda i, ps, pd: (i, 0)), mod, mod],
            out_specs=pl.BlockSpec(memory_space=pl.ANY),
            scratch_shapes=[pltpu.VMEM((2, tm * SUBLANES, LANES), F32), pltpu.VMEM((SUBLANES, LANES), F32),
                            pltpu.SemaphoreType.DMA((2,)), pltpu.SemaphoreType.DMA(())],
        ),
        out_shape=jax.ShapeDtypeStruct((n_rows * SUBLANES, LANES), F32),
        compiler_params=_params("arbitrary"),
        name="moe_dispatch",
    )(pos, pad_rows, x2d, shift, scale)


EXPERT_TILE_ROWS = 1024


def _expert_ffn_kernel(te_ref, nv_ref, x_ref, w1_ref, w3_ref, w2_ref, o_ref, h_scr, acc_scr):
    t = pl.program_id(0)
    f = pl.program_id(1)
    nv = nv_ref[0]
    tm = h_scr.shape[0]

    @pl.when(t < nv)
    def _():
        @pl.when(f == 0)
        def _():
            for s in range(SUBLANES):
                h_scr[:, s * LANES:(s + 1) * LANES] = _from_token_tiles(x_ref, tm, s).astype(BF16)
            acc_scr[...] = jnp.zeros_like(acc_scr)

        h = h_scr[...]
        a = jnp.dot(h, w1_ref[...].astype(BF16), preferred_element_type=F32)
        b = jnp.dot(h, w3_ref[...].astype(BF16), preferred_element_type=F32)
        acc_scr[...] += jnp.dot((_silu(a) * b).astype(BF16), w2_ref[...].astype(BF16),
                                preferred_element_type=F32)

        @pl.when(f == pl.num_programs(1) - 1)
        def _():
            _to_token_tiles(o_ref, acc_scr[...])

    @pl.when(jnp.logical_and(t >= nv, f == pl.num_programs(1) - 1))
    def _():
        o_ref[...] = jnp.zeros_like(o_ref)


def _expert_ffn(xs, tile_expert, n_valid, w1, w3, w2):
    tm = EXPERT_TILE_ROWS
    n = xs.shape[0] // SUBLANES
    d = w1.shape[1]
    dfe = w1.shape[2]
    tf = _pick_tile(dfe, 512, LANES)
    nf = dfe // tf

    def fsel(t, f, nv):
        return jnp.where(t < nv[0], f, nf - 1)

    return pl.pallas_call(
        _expert_ffn_kernel,
        grid_spec=pltpu.PrefetchScalarGridSpec(
            num_scalar_prefetch=2,
            grid=(n // tm, nf),
            in_specs=[pl.BlockSpec((tm * SUBLANES, LANES), lambda t, f, te, nv: (t, 0)),
                      pl.BlockSpec((None, d, tf), lambda t, f, te, nv: (te[t], 0, fsel(t, f, nv))),
                      pl.BlockSpec((None, d, tf), lambda t, f, te, nv: (te[t], 0, fsel(t, f, nv))),
                      pl.BlockSpec((None, tf, d), lambda t, f, te, nv: (te[t], fsel(t, f, nv), 0))],
            out_specs=pl.BlockSpec((tm * SUBLANES, LANES), lambda t, f, te, nv: (t, 0)),
            scratch_shapes=[pltpu.VMEM((tm, d), BF16), pltpu.VMEM((tm, d), F32)],
        ),
        out_shape=jax.ShapeDtypeStruct((n * SUBLANES, LANES), F32),
        compiler_params=_params("arbitrary", "arbitrary"),
        name="expert_swiglu",
    )(tile_expert, n_valid, xs, w1, w3, w2)


def _combine_kernel(pos_ref, x_ref, g_ref, r_ref, ys_hbm, o_ref, ybuf, sem):
    i = pl.program_id(0)
    n = pl.num_programs(0)
    tm = x_ref.shape[0]
    m = n * tm

    def issue_rows(tile, slot):
        for k in range(TOP_K):
            base = k * m + tile * tm

            def body(j, carry):
                src_row = pl.multiple_of(pos_ref[base + j] * SUBLANES, SUBLANES)
                dst_row = pl.multiple_of(j * SUBLANES, SUBLANES)
                pltpu.make_async_copy(ys_hbm.at[pl.ds(src_row, SUBLANES), :],
                                      ybuf.at[slot, k, pl.ds(dst_row, SUBLANES), :],
                                      sem.at[slot]).start(priority=GATHER_DMA_PRIORITY)
                return carry

            lax.fori_loop(0, tm, body, 0, unroll=8)

    slot = i % 2

    @pl.when(i == 0)
    def _():
        issue_rows(0, 0)

    pltpu.make_async_copy(ybuf.at[1 - slot], ybuf.at[slot], sem.at[slot]).wait()

    @pl.when(i + 1 < n)
    def _():
        issue_rows(i + 1, 1 - slot)

    r = r_ref[...]
    w1, w2 = r[:, 2:3], r[:, 3:4]
    for s in range(SUBLANES):
        lanes = slice(s * LANES, (s + 1) * LANES)
        mix = (w1 * _from_token_tiles(ybuf.at[slot, 0], tm, s) + w2 * _from_token_tiles(ybuf.at[slot, 1], tm, s))
        o_ref[:, lanes] = x_ref[:, lanes] + g_ref[:, lanes] * mix


def _combine(x2d, gate, route, ys, pos, rows_per_mod):
    m, d = x2d.shape
    tm = _pick_tile(rows_per_mod, 512, 8)
    tiles_per_mod = rows_per_mod // tm
    return pl.pallas_call(
        _combine_kernel,
        grid_spec=pltpu.PrefetchScalarGridSpec(
            num_scalar_prefetch=1,
            grid=(m // tm,),
            in_specs=[pl.BlockSpec((tm, d), lambda i, ps: (i, 0)),
                      pl.BlockSpec((None, 1, d), lambda i, ps: (i // tiles_per_mod, 0, 0)),
                      pl.BlockSpec((tm, LANES), lambda i, ps: (i, 0)),
                      pl.BlockSpec(memory_space=pl.ANY)],
            out_specs=pl.BlockSpec((tm, d), lambda i, ps: (i, 0)),
            scratch_shapes=[pltpu.VMEM((2, TOP_K, tm * SUBLANES, LANES), F32), pltpu.SemaphoreType.DMA((2,))],
        ),
        out_shape=jax.ShapeDtypeStruct((m, d), F32),
        compiler_params=_params("arbitrary"),
        name="moe_combine",
    )(pos, x2d, gate, route, ys)


def _moe_ffn(x2d, shift, scale, gate, router, w1, w3, w2, rows_per_mod):
    m, d = x2d.shape
    n_experts = router.shape[1]
    route = _route(x2d, shift, scale, router, rows_per_mod)
    tm = EXPERT_TILE_ROWS
    expert = route[:, :TOP_K].astype(jnp.int32).T.reshape(-1)
    onehot = (expert[:, None] == jnp.arange(n_experts)[None, :]).astype(jnp.int32)
    csum = jnp.cumsum(onehot, axis=0)
    rank = jnp.sum(onehot * (csum - 1), axis=1)
    counts = csum[-1]
    tiles = (counts + tm - 1) // tm
    tile_end = jnp.cumsum(tiles)
    first_row = (tile_end - tiles) * tm
    pos = (jnp.sum(onehot * first_row[None, :], axis=1) + rank).astype(jnp.int32)
    n_tiles = (TOP_K * m) // tm + n_experts
    n_rows = n_tiles * tm
    gap_len = jnp.concatenate([tiles * tm - counts, (n_rows - tile_end[-1:] * tm)])
    gap_row = jnp.concatenate([first_row + counts, tile_end[-1:] * tm])
    gap_end = jnp.cumsum(gap_len)
    q = jnp.arange(n_rows - TOP_K * m)
    in_gap = jnp.logical_and(q[:, None] >= (gap_end - gap_len)[None, :], q[:, None] < gap_end[None, :])
    pad_rows = jnp.sum(jnp.where(in_gap, (gap_row - (gap_end - gap_len))[None, :] + q[:, None], 0),
                       axis=1).astype(jnp.int32)
    tile_expert = jnp.minimum(jnp.sum((jnp.arange(n_tiles)[:, None] >= tile_end[None, :]).astype(jnp.int32), axis=1),
                              n_experts - 1).astype(jnp.int32)
    n_valid = tile_end[-1:].astype(jnp.int32)
    xs = _dispatch(x2d, shift, scale, pos, pad_rows, n_rows, rows_per_mod)
    ys = _expert_ffn(xs, tile_expert, n_valid, w1, w3, w2)
    return _combine(x2d, gate, route, ys, pos, rows_per_mod)


def kernel(x, c, ctx, c_ctx, w_mod, b_mod, w_in, conv_w, conv_b, rg_lambda, rg_w, rg_b, q_gain, k_gain, rpb,
           w_rnn_o, w_na_o, w_out, ffn_w1, ffn_w3, ffn_w2, router, moe_w1, moe_w3, moe_w2):
    nb, seq, d = x.shape
    cl = ctx.shape[1]
    depth = w_mod.shape[0]
    c_rnn = conv_w.shape[2]
    d_att = rpb.shape[1] * HEAD_DIM
    ctx_cols = c_rnn + 2 * d_att
    rows = seq // GRID_W

    n_cond = -(-(nb + 1) // 8) * 8
    cond = jnp.zeros((n_cond, d), F32).at[:nb].set(c).at[nb].set(c_ctx)
    mods = _adaln(cond, w_mod, b_mod)

    x2 = x.reshape(nb * seq, d)
    xc2 = ctx.reshape(nb * cl, d)
    for l in range(depth):
        ctx_out = l < depth - 1
        lat = [mods[l, :nb, k * d:(k + 1) * d].reshape(nb, 1, d) for k in range(N_MOD)]
        cmod = [mods[l, nb:nb + 1, k * d:(k + 1) * d].reshape(1, 1, d) for k in range(N_MOD)]
        w_in_l = w_in[l].astype(BF16)
        tile_heads = lambda g: jnp.tile(g.astype(F32), c_rnn // HEAD_DIM).reshape(1, c_rnn)
        head_gains = jnp.stack([tile_heads(k_gain[l]), tile_heads(q_gain[l]) * Q_SCALE])

        z_lat3, xr_lat = _norm_matmul(x2.reshape(nb, seq, d), lat[0], lat[1], w_in_l, head_gains, c_rnn, (1, 4))
        if ctx_out:
            z_ctx3, xr_ctx = _norm_matmul(xc2.reshape(nb, cl, d), cmod[0], cmod[1], w_in_l, head_gains, c_rnn, (1, 4))
        else:
            z_ctx3, xr_ctx = _norm_matmul(xc2.reshape(nb, cl, d), cmod[0], cmod[1], w_in_l[:, :ctx_cols],
                                          head_gains, c_rnn, (1,))

        wg_f, bg_f = _gate_weights(rg_w[l, 0], rg_b[l, 0])
        wg_r, bg_r = _gate_weights(rg_w[l, 1], rg_b[l, 1])
        hs_r, xc = _rnn_scan_conv(xr_ctx, xr_lat, conv_w[l], conv_b[l], wg_r, bg_r, rg_lambda[l, 1], True)
        hs = _rnn_scan_reuse(xc, cl, wg_f, bg_f, rg_lambda[l, 0], False, hs_r)

        bias = _na_bias_table(rpb[l], rows)
        na_lat = _neighbourhood_attention(z_lat3, z_ctx3, bias, seq)

        wr, wn, wo = w_rnn_o[l].astype(BF16), w_na_o[l].astype(BF16), w_out[l].astype(BF16)
        x2 = _merge(x2.reshape(nb, seq, d), lat[2], hs, cl, z_lat3, na_lat, wr, wn, wo).reshape(nb * seq, d)
        if ctx_out:
            na_ctx = _context_attention(z_ctx3, d_att)
            xc2 = _merge(xc2.reshape(nb, cl, d), cmod[2], hs, 0, z_ctx3, na_ctx, wr, wn, wo).reshape(nb * cl, d)

        j = l // 2
        if l % 2 == 0:
            w1, w3, w2 = ffn_w1[j].astype(BF16), ffn_w3[j].astype(BF16), ffn_w2[j].astype(BF16)
            x2 = _dense_ffn(x2, lat[3], lat[4], lat[5], w1, w3, w2, seq)
            if ctx_out:
                xc2 = _dense_ffn(xc2, cmod[3], cmod[4], cmod[5], w1, w3, w2, nb * cl)
        else:
            w1, w3, w2 = moe_w1[j], moe_w3[j], moe_w2[j]
            x2 = _moe_ffn(x2, lat[3], lat[4], lat[5], router[j], w1, w3, w2, seq)
            if ctx_out:
                xc2 = _moe_ffn(xc2, cmod[3], cmod[4], cmod[5], router[j], w1, w3, w2, nb * cl)
    return x2.reshape(nb, seq, d)
```

```python
import functools

import numpy as np
import jax
import jax.numpy as jnp
from jax import lax
from jax.experimental import pallas as pl
from jax.experimental.pallas import tpu as pltpu

F32 = jnp.float32
BF16 = jnp.bfloat16

EPS = 1e-6
N_MOD = 6
GRID_W = 64
WIN_H = 8
WIN_W = 16
HEAD_DIM = 64
RNN_BW = 64
CONV_TAPS = 4
CONV_LEFT = 2
LRU_C = 8.0
TOP_K = 2
MASK_VALUE = -1e30
SQRT_FLOOR = 1e-30

LANES = 128
MXU_DIM = 256
VMEM_LIMIT_BYTES = 56 * 1024 * 1024

NA_GROUP_ROWS = 4
NA_KEY_ROWS = 12


def _params(*sem):
    return pltpu.CompilerParams(dimension_semantics=sem, vmem_limit_bytes=VMEM_LIMIT_BYTES)


def _sigmoid(x):
    return 0.5 * (jnp.tanh(0.5 * x) + 1.0)


def _silu(x):
    return x * _sigmoid(x)


def _gelu_tanh(x):
    return 0.5 * x * (1.0 + jnp.tanh(np.sqrt(2.0 / np.pi) * (x + 0.044715 * (x * x * x))))


def _modulated_norm(x, shift, scale):
    ms = jnp.mean(x * x, axis=-1, keepdims=True)
    return x * lax.rsqrt(ms + EPS) * (1.0 + scale) + shift


def _pick_tile(n, cap, mult):
    best = None
    for t in range(mult, min(n, cap) + 1, mult):
        if n % t == 0:
            best = t
    assert best is not None, (n, cap, mult)
    return best


def _adaln_kernel(c_ref, w_ref, b_ref, o_ref):
    s = _silu(c_ref[...]).astype(BF16)
    o_ref[...] = jnp.dot(s, w_ref[...].astype(BF16), preferred_element_type=F32) + b_ref[...]


def _adaln(cond, w_mod, b_mod):
    depth, d, n = w_mod.shape
    r = cond.shape[0]
    tn = _pick_tile(n, 1536, LANES)
    return pl.pallas_call(
        _adaln_kernel,
        grid=(depth, n // tn),
        in_specs=[
            pl.BlockSpec((r, d), lambda l, j: (0, 0)),
            pl.BlockSpec((None, d, tn), lambda l, j: (l, 0, j)),
            pl.BlockSpec((None, 1, tn), lambda l, j: (l, 0, j)),
        ],
        out_specs=pl.BlockSpec((None, r, tn), lambda l, j: (l, 0, j)),
        out_shape=jax.ShapeDtypeStruct((depth, r, n), F32),
        compiler_params=_params("parallel", "parallel"),
        name="adaln",
    )(cond, w_mod, b_mod.reshape(depth, 1, n))


def _head_norm(x, gain, in_a):
    sq = x * x
    sa = jnp.sum(jnp.where(in_a, sq, 0.0), axis=-1, keepdims=True)
    sb = jnp.sum(jnp.where(in_a, 0.0, sq), axis=-1, keepdims=True)
    ms = jnp.where(in_a, sa, sb) * (1.0 / HEAD_DIM)
    return x * lax.rsqrt(ms + EPS) * gain


IN_PROJ_STEPS = 32
XR_PITCH = 40


def _norm_matmul_kernel(x_ref, sh_ref, sc_ref, w_ref, hg_ref, o_ref, xr_ref, t_scr, *, seg, normed):
    nb, ts, d = x_ref.shape
    h = _modulated_norm(x_ref[...], sh_ref[...], sc_ref[...]).reshape(nb * ts, d).astype(BF16)
    in_a = lax.broadcasted_iota(jnp.int32, (1, LANES), 1) < HEAD_DIM
    for j in range(w_ref.shape[1] // seg):
        acc = jnp.dot(h, w_ref[:, j * seg:(j + 1) * seg], preferred_element_type=F32)
        if j == 0:
            for sl in range(seg // LANES):
                for b in range(nb):
                    t_scr[sl, b * XR_PITCH:b * XR_PITCH + ts, :] = acc[b * ts:(b + 1) * ts, sl * LANES:(sl + 1) * LANES]
            for t in range(ts):
                xr_ref[t] = jnp.concatenate([t_scr.at[sl][pl.ds(t, nb, stride=XR_PITCH), :]
                                             for sl in range(seg // LANES)], axis=-1).astype(xr_ref.dtype)
        if j in normed:
            gain = hg_ref[normed.index(j)]
            for sl in range(seg // LANES):
                lanes = slice(sl * LANES, (sl + 1) * LANES)
                o_ref[:, :, j * seg + sl * LANES:j * seg + (sl + 1) * LANES] = _head_norm(
                    acc[:, lanes], gain[:, lanes], in_a).reshape(nb, ts, LANES).astype(o_ref.dtype)
        else:
            o_ref[:, :, j * seg:(j + 1) * seg] = acc.reshape(nb, ts, seg).astype(o_ref.dtype)


def _norm_matmul(x3d, shift, scale, w, head_gains, seg, normed):
    nb, length, d = x3d.shape
    n = w.shape[1]
    ts = _pick_tile(length, IN_PROJ_STEPS, 16)
    assert ts + 8 <= XR_PITCH
    whole = lambda a: pl.BlockSpec(a.shape, lambda i: (0,) * a.ndim)
    return pl.pallas_call(
        functools.partial(_norm_matmul_kernel, seg=seg, normed=normed),
        grid=(length // ts,),
        in_specs=[
            pl.BlockSpec((nb, ts, d), lambda i: (0, i, 0)),
            whole(shift),
            whole(scale),
            pl.BlockSpec((d, n), lambda i: (0, 0), pipeline_mode=pl.Buffered(1)),
            whole(head_gains),
        ],
        out_specs=[pl.BlockSpec((nb, ts, n), lambda i: (0, i, 0)),
                   pl.BlockSpec((ts, nb, seg), lambda i: (i, 0, 0))],
        out_shape=[jax.ShapeDtypeStruct((nb, length, n), BF16), jax.ShapeDtypeStruct((length, nb, seg), BF16)],
        scratch_shapes=[pltpu.VMEM((seg // LANES, nb * XR_PITCH, LANES), F32)],
        compiler_params=_params("parallel"),
        name="norm_in_proj",
    )(x3d, shift, scale, w, head_gains)


def _scan_tile_index(i, n_ctx_tiles, n_tiles, reverse):
    if not reverse:
        return i
    return jnp.where(i < n_ctx_tiles, n_ctx_tiles - 1 - i, n_tiles - 1 - (i - n_ctx_tiles))


def _gates_and_scan(xc, wg_ref, bg_ref, lam_ref, prev_ref, out_ref, a_scr, u_scr, h_scr, reverse):
    tl, nb, c = out_ref.shape

    @pl.when(pl.program_id(0) == 0)
    def _():
        h_scr[...] = jnp.zeros_like(h_scr)

    lam = lam_ref[...]
    log_sig = jnp.minimum(lam, 0.0) - jnp.log(1.0 + jnp.exp(-jnp.abs(lam)))
    k2 = (0.5 * LRU_C * np.log2(np.e)) * log_sig
    for j in range(c // MXU_DIM):
        sl = slice(MXU_DIM * j, MXU_DIM * (j + 1))
        xj = xc[:, sl]
        t = jnp.tanh(jnp.dot(xj.astype(BF16), wg_ref[j], preferred_element_type=F32) + bg_ref[j])
        a = jnp.exp2(k2[:, sl] * t[:, :MXU_DIM] + k2[:, sl])
        v = 1.0 - a * a
        gated_x = (t[:, MXU_DIM:] + 1.0) * (0.5 * xj)
        a_scr[:, sl] = a
        u_scr[:, sl] = (v * lax.rsqrt(jnp.maximum(v, SQRT_FLOOR))) * gated_x

    def step(s, h):
        tt = (tl - 1 - s) if reverse else s
        row = pl.multiple_of(tt * nb, nb)
        h = a_scr[pl.ds(row, nb), :] * h + u_scr[pl.ds(row, nb), :]
        val = h if prev_ref is None else h + prev_ref[tt].astype(F32)
        out_ref[tt] = val.astype(out_ref.dtype)
        return h

    h_scr[...] = lax.fori_loop(0, tl, step, h_scr[...], unroll=4)


def _rnn_conv_kernel(cp_ref, c_ref, cn_ref, lp_ref, l_ref, ln_ref, cw_ref, cb_ref, wg_ref, bg_ref, lam_ref,
                     out_ref, xc_ref, a_scr, u_scr, h_scr, *, n_ctx_tiles, n_tiles, reverse):
    tl, nb, c = c_ref.shape
    ti = _scan_tile_index(pl.program_id(0), n_ctx_tiles, n_tiles, reverse)
    in_ctx = ti < n_ctx_tiles
    pick = lambda ctx_ref, lat_ref: jnp.where(in_ctx, ctx_ref[...], lat_ref[...]).astype(F32)
    at_start = jnp.logical_or(ti == 0, ti == n_ctx_tiles)
    at_end = jnp.logical_or(ti == n_ctx_tiles - 1, ti == n_tiles - 1)
    xp = jnp.where(at_start, 0.0, pick(cp_ref, lp_ref))
    xn = jnp.where(at_end, 0.0, pick(cn_ref, ln_ref))
    xe = jnp.concatenate([xp, pick(c_ref, l_ref), xn], axis=0)
    xc = cb_ref[...] + xe[0:tl] * cw_ref[0:1, :]
    for k in range(1, CONV_TAPS):
        xc = xc + xe[k:k + tl] * cw_ref[k:k + 1, :]
    xc_ref[...] = xc.astype(xc_ref.dtype)
    _gates_and_scan(xc.reshape(tl * nb, c), wg_ref, bg_ref, lam_ref, None, out_ref, a_scr, u_scr, h_scr, reverse)


def _rnn_reuse_kernel(xc_ref, wg_ref, bg_ref, lam_ref, prev_ref, out_ref, a_scr, u_scr, h_scr, *, reverse):
    tl, nb, c = xc_ref.shape
    _gates_and_scan(xc_ref[...].astype(F32).reshape(tl * nb, c), wg_ref, bg_ref, lam_ref, prev_ref, out_ref,
                    a_scr, u_scr, h_scr, reverse)


def _scan_scratch(tl, nb, c):
    return [pltpu.VMEM((tl * nb, c), F32), pltpu.VMEM((tl * nb, c), F32), pltpu.VMEM((nb, c), F32)]


def _rnn_scan_conv(xr_ctx, xr_lat, conv_w, conv_b, wg, bg, lam, reverse):
    n_ctx, nb, c = xr_ctx.shape
    n_lat = xr_lat.shape[0]
    t = n_ctx + n_lat
    tl = _pick_tile(n_ctx, 64, 2)
    assert n_lat % tl == 0 and n_ctx % tl == 0
    n_tiles, n_ctx_tiles = t // tl, n_ctx // tl
    tile = functools.partial(_scan_tile_index, n_ctx_tiles=n_ctx_tiles, n_tiles=n_tiles, reverse=reverse)
    half = tl // CONV_LEFT
    full = lambda shape: pl.BlockSpec(shape, lambda i: (0,) * len(shape))

    def seq_specs(first_tile, length):
        last = length // tl - 1
        local = lambda i: jnp.clip(tile(i) - first_tile, 0, last)
        return [pl.BlockSpec((CONV_LEFT, nb, c), lambda i: (jnp.maximum(local(i) * half - 1, 0), 0, 0)),
                pl.BlockSpec((tl, nb, c), lambda i: (local(i), 0, 0)),
                pl.BlockSpec((1, nb, c), lambda i: (jnp.minimum((local(i) + 1) * tl, length - 1), 0, 0))]

    by_tile = pl.BlockSpec((tl, nb, c), lambda i: (tile(i), 0, 0))
    return pl.pallas_call(
        functools.partial(_rnn_conv_kernel, n_ctx_tiles=n_ctx_tiles, n_tiles=n_tiles, reverse=reverse),
        grid=(n_tiles,),
        in_specs=seq_specs(0, n_ctx) + seq_specs(n_ctx_tiles, n_lat) + [
            full((CONV_TAPS, c)), full((1, c)), full(wg.shape), full(bg.shape), full((1, c))],
        out_specs=[by_tile, by_tile],
        out_shape=[jax.ShapeDtypeStruct((t, nb, c), BF16), jax.ShapeDtypeStruct((t, nb, c), BF16)],
        scratch_shapes=_scan_scratch(tl, nb, c),
        compiler_params=_params("arbitrary"),
        name="rglru_scan_rev" if reverse else "rglru_scan_fwd",
    )(xr_ctx, xr_ctx, xr_ctx, xr_lat, xr_lat, xr_lat, conv_w, conv_b.reshape(1, c), wg, bg, lam.reshape(1, c))


def _rnn_scan_reuse(xc, n_ctx, wg, bg, lam, reverse, prev):
    t, nb, c = xc.shape
    tl = _pick_tile(n_ctx, 64, 2)
    n_tiles, n_ctx_tiles = t // tl, n_ctx // tl
    tile = functools.partial(_scan_tile_index, n_ctx_tiles=n_ctx_tiles, n_tiles=n_tiles, reverse=reverse)
    full = lambda shape: pl.BlockSpec(shape, lambda i: (0,) * len(shape))
    by_tile = pl.BlockSpec((tl, nb, c), lambda i: (tile(i), 0, 0))
    return pl.pallas_call(
        functools.partial(_rnn_reuse_kernel, reverse=reverse),
        grid=(n_tiles,),
        in_specs=[by_tile, full(wg.shape), full(bg.shape), full((1, c)), by_tile],
        out_specs=by_tile,
        out_shape=jax.ShapeDtypeStruct((t, nb, c), BF16),
        scratch_shapes=_scan_scratch(tl, nb, c),
        compiler_params=_params("arbitrary"),
        name="rglru_scan_rev" if reverse else "rglru_scan_fwd",
    )(xc, wg, bg, lam.reshape(1, c), prev)


def _gate_weights(rg_w_d, rg_b_d):
    _, nblk, bw, _ = rg_w_d.shape
    per = MXU_DIM // bw
    ngrp = nblk // per
    w = rg_w_d.reshape(2, ngrp, per, bw, bw)
    eye = jnp.eye(per, dtype=rg_w_d.dtype)
    dense = w[:, :, :, :, None, :] * eye[None, None, :, None, :, None]
    dense = dense.reshape(2, ngrp, MXU_DIM, MXU_DIM)
    wg = jnp.concatenate([dense[0], dense[1]], axis=-1).astype(BF16)
    b = rg_b_d.reshape(2, ngrp, 1, MXU_DIM)
    bg = jnp.concatenate([b[0], b[1]], axis=-1)
    return wg * 0.5, bg * 0.5


Q_SCALE = (HEAD_DIM ** -0.5) * float(np.log2(np.e))


def _dot_nt(a, b):
    return lax.dot_general(a, b, (((1,), (1,)), ((), ())), preferred_element_type=F32)


def _stack_heads(q, in_a):
    zero = jnp.zeros_like(q)
    return jnp.concatenate([jnp.where(in_a, q, zero), jnp.where(in_a, zero, q)], axis=0)


def _na_kernel(q_ref, k_ref, v_ref, kc_ref, vc_ref, bias_ref, o_ref, sa_scr, sb_scr, *, rows):
    cl = kc_ref.shape[0]
    gq = NA_GROUP_ROWS * GRID_W
    gk = NA_KEY_ROWS * GRID_W
    ngrp = rows // NA_GROUP_ROWS
    in_a = lax.broadcasted_iota(jnp.int32, (1, LANES), 1) < HEAD_DIM

    def key_base(g):
        kb = jnp.clip(NA_GROUP_ROWS * g - WIN_H // 2, 0, rows - NA_KEY_ROWS)
        return pl.multiple_of(kb * GRID_W, GRID_W)

    def scores(g, s_scr):
        cls = jnp.where(g == 0, 0, jnp.where(g == ngrp - 1, 2, 1))
        q2 = _stack_heads(q_ref[pl.ds(pl.multiple_of(g * gq, gq), gq), :], in_a)
        s_scr[:, 0:gk] = _dot_nt(q2, k_ref[pl.ds(key_base(g), gk), :]) + bias_ref[cls]
        s_scr[:, gk:gk + cl] = _dot_nt(q2, kc_ref[...])

    def attend(g, s_scr):
        sc = s_scr[...]
        p = jnp.exp2(sc - jnp.max(sc, axis=-1, keepdims=True))
        den = jnp.sum(p, axis=-1, keepdims=True)
        pb = p.astype(BF16)
        o2 = (jnp.dot(pb[:, 0:gk], v_ref[pl.ds(key_base(g), gk), :], preferred_element_type=F32)
              + jnp.dot(pb[:, gk:gk + cl], vc_ref[...], preferred_element_type=F32)) / den
        o_ref[pl.ds(pl.multiple_of(g * gq, gq), gq), :] = jnp.where(in_a, o2[:gq], o2[gq:]).astype(o_ref.dtype)

    scores(0, sa_scr)

    def pair(it, carry):
        g = 2 * it
        scores(g + 1, sb_scr)
        attend(g, sa_scr)
        scores(g + 2, sa_scr)
        attend(g + 1, sb_scr)
        return carry

    lax.fori_loop(0, ngrp // 2 - 1, pair, 0)
    scores(ngrp - 1, sb_scr)
    attend(ngrp - 2, sa_scr)
    attend(ngrp - 1, sb_scr)


def _bias_table_kernel(idx_ref, t_ref, o_ref):
    c = pl.program_id(1)
    left = lax.broadcasted_iota(jnp.int32, (1, LANES), 1) < GRID_W
    per_class = NA_GROUP_ROWS * NA_KEY_ROWS
    for g in range(2):
        for a in range(NA_GROUP_ROWS):
            r0 = (g * NA_GROUP_ROWS + a) * GRID_W
            for j in range(NA_KEY_ROWS // 2):
                o_even = idx_ref[c * per_class + a * NA_KEY_ROWS + 2 * j]
                o_odd = idx_ref[c * per_class + a * NA_KEY_ROWS + 2 * j + 1]
                o_ref[r0:r0 + GRID_W, j * LANES:(j + 1) * LANES] = jnp.where(left, t_ref[g, o_even], t_ref[g, o_odd])


def _na_bias_table(rpb, rows):
    nh, n_ro, n_co = rpb.shape
    ngrp = rows // NA_GROUP_ROWS
    qc = np.arange(GRID_W)[:, None]
    kc = np.arange(GRID_W)[None, :]
    wstart = np.clip(qc - WIN_W // 2, 0, GRID_W - WIN_W)
    col_valid = (kc >= wstart) & (kc < wstart + WIN_W)
    col_onehot = ((kc - qc + WIN_W - 1)[None] == np.arange(n_co)[:, None, None])
    rpb2 = rpb.astype(F32).reshape(nh // 2, 2, n_ro, n_co)
    by_col = jnp.einsum("pgrj,jqk->pgrqk", rpb2, col_onehot.astype(np.float32),
                        precision=lax.Precision.HIGHEST) * np.float32(np.log2(np.e))
    by_col = jnp.where(col_valid, by_col, MASK_VALUE)
    by_col = jnp.concatenate([by_col, jnp.full((nh // 2, 2, 1, GRID_W, GRID_W), MASK_VALUE, F32)], axis=2)
    tables = jnp.concatenate([by_col, by_col], axis=-1)
    idx = np.full((3, NA_GROUP_ROWS, NA_KEY_ROWS), n_ro, np.int32)
    for ci, g in enumerate((0, 1, ngrp - 1)):
        kb = int(np.clip(NA_GROUP_ROWS * g - WIN_H // 2, 0, rows - NA_KEY_ROWS))
        for a in range(NA_GROUP_ROWS):
            r = NA_GROUP_ROWS * g + a
            start = int(np.clip(r - WIN_H // 2, 0, rows - WIN_H))
            for cr in range(NA_KEY_ROWS):
                if start <= kb + cr < start + WIN_H:
                    idx[ci, a, cr] = kb + cr - r + WIN_H - 1
    gq, gk = NA_GROUP_ROWS * GRID_W, NA_KEY_ROWS * GRID_W
    return pl.pallas_call(
        _bias_table_kernel,
        grid_spec=pltpu.PrefetchScalarGridSpec(
            num_scalar_prefetch=1,
            grid=(nh // 2, 3),
            in_specs=[pl.BlockSpec((None,) + tables.shape[1:], lambda p, c, ix: (p, 0, 0, 0, 0))],
            out_specs=pl.BlockSpec((None, None, 2 * gq, gk), lambda p, c, ix: (p, c, 0, 0)),
        ),
        out_shape=jax.ShapeDtypeStruct((nh // 2, 3, 2 * gq, gk), F32),
        compiler_params=_params("parallel", "parallel"),
        name="na_bias_table",
    )(jnp.asarray(idx.reshape(-1)), tables)


def _neighbourhood_attention(z_lat, z_ctx, bias, seq):
    nb, s, _ = z_lat.shape
    cl = z_ctx.shape[1]
    npair = bias.shape[0]
    d_att = npair * LANES
    cb = npair
    rows = s // GRID_W
    assert s % GRID_W == 0 and rows % (2 * NA_GROUP_ROWS) == 0 and rows >= NA_KEY_ROWS + NA_GROUP_ROWS
    s_shape = (2 * NA_GROUP_ROWS * GRID_W, NA_KEY_ROWS * GRID_W + cl)
    lat = lambda seg: pl.BlockSpec((None, s, LANES), lambda p, b: (b, 0, seg * cb + p))
    ctx = lambda seg: pl.BlockSpec((None, cl, LANES), lambda p, b: (b, 0, seg * cb + p))
    return pl.pallas_call(
        functools.partial(_na_kernel, rows=rows),
        grid=(npair, nb),
        in_specs=[lat(4), lat(1), lat(2), ctx(1), ctx(2),
                  pl.BlockSpec((None,) + bias.shape[1:], lambda p, b: (p, 0, 0, 0))],
        out_specs=pl.BlockSpec((None, s, LANES), lambda p, b: (b, 0, p)),
        out_shape=jax.ShapeDtypeStruct((nb, s, d_att), BF16),
        scratch_shapes=[pltpu.VMEM(s_shape, F32), pltpu.VMEM(s_shape, F32)],
        compiler_params=_params("parallel", "parallel"),
        name="neighbourhood_attention",
    )(z_lat, z_lat, z_lat, z_ctx, z_ctx, bias)


def _ctx_attn_kernel(q_ref, k_ref, v_ref, o_ref):
    in_a = lax.broadcasted_iota(jnp.int32, (1, LANES), 1) < HEAD_DIM
    n = q_ref.shape[0]
    sc = _dot_nt(_stack_heads(q_ref[...], in_a), k_ref[...])
    p = jnp.exp2(sc - jnp.max(sc, axis=-1, keepdims=True))
    den = jnp.sum(p, axis=-1, keepdims=True)
    o2 = jnp.dot(p.astype(BF16), v_ref[...], preferred_element_type=F32) / den
    o_ref[...] = jnp.where(in_a, o2[:n], o2[n:]).astype(o_ref.dtype)


def _context_attention(z_ctx, d_att):
    nb, cl, _ = z_ctx.shape
    cb = d_att // LANES
    ctx = lambda seg: pl.BlockSpec((None, cl, LANES), lambda p, b: (b, 0, seg * cb + p))
    return pl.pallas_call(
        _ctx_attn_kernel,
        grid=(cb, nb),
        in_specs=[ctx(4), ctx(1), ctx(2)],
        out_specs=pl.BlockSpec((None, cl, LANES), lambda p, b: (b, 0, p)),
        out_shape=jax.ShapeDtypeStruct((nb, cl, d_att), BF16),
        compiler_params=_params("parallel", "parallel"),
        name="context_attention",
    )(z_ctx, z_ctx, z_ctx)


def _merge_kernel(x_ref, g_ref, hs_ref, y_ref, na_ref, gr_ref, gn_ref, wr_ref, wn_ref, wo_ref, o_ref, t_scr):
    nb, ts, d = x_ref.shape
    c = hs_ref.shape[2]
    rows2d = lambda ref: ref[...].reshape(nb * ts, ref.shape[2])
    for t in range(ts):
        hs_t = hs_ref[t].astype(F32)
        for sl in range(c // LANES):
            t_scr.at[sl][pl.ds(t, nb, stride=XR_PITCH), :] = hs_t[:, sl * LANES:(sl + 1) * LANES]
    hs = jnp.concatenate([jnp.concatenate([t_scr[sl, b * XR_PITCH:b * XR_PITCH + ts, :] for sl in range(c // LANES)],
                                          axis=-1) for b in range(nb)], axis=0)
    y_rnn = (hs * _gelu_tanh(rows2d(y_ref).astype(F32))).astype(BF16)
    t_rnn = jnp.dot(y_rnn, wr_ref[...], preferred_element_type=F32)
    t_na = jnp.dot(rows2d(na_ref), wn_ref[...], preferred_element_type=F32)
    mix = _sigmoid(rows2d(gr_ref).astype(F32)) * t_rnn + _sigmoid(rows2d(gn_ref).astype(F32)) * t_na
    out = jnp.dot(mix.astype(BF16), wo_ref[...], preferred_element_type=F32)
    o_ref[...] = x_ref[...] + g_ref[...] * out.reshape(nb, ts, d)


def _merge(x3d, gate, hs_tm, t_off, z3, na3, w_rnn_o, w_na_o, w_out):
    nb, length, d = x3d.shape
    c = hs_tm.shape[2]
    da = na3.shape[2]
    ts = _pick_tile(length, IN_PROJ_STEPS, 16)
    assert ts + 8 <= XR_PITCH and t_off % ts == 0
    y_blk = (c + 2 * da) // c
    gr_blk = (2 * c + 3 * da) // d
    gn_blk = gr_blk + 1
    row = lambda width, blk=0: pl.BlockSpec((nb, ts, width), lambda i: (0, i, blk))
    whole = lambda a: pl.BlockSpec(a.shape, lambda i: (0,) * a.ndim, pipeline_mode=pl.Buffered(1))
    return pl.pallas_call(
        _merge_kernel,
        grid=(length // ts,),
        in_specs=[row(d), pl.BlockSpec(gate.shape, lambda i: (0, 0, 0)),
                  pl.BlockSpec((ts, nb, c), lambda i: (i + t_off // ts, 0, 0)),
                  row(c, y_blk), row(da), row(d, gr_blk), row(d, gn_blk),
                  whole(w_rnn_o), whole(w_na_o), whole(w_out)],
        out_specs=row(d),
        out_shape=jax.ShapeDtypeStruct((nb, length, d), F32),
        scratch_shapes=[pltpu.VMEM((c // LANES, nb * XR_PITCH, LANES), F32)],
        compiler_params=_params("parallel"),
        name="merge_out_proj",
    )(x3d, gate, hs_tm, z3, na3, z3, z3, w_rnn_o, w_na_o, w_out)


def _ffn_kernel(x_ref, sh_ref, sc_ref, g_ref, w1_ref, w3_ref, w2_ref, o_ref, *, tf):
    x = x_ref[...]
    h = _modulated_norm(x, sh_ref[...], sc_ref[...]).astype(BF16)
    acc = None
    for j in range(w1_ref.shape[1] // tf):
        cols = slice(j * tf, (j + 1) * tf)
        a = jnp.dot(h, w1_ref[:, cols], preferred_element_type=F32)
        b = jnp.dot(h, w3_ref[:, cols], preferred_element_type=F32)
        part = jnp.dot((_silu(a) * b).astype(BF16), w2_ref[cols, :], preferred_element_type=F32)
        acc = part if acc is None else acc + part
    o_ref[...] = x + g_ref[...] * acc


def _dense_ffn(x2d, shift, scale, gate, w1, w3, w2, rows_per_mod):
    m, d = x2d.shape
    dff = w1.shape[1]
    tm = _pick_tile(rows_per_mod, 512, 8)
    tf = _pick_tile(dff, 1536, LANES)
    tiles_per_mod = rows_per_mod // tm
    mod = pl.BlockSpec((None, 1, d), lambda i: (i // tiles_per_mod, 0, 0))
    whole = lambda w: pl.BlockSpec(w.shape, lambda i: (0, 0), pipeline_mode=pl.Buffered(1))
    return pl.pallas_call(
        functools.partial(_ffn_kernel, tf=tf),
        grid=(m // tm,),
        in_specs=[pl.BlockSpec((tm, d), lambda i: (i, 0)), mod, mod, mod, whole(w1), whole(w3), whole(w2)],
        out_specs=pl.BlockSpec((tm, d), lambda i: (i, 0)),
        out_shape=jax.ShapeDtypeStruct((m, d), F32),
        compiler_params=_params("parallel"),
        name="dense_swiglu",
    )(x2d, shift, scale, gate, w1, w3, w2)


SUBLANES = 8
GATHER_DMA_PRIORITY = 1


def _to_token_tiles(ref, x):
    n = x.shape[0]
    for s in range(SUBLANES):
        ref[pl.ds(s, n, stride=SUBLANES), :] = x[:, s * LANES:(s + 1) * LANES]


def _from_token_tiles(ref, n, s):
    return ref[pl.ds(s, n, stride=SUBLANES), :]


def _route_kernel(x_ref, sh_ref, sc_ref, wr_ref, wrl_ref, r_ref, *, n_experts):
    h = _modulated_norm(x_ref[...], sh_ref[...], sc_ref[...])
    h_hi = h.astype(BF16)
    h_lo = (h - h_hi.astype(F32)).astype(BF16)
    logits = (jnp.dot(h_hi, wr_ref[...], preferred_element_type=F32)
              + jnp.dot(h_lo, wr_ref[...], preferred_element_type=F32)
              + jnp.dot(h_hi, wrl_ref[...], preferred_element_type=F32))
    lane = lax.broadcasted_iota(jnp.int32, logits.shape, 1).astype(F32)
    neg = -jnp.inf
    lg = jnp.where(lane < n_experts, logits, neg)
    m1 = jnp.max(lg, axis=-1, keepdims=True)
    i1 = jnp.min(jnp.where(lg == m1, lane, float(LANES)), axis=-1, keepdims=True)
    lg2 = jnp.where(lane == i1, neg, lg)
    m2 = jnp.max(lg2, axis=-1, keepdims=True)
    i2 = jnp.min(jnp.where(lg2 == m2, lane, float(LANES)), axis=-1, keepdims=True)
    e = jnp.exp(m2 - m1)
    w1 = 1.0 / (1.0 + e)
    w2 = e / (1.0 + e)
    r_ref[...] = jnp.where(lane == 0, i1, jnp.where(lane == 1, i2, jnp.where(lane == 2, w1,
                           jnp.where(lane == 3, w2, 0.0))))


def _route(x2d, shift, scale, router, rows_per_mod):
    m, d = x2d.shape
    n_experts = router.shape[1]
    wr32 = jnp.zeros((d, LANES), F32).at[:, :n_experts].set(router)
    wr = wr32.astype(BF16)
    wrl = (wr32 - wr.astype(F32)).astype(BF16)
    tm = _pick_tile(rows_per_mod, 512, 8)
    tiles_per_mod = rows_per_mod // tm
    mod = pl.BlockSpec((None, 1, d), lambda i: (i // tiles_per_mod, 0, 0))
    rspec = pl.BlockSpec((d, LANES), lambda i: (0, 0))
    return pl.pallas_call(
        functools.partial(_route_kernel, n_experts=n_experts),
        grid=(m // tm,),
        in_specs=[pl.BlockSpec((tm, d), lambda i: (i, 0)), mod, mod, rspec, rspec],
        out_specs=pl.BlockSpec((tm, LANES), lambda i: (i, 0)),
        out_shape=jax.ShapeDtypeStruct((m, LANES), F32),
        compiler_params=_params("parallel"),
        name="moe_route",
    )(x2d, shift, scale, wr, wrl)


def _tile_rows(tile):
    start = tile * SUBLANES
    return pl.ds(start if isinstance(start, int) else pl.multiple_of(start, SUBLANES), SUBLANES)


def _tile_copy(src_ref, src_tile, dst_ref, dst_tile, sem):
    return pltpu.make_async_copy(src_ref.at[_tile_rows(src_tile), :], dst_ref.at[_tile_rows(dst_tile), :], sem)


def _dispatch_kernel(pos_ref, pad_ref, x_ref, sh_ref, sc_ref, xs_hbm, hbuf, zbuf, sem, zsem):
    i = pl.program_id(0)
    n = pl.num_programs(0)
    tm = x_ref.shape[0]
    m = n * tm
    pads_per_step = pad_ref.shape[0] // n
    slot = i % 2

    def wait_slot(sl):
        rows = TOP_K * tm * SUBLANES
        pltpu.make_async_copy(xs_hbm.at[pl.ds(0, rows), :], xs_hbm.at[pl.ds(0, rows), :], sem.at[sl]).wait()

    @pl.when(i >= 2)
    def _():
        wait_slot(slot)

    _to_token_tiles(hbuf.at[slot], _modulated_norm(x_ref[...], sh_ref[...], sc_ref[...]))
    for k in range(TOP_K):
        base = k * m + i * tm

        def put(j, carry):
            _tile_copy(hbuf.at[slot], j, xs_hbm, pos_ref[base + j], sem.at[slot]).start(priority=GATHER_DMA_PRIORITY)
            return carry

        lax.fori_loop(0, tm, put, 0, unroll=8)

    @pl.when(i == 0)
    def _():
        zbuf[...] = jnp.zeros_like(zbuf)

    def clear(q, carry):
        _tile_copy(zbuf, 0, xs_hbm, pad_ref[i * pads_per_step + q], zsem).start(priority=GATHER_DMA_PRIORITY)
        return carry

    lax.fori_loop(0, pads_per_step, clear, 0, unroll=8)

    @pl.when(i == n - 1)
    def _():
        wait_slot(slot)

        @pl.when(n >= 2)
        def _():
            wait_slot(1 - slot)

        rows = pad_ref.shape[0] * SUBLANES
        pltpu.make_async_copy(xs_hbm.at[pl.ds(0, rows), :], xs_hbm.at[pl.ds(0, rows), :], zsem).wait()


def _dispatch(x2d, shift, scale, pos, pad_rows, n_rows, rows_per_mod):
    m, d = x2d.shape
    assert d == SUBLANES * LANES
    tm = _pick_tile(rows_per_mod, 512, 8)
    tiles_per_mod = rows_per_mod // tm
    assert pad_rows.shape[0] % (m // tm) == 0
    mod = pl.BlockSpec((None, 1, d), lambda i, ps, pd: (i // tiles_per_mod, 0, 0))
    return pl.pallas_call(
        _dispatch_kernel,
        grid_spec=pltpu.PrefetchScalarGridSpec(
            num_scalar_prefetch=2,
            grid=(m // tm,),
            in_specs=[pl.BlockSpec((tm, d), lambda i, ps, pd: (i, 0)), mod, mod],
            out_specs=pl.BlockSpec(memory_space=pl.ANY),
            scratch_shapes=[pltpu.VMEM((2, tm * SUBLANES, LANES), F32), pltpu.VMEM((SUBLANES, LANES), F32),
                            pltpu.SemaphoreType.DMA((2,)), pltpu.SemaphoreType.DMA(())],
        ),
        out_shape=jax.ShapeDtypeStruct((n_rows * SUBLANES, LANES), F32),
        compiler_params=_params("arbitrary"),
        name="moe_dispatch",
    )(pos, pad_rows, x2d, shift, scale)


EXPERT_TILE_ROWS = 1024


def _expert_ffn_kernel(te_ref, nv_ref, x_ref, w1_ref, w3_ref, w2_ref, o_ref, h_scr, acc_scr):
    t = pl.program_id(0)
    f = pl.program_id(1)
    nv = nv_ref[0]
    tm = h_scr.shape[0]

    @pl.when(t < nv)
    def _():
        @pl.when(f == 0)
        def _():
            for s in range(SUBLANES):
                h_scr[:, s * LANES:(s + 1) * LANES] = _from_token_tiles(x_ref, tm, s).astype(BF16)
            acc_scr[...] = jnp.zeros_like(acc_scr)

        h = h_scr[...]
        a = jnp.dot(h, w1_ref[...].astype(BF16), preferred_element_type=F32)
        b = jnp.dot(h, w3_ref[...].astype(BF16), preferred_element_type=F32)
        acc_scr[...] += jnp.dot((_silu(a) * b).astype(BF16), w2_ref[...].astype(BF16),
                                preferred_element_type=F32)

        @pl.when(f == pl.num_programs(1) - 1)
        def _():
            _to_token_tiles(o_ref, acc_scr[...])

    @pl.when(jnp.logical_and(t >= nv, f == pl.num_programs(1) - 1))
    def _():
        o_ref[...] = jnp.zeros_like(o_ref)


def _expert_ffn(xs, tile_expert, n_valid, w1, w3, w2):
    tm = EXPERT_TILE_ROWS
    n = xs.shape[0] // SUBLANES
    d = w1.shape[1]
    dfe = w1.shape[2]
    tf = _pick_tile(dfe, 896, LANES)
    nf = dfe // tf

    def fsel(t, f, nv):
        return jnp.where(t < nv[0], f, nf - 1)

    return pl.pallas_call(
        _expert_ffn_kernel,
        grid_spec=pltpu.PrefetchScalarGridSpec(
            num_scalar_prefetch=2,
            grid=(n // tm, nf),
            in_specs=[pl.BlockSpec((tm * SUBLANES, LANES), lambda t, f, te, nv: (t, 0)),
                      pl.BlockSpec((None, d, tf), lambda t, f, te, nv: (te[t], 0, fsel(t, f, nv))),
                      pl.BlockSpec((None, d, tf), lambda t, f, te, nv: (te[t], 0, fsel(t, f, nv))),
                      pl.BlockSpec((None, tf, d), lambda t, f, te, nv: (te[t], fsel(t, f, nv), 0))],
            out_specs=pl.BlockSpec((tm * SUBLANES, LANES), lambda t, f, te, nv: (t, 0)),
            scratch_shapes=[pltpu.VMEM((tm, d), BF16), pltpu.VMEM((tm, d), F32)],
        ),
        out_shape=jax.ShapeDtypeStruct((n * SUBLANES, LANES), F32),
        compiler_params=_params("arbitrary", "arbitrary"),
        name="expert_swiglu",
    )(tile_expert, n_valid, xs, w1, w3, w2)


def _combine_kernel(pos_ref, x_ref, g_ref, r_ref, ys_hbm, o_ref, ybuf, sem):
    i = pl.program_id(0)
    n = pl.num_programs(0)
    tm = x_ref.shape[0]
    m = n * tm

    def issue_rows(tile, slot):
        for k in range(TOP_K):
            base = k * m + tile * tm

            def body(j, carry):
                src_row = pl.multiple_of(pos_ref[base + j] * SUBLANES, SUBLANES)
                dst_row = pl.multiple_of(j * SUBLANES, SUBLANES)
                pltpu.make_async_copy(ys_hbm.at[pl.ds(src_row, SUBLANES), :],
                                      ybuf.at[slot, k, pl.ds(dst_row, SUBLANES), :],
                                      sem.at[slot]).start(priority=GATHER_DMA_PRIORITY)
                return carry

            lax.fori_loop(0, tm, body, 0, unroll=8)

    slot = i % 2

    @pl.when(i == 0)
    def _():
        issue_rows(0, 0)

    pltpu.make_async_copy(ybuf.at[1 - slot], ybuf.at[slot], sem.at[slot]).wait()

    @pl.when(i + 1 < n)
    def _():
        issue_rows(i + 1, 1 - slot)

    r = r_ref[...]
    w1, w2 = r[:, 2:3], r[:, 3:4]
    for s in range(SUBLANES):
        lanes = slice(s * LANES, (s + 1) * LANES)
        mix = (w1 * _from_token_tiles(ybuf.at[slot, 0], tm, s) + w2 * _from_token_tiles(ybuf.at[slot, 1], tm, s))
        o_ref[:, lanes] = x_ref[:, lanes] + g_ref[:, lanes] * mix


def _combine(x2d, gate, route, ys, pos, rows_per_mod):
    m, d = x2d.shape
    tm = _pick_tile(rows_per_mod, 512, 8)
    tiles_per_mod = rows_per_mod // tm
    return pl.pallas_call(
        _combine_kernel,
        grid_spec=pltpu.PrefetchScalarGridSpec(
            num_scalar_prefetch=1,
            grid=(m // tm,),
            in_specs=[pl.BlockSpec((tm, d), lambda i, ps: (i, 0)),
                      pl.BlockSpec((None, 1, d), lambda i, ps: (i // tiles_per_mod, 0, 0)),
                      pl.BlockSpec((tm, LANES), lambda i, ps: (i, 0)),
                      pl.BlockSpec(memory_space=pl.ANY)],
            out_specs=pl.BlockSpec((tm, d), lambda i, ps: (i, 0)),
            scratch_shapes=[pltpu.VMEM((2, TOP_K, tm * SUBLANES, LANES), F32), pltpu.SemaphoreType.DMA((2,))],
        ),
        out_shape=jax.ShapeDtypeStruct((m, d), F32),
        compiler_params=_params("arbitrary"),
        name="moe_combine",
    )(pos, x2d, gate, route, ys)


def _moe_ffn(x2d, shift, scale, gate, router, w1, w3, w2, rows_per_mod):
    m, d = x2d.shape
    n_experts = router.shape[1]
    route = _route(x2d, shift, scale, router, rows_per_mod)
    tm = EXPERT_TILE_ROWS
    expert = route[:, :TOP_K].astype(jnp.int32).T.reshape(-1)
    onehot = (expert[:, None] == jnp.arange(n_experts)[None, :]).astype(jnp.int32)
    csum = jnp.cumsum(onehot, axis=0)
    rank = jnp.sum(onehot * (csum - 1), axis=1)
    counts = csum[-1]
    tiles = (counts + tm - 1) // tm
    tile_end = jnp.cumsum(tiles)
    first_row = (tile_end - tiles) * tm
    pos = (jnp.sum(onehot * first_row[None, :], axis=1) + rank).astype(jnp.int32)
    n_tiles = (TOP_K * m) // tm + n_experts
    n_rows = n_tiles * tm
    gap_len = jnp.concatenate([tiles * tm - counts, (n_rows - tile_end[-1:] * tm)])
    gap_row = jnp.concatenate([first_row + counts, tile_end[-1:] * tm])
    gap_end = jnp.cumsum(gap_len)
    q = jnp.arange(n_rows - TOP_K * m)
    in_gap = jnp.logical_and(q[:, None] >= (gap_end - gap_len)[None, :], q[:, None] < gap_end[None, :])
    pad_rows = jnp.sum(jnp.where(in_gap, (gap_row - (gap_end - gap_len))[None, :] + q[:, None], 0),
                       axis=1).astype(jnp.int32)
    tile_expert = jnp.minimum(jnp.sum((jnp.arange(n_tiles)[:, None] >= tile_end[None, :]).astype(jnp.int32), axis=1),
                              n_experts - 1).astype(jnp.int32)
    n_valid = tile_end[-1:].astype(jnp.int32)
    xs = _dispatch(x2d, shift, scale, pos, pad_rows, n_rows, rows_per_mod)
    ys = _expert_ffn(xs, tile_expert, n_valid, w1, w3, w2)
    return _combine(x2d, gate, route, ys, pos, rows_per_mod)


def kernel(x, c, ctx, c_ctx, w_mod, b_mod, w_in, conv_w, conv_b, rg_lambda, rg_w, rg_b, q_gain, k_gain, rpb,
           w_rnn_o, w_na_o, w_out, ffn_w1, ffn_w3, ffn_w2, router, moe_w1, moe_w3, moe_w2):
    nb, seq, d = x.shape
    cl = ctx.shape[1]
    depth = w_mod.shape[0]
    c_rnn = conv_w.shape[2]
    d_att = rpb.shape[1] * HEAD_DIM
    ctx_cols = c_rnn + 2 * d_att
    rows = seq // GRID_W

    n_cond = -(-(nb + 1) // 8) * 8
    cond = jnp.zeros((n_cond, d), F32).at[:nb].set(c).at[nb].set(c_ctx)
    mods = _adaln(cond, w_mod, b_mod)

    x2 = x.reshape(nb * seq, d)
    xc2 = ctx.reshape(nb * cl, d)
    for l in range(depth):
        ctx_out = l < depth - 1
        lat = [mods[l, :nb, k * d:(k + 1) * d].reshape(nb, 1, d) for k in range(N_MOD)]
        cmod = [mods[l, nb:nb + 1, k * d:(k + 1) * d].reshape(1, 1, d) for k in range(N_MOD)]
        w_in_l = w_in[l].astype(BF16)
        tile_heads = lambda g: jnp.tile(g.astype(F32), c_rnn // HEAD_DIM).reshape(1, c_rnn)
        head_gains = jnp.stack([tile_heads(k_gain[l]), tile_heads(q_gain[l]) * Q_SCALE])

        z_lat3, xr_lat = _norm_matmul(x2.reshape(nb, seq, d), lat[0], lat[1], w_in_l, head_gains, c_rnn, (1, 4))
        if ctx_out:
            z_ctx3, xr_ctx = _norm_matmul(xc2.reshape(nb, cl, d), cmod[0], cmod[1], w_in_l, head_gains, c_rnn, (1, 4))
        else:
            z_ctx3, xr_ctx = _norm_matmul(xc2.reshape(nb, cl, d), cmod[0], cmod[1], w_in_l[:, :ctx_cols],
                                          head_gains, c_rnn, (1,))

        wg_f, bg_f = _gate_weights(rg_w[l, 0], rg_b[l, 0])
        wg_r, bg_r = _gate_weights(rg_w[l, 1], rg_b[l, 1])
        hs_r, xc = _rnn_scan_conv(xr_ctx, xr_lat, conv_w[l], conv_b[l], wg_r, bg_r, rg_lambda[l, 1], True)
        hs = _rnn_scan_reuse(xc, cl, wg_f, bg_f, rg_lambda[l, 0], False, hs_r)

        bias = _na_bias_table(rpb[l], rows)
        na_lat = _neighbourhood_attention(z_lat3, z_ctx3, bias, seq)

        wr, wn, wo = w_rnn_o[l].astype(BF16), w_na_o[l].astype(BF16), w_out[l].astype(BF16)
        x2 = _merge(x2.reshape(nb, seq, d), lat[2], hs, cl, z_lat3, na_lat, wr, wn, wo).reshape(nb * seq, d)
        if ctx_out:
            na_ctx = _context_attention(z_ctx3, d_att)
            xc2 = _merge(xc2.reshape(nb, cl, d), cmod[2], hs, 0, z_ctx3, na_ctx, wr, wn, wo).reshape(nb * cl, d)

        j = l // 2
        if l % 2 == 0:
            w1, w3, w2 = ffn_w1[j].astype(BF16), ffn_w3[j].astype(BF16), ffn_w2[j].astype(BF16)
            x2 = _dense_ffn(x2, lat[3], lat[4], lat[5], w1, w3, w2, seq)
            if ctx_out:
                xc2 = _dense_ffn(xc2, cmod[3], cmod[4], cmod[5], w1, w3, w2, nb * cl)
        else:
            w1, w3, w2 = moe_w1[j], moe_w3[j], moe_w2[j]
            x2 = _moe_ffn(x2, lat[3], lat[4], lat[5], router[j], w1, w3, w2, seq)
            if ctx_out:
                xc2 = _moe_ffn(xc2, cmod[3], cmod[4], cmod[5], router[j], w1, w3, w2, nb * cl)
    return x2.reshape(nb, seq, d)
```

```python
import functools

import numpy as np
import jax
import jax.numpy as jnp
from jax import lax
from jax.experimental import pallas as pl
from jax.experimental.pallas import tpu as pltpu

F32 = jnp.float32
BF16 = jnp.bfloat16

EPS = 1e-6
N_MOD = 6
GRID_W = 64
WIN_H = 8
WIN_W = 16
HEAD_DIM = 64
CONV_TAPS = 4
CONV_LEFT = 2
LRU_C = 8.0
TOP_K = 2
MASK_VALUE = -1e30
SQRT_FLOOR = 1e-30

LANES = 128
MXU_DIM = 256
VMEM_LIMIT_BYTES = 56 * 1024 * 1024

NA_GROUP_ROWS = 4
NA_KEY_ROWS = 12


def _params(*sem):
    return pltpu.CompilerParams(dimension_semantics=sem, vmem_limit_bytes=VMEM_LIMIT_BYTES)


def _sigmoid(x):
    return 0.5 * (jnp.tanh(0.5 * x) + 1.0)


def _silu(x):
    return x * _sigmoid(x)


def _gelu_tanh(x):
    return 0.5 * x * (1.0 + jnp.tanh(np.sqrt(2.0 / np.pi) * (x + 0.044715 * (x * x * x))))


def _modulated_norm(x, shift, scale):
    ms = jnp.mean(x * x, axis=-1, keepdims=True)
    return x * lax.rsqrt(ms + EPS) * (1.0 + scale) + shift


def _pick_tile(n, cap, mult):
    best = None
    for t in range(mult, min(n, cap) + 1, mult):
        if n % t == 0:
            best = t
    assert best is not None, (n, cap, mult)
    return best


def _adaln_kernel(c_ref, w_ref, b_ref, o_ref):
    s = _silu(c_ref[...]).astype(BF16)
    o_ref[...] = jnp.dot(s, w_ref[...].astype(BF16), preferred_element_type=F32) + b_ref[...]


def _adaln(cond, w_mod, b_mod):
    depth, d, n = w_mod.shape
    r = cond.shape[0]
    tn = _pick_tile(n, 1536, LANES)
    return pl.pallas_call(
        _adaln_kernel,
        grid=(depth, n // tn),
        in_specs=[
            pl.BlockSpec((r, d), lambda l, j: (0, 0)),
            pl.BlockSpec((None, d, tn), lambda l, j: (l, 0, j)),
            pl.BlockSpec((None, 1, tn), lambda l, j: (l, 0, j)),
        ],
        out_specs=pl.BlockSpec((None, r, tn), lambda l, j: (l, 0, j)),
        out_shape=jax.ShapeDtypeStruct((depth, r, n), F32),
        compiler_params=_params("parallel", "parallel"),
        name="adaln",
    )(cond, w_mod, b_mod.reshape(depth, 1, n))


def _head_norm(x, gain, in_a):
    sq = x * x
    sa = jnp.sum(jnp.where(in_a, sq, 0.0), axis=-1, keepdims=True)
    sb = jnp.sum(jnp.where(in_a, 0.0, sq), axis=-1, keepdims=True)
    ms = jnp.where(in_a, sa, sb) * (1.0 / HEAD_DIM)
    return x * lax.rsqrt(ms + EPS) * gain


IN_PROJ_STEPS = 32
XR_PITCH = 40


def _norm_matmul_kernel(x_ref, sh_ref, sc_ref, w_ref, hg_ref, o_ref, xr_ref, t_scr, *, seg, normed):
    nb, ts, d = x_ref.shape
    h = _modulated_norm(x_ref[...], sh_ref[...], sc_ref[...]).reshape(nb * ts, d).astype(BF16)
    in_a = lax.broadcasted_iota(jnp.int32, (1, LANES), 1) < HEAD_DIM
    for j in range(w_ref.shape[1] // seg):
        acc = jnp.dot(h, w_ref[:, j * seg:(j + 1) * seg], preferred_element_type=F32)
        if j == 0:
            for sl in range(seg // LANES):
                for b in range(nb):
                    t_scr[sl, b * XR_PITCH:b * XR_PITCH + ts, :] = acc[b * ts:(b + 1) * ts, sl * LANES:(sl + 1) * LANES]
            for t in range(ts):
                xr_ref[t] = jnp.concatenate([t_scr.at[sl][pl.ds(t, nb, stride=XR_PITCH), :]
                                             for sl in range(seg // LANES)], axis=-1).astype(xr_ref.dtype)
        if j in normed:
            gain = hg_ref[normed.index(j)]
            for sl in range(seg // LANES):
                lanes = slice(sl * LANES, (sl + 1) * LANES)
                o_ref[:, :, j * seg + sl * LANES:j * seg + (sl + 1) * LANES] = _head_norm(
                    acc[:, lanes], gain[:, lanes], in_a).reshape(nb, ts, LANES).astype(o_ref.dtype)
        else:
            o_ref[:, :, j * seg:(j + 1) * seg] = acc.reshape(nb, ts, seg).astype(o_ref.dtype)


def _norm_matmul(x3d, shift, scale, w, head_gains, seg, normed):
    nb, length, d = x3d.shape
    n = w.shape[1]
    ts = _pick_tile(length, IN_PROJ_STEPS, 16)
    assert ts + 8 <= XR_PITCH
    whole = lambda a: pl.BlockSpec(a.shape, lambda i: (0,) * a.ndim)
    return pl.pallas_call(
        functools.partial(_norm_matmul_kernel, seg=seg, normed=normed),
        grid=(length // ts,),
        in_specs=[
            pl.BlockSpec((nb, ts, d), lambda i: (0, i, 0)),
            whole(shift),
            whole(scale),
            pl.BlockSpec((d, n), lambda i: (0, 0), pipeline_mode=pl.Buffered(1)),
            whole(head_gains),
        ],
        out_specs=[pl.BlockSpec((nb, ts, n), lambda i: (0, i, 0)),
                   pl.BlockSpec((ts, nb, seg), lambda i: (i, 0, 0))],
        out_shape=[jax.ShapeDtypeStruct((nb, length, n), BF16), jax.ShapeDtypeStruct((length, nb, seg), BF16)],
        scratch_shapes=[pltpu.VMEM((seg // LANES, nb * XR_PITCH, LANES), F32)],
        compiler_params=_params("parallel"),
        name="norm_in_proj",
    )(x3d, shift, scale, w, head_gains)


def _scan_tile_index(i, n_ctx_tiles, n_tiles, reverse):
    if not reverse:
        return i
    return jnp.where(i < n_ctx_tiles, n_ctx_tiles - 1 - i, n_tiles - 1 - (i - n_ctx_tiles))


def _gates_and_scan(xc, wg_ref, bg_ref, lam_ref, prev_ref, out_ref, a_scr, u_scr, h_scr, reverse):
    tl, nb, c = out_ref.shape

    @pl.when(pl.program_id(0) == 0)
    def _():
        h_scr[...] = jnp.zeros_like(h_scr)

    lam = lam_ref[...]
    log_sig = jnp.minimum(lam, 0.0) - jnp.log(1.0 + jnp.exp(-jnp.abs(lam)))
    k2 = (0.5 * LRU_C * np.log2(np.e)) * log_sig
    for j in range(c // MXU_DIM):
        sl = slice(MXU_DIM * j, MXU_DIM * (j + 1))
        xj = xc[:, sl]
        t = jnp.tanh(jnp.dot(xj.astype(BF16), wg_ref[j], preferred_element_type=F32) + bg_ref[j])
        a = jnp.exp2(k2[:, sl] * t[:, :MXU_DIM] + k2[:, sl])
        v = 1.0 - a * a
        gated_x = (t[:, MXU_DIM:] + 1.0) * (0.5 * xj)
        a_scr[:, sl] = a
        u_scr[:, sl] = (v * lax.rsqrt(jnp.maximum(v, SQRT_FLOOR))) * gated_x

    def step(s, h):
        tt = (tl - 1 - s) if reverse else s
        row = pl.multiple_of(tt * nb, nb)
        h = a_scr[pl.ds(row, nb), :] * h + u_scr[pl.ds(row, nb), :]
        val = h if prev_ref is None else h + prev_ref[tt].astype(F32)
        out_ref[tt] = val.astype(out_ref.dtype)
        return h

    h_scr[...] = lax.fori_loop(0, tl, step, h_scr[...], unroll=4)


def _rnn_conv_kernel(cp_ref, c_ref, cn_ref, lp_ref, l_ref, ln_ref, cw_ref, cb_ref, wg_ref, bg_ref, lam_ref,
                     out_ref, xc_ref, a_scr, u_scr, h_scr, *, n_ctx_tiles, n_tiles, reverse):
    tl, nb, c = c_ref.shape
    ti = _scan_tile_index(pl.program_id(0), n_ctx_tiles, n_tiles, reverse)
    in_ctx = ti < n_ctx_tiles
    pick = lambda ctx_ref, lat_ref: jnp.where(in_ctx, ctx_ref[...], lat_ref[...]).astype(F32)
    at_start = jnp.logical_or(ti == 0, ti == n_ctx_tiles)
    at_end = jnp.logical_or(ti == n_ctx_tiles - 1, ti == n_tiles - 1)
    xp = jnp.where(at_start, 0.0, pick(cp_ref, lp_ref))
    xn = jnp.where(at_end, 0.0, pick(cn_ref, ln_ref))
    xe = jnp.concatenate([xp, pick(c_ref, l_ref), xn], axis=0)
    xc = cb_ref[...] + xe[0:tl] * cw_ref[0:1, :]
    for k in range(1, CONV_TAPS):
        xc = xc + xe[k:k + tl] * cw_ref[k:k + 1, :]
    xc_ref[...] = xc.astype(xc_ref.dtype)
    _gates_and_scan(xc.reshape(tl * nb, c), wg_ref, bg_ref, lam_ref, None, out_ref, a_scr, u_scr, h_scr, reverse)


def _rnn_reuse_kernel(xc_ref, wg_ref, bg_ref, lam_ref, prev_ref, out_ref, a_scr, u_scr, h_scr, *, reverse):
    tl, nb, c = xc_ref.shape
    _gates_and_scan(xc_ref[...].astype(F32).reshape(tl * nb, c), wg_ref, bg_ref, lam_ref, prev_ref, out_ref,
                    a_scr, u_scr, h_scr, reverse)


def _scan_scratch(tl, nb, c):
    return [pltpu.VMEM((tl * nb, c), F32), pltpu.VMEM((tl * nb, c), F32), pltpu.VMEM((nb, c), F32)]


def _rnn_scan_conv(xr_ctx, xr_lat, conv_w, conv_b, wg, bg, lam, reverse):
    n_ctx, nb, c = xr_ctx.shape
    n_lat = xr_lat.shape[0]
    t = n_ctx + n_lat
    tl = _pick_tile(n_ctx, 64, 2)
    assert n_lat % tl == 0 and n_ctx % tl == 0
    n_tiles, n_ctx_tiles = t // tl, n_ctx // tl
    tile = functools.partial(_scan_tile_index, n_ctx_tiles=n_ctx_tiles, n_tiles=n_tiles, reverse=reverse)
    half = tl // CONV_LEFT
    full = lambda shape: pl.BlockSpec(shape, lambda i: (0,) * len(shape))

    def seq_specs(first_tile, length):
        last = length // tl - 1
        local = lambda i: jnp.clip(tile(i) - first_tile, 0, last)
        return [pl.BlockSpec((CONV_LEFT, nb, c), lambda i: (jnp.maximum(local(i) * half - 1, 0), 0, 0)),
                pl.BlockSpec((tl, nb, c), lambda i: (local(i), 0, 0)),
                pl.BlockSpec((1, nb, c), lambda i: (jnp.minimum((local(i) + 1) * tl, length - 1), 0, 0))]

    by_tile = pl.BlockSpec((tl, nb, c), lambda i: (tile(i), 0, 0))
    return pl.pallas_call(
        functools.partial(_rnn_conv_kernel, n_ctx_tiles=n_ctx_tiles, n_tiles=n_tiles, reverse=reverse),
        grid=(n_tiles,),
        in_specs=seq_specs(0, n_ctx) + seq_specs(n_ctx_tiles, n_lat) + [
            full((CONV_TAPS, c)), full((1, c)), full(wg.shape), full(bg.shape), full((1, c))],
        out_specs=[by_tile, by_tile],
        out_shape=[jax.ShapeDtypeStruct((t, nb, c), BF16), jax.ShapeDtypeStruct((t, nb, c), BF16)],
        scratch_shapes=_scan_scratch(tl, nb, c),
        compiler_params=_params("arbitrary"),
        name="rglru_scan_rev" if reverse else "rglru_scan_fwd",
    )(xr_ctx, xr_ctx, xr_ctx, xr_lat, xr_lat, xr_lat, conv_w, conv_b.reshape(1, c), wg, bg, lam.reshape(1, c))


def _rnn_scan_reuse(xc, n_ctx, wg, bg, lam, reverse, prev):
    t, nb, c = xc.shape
    tl = _pick_tile(n_ctx, 64, 2)
    n_tiles, n_ctx_tiles = t // tl, n_ctx // tl
    tile = functools.partial(_scan_tile_index, n_ctx_tiles=n_ctx_tiles, n_tiles=n_tiles, reverse=reverse)
    full = lambda shape: pl.BlockSpec(shape, lambda i: (0,) * len(shape))
    by_tile = pl.BlockSpec((tl, nb, c), lambda i: (tile(i), 0, 0))
    return pl.pallas_call(
        functools.partial(_rnn_reuse_kernel, reverse=reverse),
        grid=(n_tiles,),
        in_specs=[by_tile, full(wg.shape), full(bg.shape), full((1, c)), by_tile],
        out_specs=by_tile,
        out_shape=jax.ShapeDtypeStruct((t, nb, c), BF16),
        scratch_shapes=_scan_scratch(tl, nb, c),
        compiler_params=_params("arbitrary"),
        name="rglru_scan_rev" if reverse else "rglru_scan_fwd",
    )(xc, wg, bg, lam.reshape(1, c), prev)


def _gate_weights(rg_w_d, rg_b_d):
    _, nblk, bw, _ = rg_w_d.shape
    per = MXU_DIM // bw
    ngrp = nblk // per
    w = rg_w_d.reshape(2, ngrp, per, bw, bw)
    eye = jnp.eye(per, dtype=rg_w_d.dtype)
    dense = w[:, :, :, :, None, :] * eye[None, None, :, None, :, None]
    dense = dense.reshape(2, ngrp, MXU_DIM, MXU_DIM)
    wg = jnp.concatenate([dense[0], dense[1]], axis=-1).astype(BF16)
    b = rg_b_d.reshape(2, ngrp, 1, MXU_DIM)
    bg = jnp.concatenate([b[0], b[1]], axis=-1)
    return wg * 0.5, bg * 0.5


Q_SCALE = (HEAD_DIM ** -0.5) * float(np.log2(np.e))


def _dot_nt(a, b):
    return lax.dot_general(a, b, (((1,), (1,)), ((), ())), preferred_element_type=F32)


def _stack_heads(q, in_a):
    zero = jnp.zeros_like(q)
    return jnp.concatenate([jnp.where(in_a, q, zero), jnp.where(in_a, zero, q)], axis=0)


def _na_kernel(q_ref, k_ref, v_ref, kc_ref, vc_ref, bias_ref, o_ref, sa_scr, sb_scr, *, rows):
    cl = kc_ref.shape[0]
    gq = NA_GROUP_ROWS * GRID_W
    gk = NA_KEY_ROWS * GRID_W
    ngrp = rows // NA_GROUP_ROWS
    in_a = lax.broadcasted_iota(jnp.int32, (1, LANES), 1) < HEAD_DIM

    def key_base(g):
        kb = jnp.clip(NA_GROUP_ROWS * g - WIN_H // 2, 0, rows - NA_KEY_ROWS)
        return pl.multiple_of(kb * GRID_W, GRID_W)

    def scores(g, s_scr):
        cls = jnp.where(g == 0, 0, jnp.where(g == ngrp - 1, 2, 1))
        q2 = _stack_heads(q_ref[pl.ds(pl.multiple_of(g * gq, gq), gq), :], in_a)
        s_scr[:, 0:gk] = _dot_nt(q2, k_ref[pl.ds(key_base(g), gk), :]) + bias_ref[cls]
        s_scr[:, gk:gk + cl] = _dot_nt(q2, kc_ref[...])

    def attend(g, s_scr):
        sc = s_scr[...]
        p = jnp.exp2(sc - jnp.max(sc, axis=-1, keepdims=True))
        den = jnp.sum(p, axis=-1, keepdims=True)
        pb = p.astype(BF16)
        o2 = (jnp.dot(pb[:, 0:gk], v_ref[pl.ds(key_base(g), gk), :], preferred_element_type=F32)
              + jnp.dot(pb[:, gk:gk + cl], vc_ref[...], preferred_element_type=F32)) / den
        o_ref[pl.ds(pl.multiple_of(g * gq, gq), gq), :] = jnp.where(in_a, o2[:gq], o2[gq:]).astype(o_ref.dtype)

    scores(0, sa_scr)

    def pair(it, carry):
        g = 2 * it
        scores(g + 1, sb_scr)
        attend(g, sa_scr)
        scores(g + 2, sa_scr)
        attend(g + 1, sb_scr)
        return carry

    lax.fori_loop(0, ngrp // 2 - 1, pair, 0)
    scores(ngrp - 1, sb_scr)
    attend(ngrp - 2, sa_scr)
    attend(ngrp - 1, sb_scr)


def _bias_table_kernel(idx_ref, t_ref, o_ref):
    c = pl.program_id(1)
    left = lax.broadcasted_iota(jnp.int32, (1, LANES), 1) < GRID_W
    per_class = NA_GROUP_ROWS * NA_KEY_ROWS
    for g in range(2):
        for a in range(NA_GROUP_ROWS):
            r0 = (g * NA_GROUP_ROWS + a) * GRID_W
            for j in range(NA_KEY_ROWS // 2):
                o_even = idx_ref[c * per_class + a * NA_KEY_ROWS + 2 * j]
                o_odd = idx_ref[c * per_class + a * NA_KEY_ROWS + 2 * j + 1]
                o_ref[r0:r0 + GRID_W, j * LANES:(j + 1) * LANES] = jnp.where(left, t_ref[g, o_even], t_ref[g, o_odd])


def _na_bias_table(rpb, rows):
    nh, n_ro, n_co = rpb.shape
    ngrp = rows // NA_GROUP_ROWS
    qc = np.arange(GRID_W)[:, None]
    kc = np.arange(GRID_W)[None, :]
    wstart = np.clip(qc - WIN_W // 2, 0, GRID_W - WIN_W)
    col_valid = (kc >= wstart) & (kc < wstart + WIN_W)
    col_onehot = ((kc - qc + WIN_W - 1)[None] == np.arange(n_co)[:, None, None])
    rpb2 = rpb.astype(F32).reshape(nh // 2, 2, n_ro, n_co)
    by_col = jnp.einsum("pgrj,jqk->pgrqk", rpb2, col_onehot.astype(np.float32),
                        precision=lax.Precision.HIGHEST) * np.float32(np.log2(np.e))
    by_col = jnp.where(col_valid, by_col, MASK_VALUE)
    by_col = jnp.concatenate([by_col, jnp.full((nh // 2, 2, 1, GRID_W, GRID_W), MASK_VALUE, F32)], axis=2)
    tables = jnp.concatenate([by_col, by_col], axis=-1)
    idx = np.full((3, NA_GROUP_ROWS, NA_KEY_ROWS), n_ro, np.int32)
    for ci, g in enumerate((0, 1, ngrp - 1)):
        kb = int(np.clip(NA_GROUP_ROWS * g - WIN_H // 2, 0, rows - NA_KEY_ROWS))
        for a in range(NA_GROUP_ROWS):
            r = NA_GROUP_ROWS * g + a
            start = int(np.clip(r - WIN_H // 2, 0, rows - WIN_H))
            for cr in range(NA_KEY_ROWS):
                if start <= kb + cr < start + WIN_H:
                    idx[ci, a, cr] = kb + cr - r + WIN_H - 1
    gq, gk = NA_GROUP_ROWS * GRID_W, NA_KEY_ROWS * GRID_W
    return pl.pallas_call(
        _bias_table_kernel,
        grid_spec=pltpu.PrefetchScalarGridSpec(
            num_scalar_prefetch=1,
            grid=(nh // 2, 3),
            in_specs=[pl.BlockSpec((None,) + tables.shape[1:], lambda p, c, ix: (p, 0, 0, 0, 0))],
            out_specs=pl.BlockSpec((None, None, 2 * gq, gk), lambda p, c, ix: (p, c, 0, 0)),
        ),
        out_shape=jax.ShapeDtypeStruct((nh // 2, 3, 2 * gq, gk), F32),
        compiler_params=_params("parallel", "parallel"),
        name="na_bias_table",
    )(jnp.asarray(idx.reshape(-1)), tables)


def _neighbourhood_attention(z_lat, z_ctx, bias, seq):
    nb, s, _ = z_lat.shape
    cl = z_ctx.shape[1]
    npair = bias.shape[0]
    d_att = npair * LANES
    cb = npair
    rows = s // GRID_W
    assert s % GRID_W == 0 and rows % (2 * NA_GROUP_ROWS) == 0 and rows >= NA_KEY_ROWS + NA_GROUP_ROWS
    s_shape = (2 * NA_GROUP_ROWS * GRID_W, NA_KEY_ROWS * GRID_W + cl)
    lat = lambda seg: pl.BlockSpec((None, s, LANES), lambda p, b: (b, 0, seg * cb + p))
    ctx = lambda seg: pl.BlockSpec((None, cl, LANES), lambda p, b: (b, 0, seg * cb + p))
    return pl.pallas_call(
        functools.partial(_na_kernel, rows=rows),
        grid=(npair, nb),
        in_specs=[lat(4), lat(1), lat(2), ctx(1), ctx(2),
                  pl.BlockSpec((None,) + bias.shape[1:], lambda p, b: (p, 0, 0, 0))],
        out_specs=pl.BlockSpec((None, s, LANES), lambda p, b: (b, 0, p)),
        out_shape=jax.ShapeDtypeStruct((nb, s, d_att), BF16),
        scratch_shapes=[pltpu.VMEM(s_shape, F32), pltpu.VMEM(s_shape, F32)],
        compiler_params=_params("parallel", "parallel"),
        name="neighbourhood_attention",
    )(z_lat, z_lat, z_lat, z_ctx, z_ctx, bias)


def _ctx_attn_kernel(q_ref, k_ref, v_ref, o_ref):
    in_a = lax.broadcasted_iota(jnp.int32, (1, LANES), 1) < HEAD_DIM
    n = q_ref.shape[0]
    sc = _dot_nt(_stack_heads(q_ref[...], in_a), k_ref[...])
    p = jnp.exp2(sc - jnp.max(sc, axis=-1, keepdims=True))
    den = jnp.sum(p, axis=-1, keepdims=True)
    o2 = jnp.dot(p.astype(BF16), v_ref[...], preferred_element_type=F32) / den
    o_ref[...] = jnp.where(in_a, o2[:n], o2[n:]).astype(o_ref.dtype)


def _context_attention(z_ctx, d_att):
    nb, cl, _ = z_ctx.shape
    cb = d_att // LANES
    ctx = lambda seg: pl.BlockSpec((None, cl, LANES), lambda p, b: (b, 0, seg * cb + p))
    return pl.pallas_call(
        _ctx_attn_kernel,
        grid=(cb, nb),
        in_specs=[ctx(4), ctx(1), ctx(2)],
        out_specs=pl.BlockSpec((None, cl, LANES), lambda p, b: (b, 0, p)),
        out_shape=jax.ShapeDtypeStruct((nb, cl, d_att), BF16),
        compiler_params=_params("parallel", "parallel"),
        name="context_attention",
    )(z_ctx, z_ctx, z_ctx)


def _merge_kernel(x_ref, g_ref, hs_ref, y_ref, na_ref, gr_ref, gn_ref, wr_ref, wn_ref, wo_ref, o_ref, t_scr):
    nb, ts, d = x_ref.shape
    c = hs_ref.shape[2]
    rows2d = lambda ref: ref[...].reshape(nb * ts, ref.shape[2])
    for t in range(ts):
        hs_t = hs_ref[t].astype(F32)
        for sl in range(c // LANES):
            t_scr.at[sl][pl.ds(t, nb, stride=XR_PITCH), :] = hs_t[:, sl * LANES:(sl + 1) * LANES]
    hs = jnp.concatenate([jnp.concatenate([t_scr[sl, b * XR_PITCH:b * XR_PITCH + ts, :] for sl in range(c // LANES)],
                                          axis=-1) for b in range(nb)], axis=0)
    y_rnn = (hs * _gelu_tanh(rows2d(y_ref).astype(F32))).astype(BF16)
    t_rnn = jnp.dot(y_rnn, wr_ref[...], preferred_element_type=F32)
    t_na = jnp.dot(rows2d(na_ref), wn_ref[...], preferred_element_type=F32)
    mix = _sigmoid(rows2d(gr_ref).astype(F32)) * t_rnn + _sigmoid(rows2d(gn_ref).astype(F32)) * t_na
    out = jnp.dot(mix.astype(BF16), wo_ref[...], preferred_element_type=F32)
    o_ref[...] = x_ref[...] + g_ref[...] * out.reshape(nb, ts, d)


def _merge(x3d, gate, hs_tm, t_off, z3, na3, w_rnn_o, w_na_o, w_out):
    nb, length, d = x3d.shape
    c = hs_tm.shape[2]
    da = na3.shape[2]
    ts = _pick_tile(length, IN_PROJ_STEPS, 16)
    assert ts + 8 <= XR_PITCH and t_off % ts == 0
    y_blk = (c + 2 * da) // c
    gr_blk = (2 * c + 3 * da) // d
    gn_blk = gr_blk + 1
    row = lambda width, blk=0: pl.BlockSpec((nb, ts, width), lambda i: (0, i, blk))
    whole = lambda a: pl.BlockSpec(a.shape, lambda i: (0,) * a.ndim, pipeline_mode=pl.Buffered(1))
    return pl.pallas_call(
        _merge_kernel,
        grid=(length // ts,),
        in_specs=[row(d), pl.BlockSpec(gate.shape, lambda i: (0, 0, 0)),
                  pl.BlockSpec((ts, nb, c), lambda i: (i + t_off // ts, 0, 0)),
                  row(c, y_blk), row(da), row(d, gr_blk), row(d, gn_blk),
                  whole(w_rnn_o), whole(w_na_o), whole(w_out)],
        out_specs=row(d),
        out_shape=jax.ShapeDtypeStruct((nb, length, d), F32),
        scratch_shapes=[pltpu.VMEM((c // LANES, nb * XR_PITCH, LANES), F32)],
        compiler_params=_params("parallel"),
        name="merge_out_proj",
    )(x3d, gate, hs_tm, z3, na3, z3, z3, w_rnn_o, w_na_o, w_out)


def _ffn_kernel(x_ref, sh_ref, sc_ref, g_ref, w1_ref, w3_ref, w2_ref, o_ref, *, tf):
    x = x_ref[...]
    h = _modulated_norm(x, sh_ref[...], sc_ref[...]).astype(BF16)
    acc = None
    for j in range(w1_ref.shape[1] // tf):
        cols = slice(j * tf, (j + 1) * tf)
        a = jnp.dot(h, w1_ref[:, cols], preferred_element_type=F32)
        b = jnp.dot(h, w3_ref[:, cols], preferred_element_type=F32)
        part = jnp.dot((_silu(a) * b).astype(BF16), w2_ref[cols, :], preferred_element_type=F32)
        acc = part if acc is None else acc + part
    o_ref[...] = x + g_ref[...] * acc


def _dense_ffn(x2d, shift, scale, gate, w1, w3, w2, rows_per_mod):
    m, d = x2d.shape
    dff = w1.shape[1]
    tm = _pick_tile(rows_per_mod, 512, 8)
    tf = _pick_tile(dff, 1536, LANES)
    tiles_per_mod = rows_per_mod // tm
    mod = pl.BlockSpec((None, 1, d), lambda i: (i // tiles_per_mod, 0, 0))
    whole = lambda w: pl.BlockSpec(w.shape, lambda i: (0, 0), pipeline_mode=pl.Buffered(1))
    return pl.pallas_call(
        functools.partial(_ffn_kernel, tf=tf),
        grid=(m // tm,),
        in_specs=[pl.BlockSpec((tm, d), lambda i: (i, 0)), mod, mod, mod, whole(w1), whole(w3), whole(w2)],
        out_specs=pl.BlockSpec((tm, d), lambda i: (i, 0)),
        out_shape=jax.ShapeDtypeStruct((m, d), F32),
        compiler_params=_params("parallel"),
        name="dense_swiglu",
    )(x2d, shift, scale, gate, w1, w3, w2)


SUBLANES = 8


def _to_token_tiles(ref, x):
    n = x.shape[0]
    for s in range(SUBLANES):
        ref[pl.ds(s, n, stride=SUBLANES), :] = x[:, s * LANES:(s + 1) * LANES]


def _from_token_tiles(ref, n, s):
    return ref[pl.ds(s, n, stride=SUBLANES), :]


def _route_kernel(x_ref, sh_ref, sc_ref, wr_ref, wrl_ref, r_ref, *, n_experts):
    h = _modulated_norm(x_ref[...], sh_ref[...], sc_ref[...])
    h_hi = h.astype(BF16)
    h_lo = (h - h_hi.astype(F32)).astype(BF16)
    logits = (jnp.dot(h_hi, wr_ref[...], preferred_element_type=F32)
              + jnp.dot(h_lo, wr_ref[...], preferred_element_type=F32)
              + jnp.dot(h_hi, wrl_ref[...], preferred_element_type=F32))
    lane = lax.broadcasted_iota(jnp.int32, logits.shape, 1).astype(F32)
    neg = -jnp.inf
    lg = jnp.where(lane < n_experts, logits, neg)
    m1 = jnp.max(lg, axis=-1, keepdims=True)
    i1 = jnp.min(jnp.where(lg == m1, lane, float(LANES)), axis=-1, keepdims=True)
    lg2 = jnp.where(lane == i1, neg, lg)
    m2 = jnp.max(lg2, axis=-1, keepdims=True)
    i2 = jnp.min(jnp.where(lg2 == m2, lane, float(LANES)), axis=-1, keepdims=True)
    e = jnp.exp(m2 - m1)
    w1 = 1.0 / (1.0 + e)
    w2 = e / (1.0 + e)
    r_ref[...] = jnp.where(lane == 0, i1, jnp.where(lane == 1, i2, jnp.where(lane == 2, w1,
                           jnp.where(lane == 3, w2, 0.0))))


def _route(x2d, shift, scale, router, rows_per_mod):
    m, d = x2d.shape
    n_experts = router.shape[1]
    wr32 = jnp.zeros((d, LANES), F32).at[:, :n_experts].set(router)
    wr = wr32.astype(BF16)
    wrl = (wr32 - wr.astype(F32)).astype(BF16)
    tm = _pick_tile(rows_per_mod, 512, 8)
    tiles_per_mod = rows_per_mod // tm
    mod = pl.BlockSpec((None, 1, d), lambda i: (i // tiles_per_mod, 0, 0))
    rspec = pl.BlockSpec((d, LANES), lambda i: (0, 0))
    return pl.pallas_call(
        functools.partial(_route_kernel, n_experts=n_experts),
        grid=(m // tm,),
        in_specs=[pl.BlockSpec((tm, d), lambda i: (i, 0)), mod, mod, rspec, rspec],
        out_specs=pl.BlockSpec((tm, LANES), lambda i: (i, 0)),
        out_shape=jax.ShapeDtypeStruct((m, LANES), F32),
        compiler_params=_params("parallel"),
        name="moe_route",
    )(x2d, shift, scale, wr, wrl)


def _tile_rows(tile):
    start = tile * SUBLANES
    return pl.ds(start if isinstance(start, int) else pl.multiple_of(start, SUBLANES), SUBLANES)


def _tile_copy(src_ref, src_tile, dst_ref, dst_tile, sem):
    return pltpu.make_async_copy(src_ref.at[_tile_rows(src_tile), :], dst_ref.at[_tile_rows(dst_tile), :], sem)


def _dispatch_kernel(pos_ref, pad_ref, x_ref, sh_ref, sc_ref, xs_hbm, hbuf, zbuf, sem, zsem):
    i = pl.program_id(0)
    n = pl.num_programs(0)
    tm = x_ref.shape[0]
    m = n * tm
    pads_per_step = pad_ref.shape[0] // n
    slot = i % 2

    def wait_slot(sl):
        rows = TOP_K * tm * SUBLANES
        pltpu.make_async_copy(xs_hbm.at[pl.ds(0, rows), :], xs_hbm.at[pl.ds(0, rows), :], sem.at[sl]).wait()

    @pl.when(i >= 2)
    def _():
        wait_slot(slot)

    _to_token_tiles(hbuf.at[slot], _modulated_norm(x_ref[...], sh_ref[...], sc_ref[...]))
    for k in range(TOP_K):
        base = k * m + i * tm

        def put(j, carry):
            _tile_copy(hbuf.at[slot], j, xs_hbm, pos_ref[base + j], sem.at[slot]).start()
            return carry

        lax.fori_loop(0, tm, put, 0, unroll=8)

    @pl.when(i == 0)
    def _():
        zbuf[...] = jnp.zeros_like(zbuf)

    def clear(q, carry):
        _tile_copy(zbuf, 0, xs_hbm, pad_ref[i * pads_per_step + q], zsem).start()
        return carry

    lax.fori_loop(0, pads_per_step, clear, 0, unroll=8)

    @pl.when(i == n - 1)
    def _():
        wait_slot(slot)

        @pl.when(n >= 2)
        def _():
            wait_slot(1 - slot)

        rows = pad_ref.shape[0] * SUBLANES
        pltpu.make_async_copy(xs_hbm.at[pl.ds(0, rows), :], xs_hbm.at[pl.ds(0, rows), :], zsem).wait()


def _dispatch(x2d, shift, scale, pos, pad_rows, n_rows, rows_per_mod):
    m, d = x2d.shape
    assert d == SUBLANES * LANES
    tm = _pick_tile(rows_per_mod, 512, 8)
    tiles_per_mod = rows_per_mod // tm
    assert pad_rows.shape[0] % (m // tm) == 0
    mod = pl.BlockSpec((None, 1, d), lambda i, ps, pd: (i // tiles_per_mod, 0, 0))
    return pl.pallas_call(
        _dispatch_kernel,
        grid_spec=pltpu.PrefetchScalarGridSpec(
            num_scalar_prefetch=2,
            grid=(m // tm,),
            in_specs=[pl.BlockSpec((tm, d), lambda i, ps, pd: (i, 0)), mod, mod],
            out_specs=pl.BlockSpec(memory_space=pl.ANY),
            scratch_shapes=[pltpu.VMEM((2, tm * SUBLANES, LANES), F32), pltpu.VMEM((SUBLANES, LANES), F32),
                            pltpu.SemaphoreType.DMA((2,)), pltpu.SemaphoreType.DMA(())],
        ),
        out_shape=jax.ShapeDtypeStruct((n_rows * SUBLANES, LANES), F32),
        compiler_params=_params("arbitrary"),
        name="moe_dispatch",
    )(pos, pad_rows, x2d, shift, scale)


EXPERT_TILE_ROWS = 1024


def _expert_ffn_kernel(te_ref, nv_ref, x_ref, w1_ref, w3_ref, w2_ref, o_ref, h_scr, acc_scr):
    t = pl.program_id(0)
    f = pl.program_id(1)
    nv = nv_ref[0]
    tm = h_scr.shape[0]

    @pl.when(t < nv)
    def _():
        @pl.when(f == 0)
        def _():
            for s in range(SUBLANES):
                h_scr[:, s * LANES:(s + 1) * LANES] = _from_token_tiles(x_ref, tm, s).astype(BF16)
            acc_scr[...] = jnp.zeros_like(acc_scr)

        h = h_scr[...]
        a = jnp.dot(h, w1_ref[...].astype(BF16), preferred_element_type=F32)
        b = jnp.dot(h, w3_ref[...].astype(BF16), preferred_element_type=F32)
        acc_scr[...] += jnp.dot((_silu(a) * b).astype(BF16), w2_ref[...].astype(BF16),
                                preferred_element_type=F32)

        @pl.when(f == pl.num_programs(1) - 1)
        def _():
            _to_token_tiles(o_ref, acc_scr[...])

    @pl.when(jnp.logical_and(t >= nv, f == pl.num_programs(1) - 1))
    def _():
        o_ref[...] = jnp.zeros_like(o_ref)


def _expert_ffn(xs, tile_expert, n_valid, w1, w3, w2):
    tm = EXPERT_TILE_ROWS
    n = xs.shape[0] // SUBLANES
    d = w1.shape[1]
    dfe = w1.shape[2]
    tf = _pick_tile(dfe, 512, LANES)
    nf = dfe // tf

    def fsel(t, f, nv):
        return jnp.where(t < nv[0], f, nf - 1)

    return pl.pallas_call(
        _expert_ffn_kernel,
        grid_spec=pltpu.PrefetchScalarGridSpec(
            num_scalar_prefetch=2,
            grid=(n // tm, nf),
            in_specs=[pl.BlockSpec((tm * SUBLANES, LANES), lambda t, f, te, nv: (t, 0)),
                      pl.BlockSpec((None, d, tf), lambda t, f, te, nv: (te[t], 0, fsel(t, f, nv))),
                      pl.BlockSpec((None, d, tf), lambda t, f, te, nv: (te[t], 0, fsel(t, f, nv))),
                      pl.BlockSpec((None, tf, d), lambda t, f, te, nv: (te[t], fsel(t, f, nv), 0))],
            out_specs=pl.BlockSpec((tm * SUBLANES, LANES), lambda t, f, te, nv: (t, 0)),
            scratch_shapes=[pltpu.VMEM((tm, d), BF16), pltpu.VMEM((tm, d), F32)],
        ),
        out_shape=jax.ShapeDtypeStruct((n * SUBLANES, LANES), F32),
        compiler_params=_params("arbitrary", "arbitrary"),
        name="expert_swiglu",
    )(tile_expert, n_valid, xs, w1, w3, w2)


def _combine_kernel(pos_ref, x_ref, g_ref, r_ref, ys_hbm, o_ref, ybuf, sem):
    i = pl.program_id(0)
    n = pl.num_programs(0)
    tm = x_ref.shape[0]
    m = n * tm

    def issue_rows(tile, slot):
        for k in range(TOP_K):
            base = k * m + tile * tm

            def body(j, carry):
                _tile_copy(ys_hbm, pos_ref[base + j], ybuf.at[slot, k], j, sem.at[slot]).start()
                return carry

            lax.fori_loop(0, tm, body, 0, unroll=8)

    slot = i % 2

    @pl.when(i == 0)
    def _():
        issue_rows(0, 0)

    pltpu.make_async_copy(ybuf.at[1 - slot], ybuf.at[slot], sem.at[slot]).wait()

    @pl.when(i + 1 < n)
    def _():
        issue_rows(i + 1, 1 - slot)

    r = r_ref[...]
    w1, w2 = r[:, 2:3], r[:, 3:4]
    for s in range(SUBLANES):
        lanes = slice(s * LANES, (s + 1) * LANES)
        mix = (w1 * _from_token_tiles(ybuf.at[slot, 0], tm, s) + w2 * _from_token_tiles(ybuf.at[slot, 1], tm, s))
        o_ref[:, lanes] = x_ref[:, lanes] + g_ref[:, lanes] * mix


def _combine(x2d, gate, route, ys, pos, rows_per_mod):
    m, d = x2d.shape
    tm = _pick_tile(rows_per_mod, 512, 8)
    tiles_per_mod = rows_per_mod // tm
    return pl.pallas_call(
        _combine_kernel,
        grid_spec=pltpu.PrefetchScalarGridSpec(
            num_scalar_prefetch=1,
            grid=(m // tm,),
            in_specs=[pl.BlockSpec((tm, d), lambda i, ps: (i, 0)),
                      pl.BlockSpec((None, 1, d), lambda i, ps: (i // tiles_per_mod, 0, 0)),
                      pl.BlockSpec((tm, LANES), lambda i, ps: (i, 0)),
                      pl.BlockSpec(memory_space=pl.ANY)],
            out_specs=pl.BlockSpec((tm, d), lambda i, ps: (i, 0)),
            scratch_shapes=[pltpu.VMEM((2, TOP_K, tm * SUBLANES, LANES), F32), pltpu.SemaphoreType.DMA((2,))],
        ),
        out_shape=jax.ShapeDtypeStruct((m, d), F32),
        compiler_params=_params("arbitrary"),
        name="moe_combine",
    )(pos, x2d, gate, route, ys)


def _moe_ffn(x2d, shift, scale, gate, router, w1, w3, w2, rows_per_mod):
    m, d = x2d.shape
    n_experts = router.shape[1]
    route = _route(x2d, shift, scale, router, rows_per_mod)
    tm = EXPERT_TILE_ROWS
    expert = route[:, :TOP_K].astype(jnp.int32).T.reshape(-1)
    onehot = (expert[:, None] == jnp.arange(n_experts)[None, :]).astype(jnp.int32)
    csum = jnp.cumsum(onehot, axis=0)
    rank = jnp.sum(onehot * (csum - 1), axis=1)
    counts = csum[-1]
    tiles = (counts + tm - 1) // tm
    tile_end = jnp.cumsum(tiles)
    first_row = (tile_end - tiles) * tm
    pos = (jnp.sum(onehot * first_row[None, :], axis=1) + rank).astype(jnp.int32)
    n_tiles = (TOP_K * m) // tm + n_experts
    n_rows = n_tiles * tm
    gap_len = jnp.concatenate([tiles * tm - counts, (n_rows - tile_end[-1:] * tm)])
    gap_row = jnp.concatenate([first_row + counts, tile_end[-1:] * tm])
    gap_end = jnp.cumsum(gap_len)
    q = jnp.arange(n_rows - TOP_K * m)
    in_gap = jnp.logical_and(q[:, None] >= (gap_end - gap_len)[None, :], q[:, None] < gap_end[None, :])
    pad_rows = jnp.sum(jnp.where(in_gap, (gap_row - (gap_end - gap_len))[None, :] + q[:, None], 0),
                       axis=1).astype(jnp.int32)
    tile_expert = jnp.minimum(jnp.sum((jnp.arange(n_tiles)[:, None] >= tile_end[None, :]).astype(jnp.int32), axis=1),
                              n_experts - 1).astype(jnp.int32)
    n_valid = tile_end[-1:].astype(jnp.int32)
    xs = _dispatch(x2d, shift, scale, pos, pad_rows, n_rows, rows_per_mod)
    ys = _expert_ffn(xs, tile_expert, n_valid, w1, w3, w2)
    return _combine(x2d, gate, route, ys, pos, rows_per_mod)


def kernel(x, c, ctx, c_ctx, w_mod, b_mod, w_in, conv_w, conv_b, rg_lambda, rg_w, rg_b, q_gain, k_gain, rpb,
           w_rnn_o, w_na_o, w_out, ffn_w1, ffn_w3, ffn_w2, router, moe_w1, moe_w3, moe_w2):
    nb, seq, d = x.shape
    cl = ctx.shape[1]
    depth = w_mod.shape[0]
    c_rnn = conv_w.shape[2]
    d_att = rpb.shape[1] * HEAD_DIM
    ctx_cols = c_rnn + 2 * d_att
    rows = seq // GRID_W

    n_cond = -(-(nb + 1) // 8) * 8
    cond = jnp.zeros((n_cond, d), F32).at[:nb].set(c).at[nb].set(c_ctx)
    mods = _adaln(cond, w_mod, b_mod)

    x2 = x.reshape(nb * seq, d)
    xc2 = ctx.reshape(nb * cl, d)
    for l in range(depth):
        ctx_out = l < depth - 1
        lat = [mods[l, :nb, k * d:(k + 1) * d].reshape(nb, 1, d) for k in range(N_MOD)]
        cmod = [mods[l, nb:nb + 1, k * d:(k + 1) * d].reshape(1, 1, d) for k in range(N_MOD)]
        w_in_l = w_in[l].astype(BF16)
        tile_heads = lambda g: jnp.tile(g.astype(F32), c_rnn // HEAD_DIM).reshape(1, c_rnn)
        head_gains = jnp.stack([tile_heads(k_gain[l]), tile_heads(q_gain[l]) * Q_SCALE])

        z_lat3, xr_lat = _norm_matmul(x2.reshape(nb, seq, d), lat[0], lat[1], w_in_l, head_gains, c_rnn, (1, 4))
        if ctx_out:
            z_ctx3, xr_ctx = _norm_matmul(xc2.reshape(nb, cl, d), cmod[0], cmod[1], w_in_l, head_gains, c_rnn, (1, 4))
        else:
            z_ctx3, xr_ctx = _norm_matmul(xc2.reshape(nb, cl, d), cmod[0], cmod[1], w_in_l[:, :ctx_cols],
                                          head_gains, c_rnn, (1,))

        wg_f, bg_f = _gate_weights(rg_w[l, 0], rg_b[l, 0])
        wg_r, bg_r = _gate_weights(rg_w[l, 1], rg_b[l, 1])
        hs_r, xc = _rnn_scan_conv(xr_ctx, xr_lat, conv_w[l], conv_b[l], wg_r, bg_r, rg_lambda[l, 1], True)
        hs = _rnn_scan_reuse(xc, cl, wg_f, bg_f, rg_lambda[l, 0], False, hs_r)

        bias = _na_bias_table(rpb[l], rows)
        na_lat = _neighbourhood_attention(z_lat3, z_ctx3, bias, seq)

        wr, wn, wo = w_rnn_o[l].astype(BF16), w_na_o[l].astype(BF16), w_out[l].astype(BF16)
        x2 = _merge(x2.reshape(nb, seq, d), lat[2], hs, cl, z_lat3, na_lat, wr, wn, wo).reshape(nb * seq, d)
        if ctx_out:
            na_ctx = _context_attention(z_ctx3, d_att)
            xc2 = _merge(xc2.reshape(nb, cl, d), cmod[2], hs, 0, z_ctx3, na_ctx, wr, wn, wo).reshape(nb * cl, d)

        j = l // 2
        if l % 2 == 0:
            w1, w3, w2 = ffn_w1[j].astype(BF16), ffn_w3[j].astype(BF16), ffn_w2[j].astype(BF16)
            x2 = _dense_ffn(x2, lat[3], lat[4], lat[5], w1, w3, w2, seq)
            if ctx_out:
                xc2 = _dense_ffn(xc2, cmod[3], cmod[4], cmod[5], w1, w3, w2, nb * cl)
        else:
            w1, w3, w2 = moe_w1[j], moe_w3[j], moe_w2[j]
            x2 = _moe_ffn(x2, lat[3], lat[4], lat[5], router[j], w1, w3, w2, seq)
            if ctx_out:
                xc2 = _moe_ffn(xc2, cmod[3], cmod[4], cmod[5], router[j], w1, w3, w2, nb * cl)
    return x2.reshape(nb, seq, d)
```

```python
import functools

import numpy as np
import jax
import jax.numpy as jnp
from jax import lax
from jax.experimental import pallas as pl
from jax.experimental.pallas import tpu as pltpu

F32 = jnp.float32
BF16 = jnp.bfloat16

EPS = 1e-6
N_MOD = 6
GRID_W = 64
WIN_H = 8
WIN_W = 16
HEAD_DIM = 64
CONV_TAPS = 4
CONV_LEFT = 2
LRU_C = 8.0
TOP_K = 2
MASK_VALUE = -1e30
SQRT_FLOOR = 1e-30

LANES = 128
MXU_DIM = 256
VMEM_LIMIT_BYTES = 56 * 1024 * 1024

NA_GROUP_ROWS = 4
NA_KEY_ROWS = 12


def _params(*sem):
    return pltpu.CompilerParams(dimension_semantics=sem, vmem_limit_bytes=VMEM_LIMIT_BYTES)


def _sigmoid(x):
    return 0.5 * (jnp.tanh(0.5 * x) + 1.0)


def _silu(x):
    return x * _sigmoid(x)


def _gelu_tanh(x):
    return 0.5 * x * (1.0 + jnp.tanh(np.sqrt(2.0 / np.pi) * (x + 0.044715 * (x * x * x))))


def _modulated_norm(x, shift, scale):
    ms = jnp.mean(x * x, axis=-1, keepdims=True)
    return x * lax.rsqrt(ms + EPS) * (1.0 + scale) + shift


def _pick_tile(n, cap, mult):
    best = None
    for t in range(mult, min(n, cap) + 1, mult):
        if n % t == 0:
            best = t
    assert best is not None, (n, cap, mult)
    return best


def _adaln_kernel(c_ref, w_ref, b_ref, o_ref):
    s = _silu(c_ref[...]).astype(BF16)
    o_ref[...] = jnp.dot(s, w_ref[...].astype(BF16), preferred_element_type=F32) + b_ref[...]


def _adaln(cond, w_mod, b_mod):
    depth, d, n = w_mod.shape
    r = cond.shape[0]
    tn = _pick_tile(n, 1536, LANES)
    return pl.pallas_call(
        _adaln_kernel,
        grid=(depth, n // tn),
        in_specs=[
            pl.BlockSpec((r, d), lambda l, j: (0, 0)),
            pl.BlockSpec((None, d, tn), lambda l, j: (l, 0, j)),
            pl.BlockSpec((None, 1, tn), lambda l, j: (l, 0, j)),
        ],
        out_specs=pl.BlockSpec((None, r, tn), lambda l, j: (l, 0, j)),
        out_shape=jax.ShapeDtypeStruct((depth, r, n), F32),
        compiler_params=_params("parallel", "parallel"),
        name="adaln",
    )(cond, w_mod, b_mod.reshape(depth, 1, n))


def _head_norm(x, gain, in_a):
    sq = x * x
    sa = jnp.sum(jnp.where(in_a, sq, 0.0), axis=-1, keepdims=True)
    sb = jnp.sum(jnp.where(in_a, 0.0, sq), axis=-1, keepdims=True)
    ms = jnp.where(in_a, sa, sb) * (1.0 / HEAD_DIM)
    return x * lax.rsqrt(ms + EPS) * gain


IN_PROJ_STEPS = 32
XR_PITCH = 40


def _norm_matmul_kernel(x_ref, sh_ref, sc_ref, w_ref, hg_ref, o_ref, xr_ref, t_scr, *, seg, normed):
    nb, ts, d = x_ref.shape
    h = _modulated_norm(x_ref[...], sh_ref[...], sc_ref[...]).reshape(nb * ts, d).astype(BF16)
    in_a = lax.broadcasted_iota(jnp.int32, (1, LANES), 1) < HEAD_DIM
    for j in range(w_ref.shape[1] // seg):
        acc = jnp.dot(h, w_ref[:, j * seg:(j + 1) * seg], preferred_element_type=F32)
        if j == 0:
            for sl in range(seg // LANES):
                for b in range(nb):
                    t_scr[sl, b * XR_PITCH:b * XR_PITCH + ts, :] = acc[b * ts:(b + 1) * ts, sl * LANES:(sl + 1) * LANES]
            for t in range(ts):
                xr_ref[t] = jnp.concatenate([t_scr.at[sl][pl.ds(t, nb, stride=XR_PITCH), :]
                                             for sl in range(seg // LANES)], axis=-1).astype(xr_ref.dtype)
        if j in normed:
            gain = hg_ref[normed.index(j)]
            for sl in range(seg // LANES):
                lanes = slice(sl * LANES, (sl + 1) * LANES)
                o_ref[:, :, j * seg + sl * LANES:j * seg + (sl + 1) * LANES] = _head_norm(
                    acc[:, lanes], gain[:, lanes], in_a).reshape(nb, ts, LANES).astype(o_ref.dtype)
        else:
            o_ref[:, :, j * seg:(j + 1) * seg] = acc.reshape(nb, ts, seg).astype(o_ref.dtype)


def _norm_matmul(x3d, shift, scale, w, head_gains, seg, normed):
    nb, length, d = x3d.shape
    n = w.shape[1]
    ts = _pick_tile(length, IN_PROJ_STEPS, 16)
    assert ts + 8 <= XR_PITCH
    whole = lambda a: pl.BlockSpec(a.shape, lambda i: (0,) * a.ndim)
    return pl.pallas_call(
        functools.partial(_norm_matmul_kernel, seg=seg, normed=normed),
        grid=(length // ts,),
        in_specs=[
            pl.BlockSpec((nb, ts, d), lambda i: (0, i, 0)),
            whole(shift),
            whole(scale),
            pl.BlockSpec((d, n), lambda i: (0, 0), pipeline_mode=pl.Buffered(1)),
            whole(head_gains),
        ],
        out_specs=[pl.BlockSpec((nb, ts, n), lambda i: (0, i, 0)),
                   pl.BlockSpec((ts, nb, seg), lambda i: (i, 0, 0))],
        out_shape=[jax.ShapeDtypeStruct((nb, length, n), BF16), jax.ShapeDtypeStruct((length, nb, seg), BF16)],
        scratch_shapes=[pltpu.VMEM((seg // LANES, nb * XR_PITCH, LANES), F32)],
        compiler_params=_params("parallel"),
        name="norm_in_proj",
    )(x3d, shift, scale, w, head_gains)


def _scan_tile_index(i, n_ctx_tiles, n_tiles, reverse):
    if not reverse:
        return i
    return jnp.where(i < n_ctx_tiles, n_ctx_tiles - 1 - i, n_tiles - 1 - (i - n_ctx_tiles))


def _gates_and_scan(xc, wg_ref, bg_ref, lam_ref, prev_ref, out_ref, a_scr, u_scr, h_scr, reverse):
    tl, nb, c = out_ref.shape

    @pl.when(pl.program_id(0) == 0)
    def _():
        h_scr[...] = jnp.zeros_like(h_scr)

    lam = lam_ref[...]
    log_sig = jnp.minimum(lam, 0.0) - jnp.log(1.0 + jnp.exp(-jnp.abs(lam)))
    k2 = (0.5 * LRU_C * np.log2(np.e)) * log_sig
    for j in range(c // MXU_DIM):
        sl = slice(MXU_DIM * j, MXU_DIM * (j + 1))
        xj = xc[:, sl]
        t = jnp.tanh(jnp.dot(xj.astype(BF16), wg_ref[j], preferred_element_type=F32) + bg_ref[j])
        a = jnp.exp2(k2[:, sl] * t[:, :MXU_DIM] + k2[:, sl])
        v = 1.0 - a * a
        gated_x = (t[:, MXU_DIM:] + 1.0) * (0.5 * xj)
        a_scr[:, sl] = a
        u_scr[:, sl] = (v * lax.rsqrt(jnp.maximum(v, SQRT_FLOOR))) * gated_x

    def step(s, h):
        tt = (tl - 1 - s) if reverse else s
        row = pl.multiple_of(tt * nb, nb)
        h = a_scr[pl.ds(row, nb), :] * h + u_scr[pl.ds(row, nb), :]
        val = h if prev_ref is None else h + prev_ref[tt].astype(F32)
        out_ref[tt] = val.astype(out_ref.dtype)
        return h

    h_scr[...] = lax.fori_loop(0, tl, step, h_scr[...], unroll=4)


def _rnn_conv_kernel(cp_ref, c_ref, cn_ref, lp_ref, l_ref, ln_ref, cw_ref, cb_ref, wg_ref, bg_ref, lam_ref,
                     out_ref, xc_ref, a_scr, u_scr, h_scr, *, n_ctx_tiles, n_tiles, reverse):
    tl, nb, c = c_ref.shape
    ti = _scan_tile_index(pl.program_id(0), n_ctx_tiles, n_tiles, reverse)
    in_ctx = ti < n_ctx_tiles
    pick = lambda ctx_ref, lat_ref: jnp.where(in_ctx, ctx_ref[...], lat_ref[...]).astype(F32)
    at_start = jnp.logical_or(ti == 0, ti == n_ctx_tiles)
    at_end = jnp.logical_or(ti == n_ctx_tiles - 1, ti == n_tiles - 1)
    xp = jnp.where(at_start, 0.0, pick(cp_ref, lp_ref))
    xn = jnp.where(at_end, 0.0, pick(cn_ref, ln_ref))
    xe = jnp.concatenate([xp, pick(c_ref, l_ref), xn], axis=0)
    xc = cb_ref[...] + xe[0:tl] * cw_ref[0:1, :]
    for k in range(1, CONV_TAPS):
        xc = xc + xe[k:k + tl] * cw_ref[k:k + 1, :]
    xc_ref[...] = xc.astype(xc_ref.dtype)
    _gates_and_scan(xc.reshape(tl * nb, c), wg_ref, bg_ref, lam_ref, None, out_ref, a_scr, u_scr, h_scr, reverse)


def _rnn_reuse_kernel(xc_ref, wg_ref, bg_ref, lam_ref, prev_ref, out_ref, a_scr, u_scr, h_scr, *, reverse):
    tl, nb, c = xc_ref.shape
    _gates_and_scan(xc_ref[...].astype(F32).reshape(tl * nb, c), wg_ref, bg_ref, lam_ref, prev_ref, out_ref,
                    a_scr, u_scr, h_scr, reverse)


def _scan_scratch(tl, nb, c):
    return [pltpu.VMEM((tl * nb, c), F32), pltpu.VMEM((tl * nb, c), F32), pltpu.VMEM((nb, c), F32)]


def _rnn_scan_conv(xr_ctx, xr_lat, conv_w, conv_b, wg, bg, lam, reverse):
    n_ctx, nb, c = xr_ctx.shape
    n_lat = xr_lat.shape[0]
    t = n_ctx + n_lat
    tl = _pick_tile(n_ctx, 64, 2)
    assert n_lat % tl == 0 and n_ctx % tl == 0
    n_tiles, n_ctx_tiles = t // tl, n_ctx // tl
    tile = functools.partial(_scan_tile_index, n_ctx_tiles=n_ctx_tiles, n_tiles=n_tiles, reverse=reverse)
    half = tl // CONV_LEFT
    full = lambda shape: pl.BlockSpec(shape, lambda i: (0,) * len(shape))

    def seq_specs(first_tile, length):
        last = length // tl - 1
        local = lambda i: jnp.clip(tile(i) - first_tile, 0, last)
        return [pl.BlockSpec((CONV_LEFT, nb, c), lambda i: (jnp.maximum(local(i) * half - 1, 0), 0, 0)),
                pl.BlockSpec((tl, nb, c), lambda i: (local(i), 0, 0)),
                pl.BlockSpec((1, nb, c), lambda i: (jnp.minimum((local(i) + 1) * tl, length - 1), 0, 0))]

    by_tile = pl.BlockSpec((tl, nb, c), lambda i: (tile(i), 0, 0))
    return pl.pallas_call(
        functools.partial(_rnn_conv_kernel, n_ctx_tiles=n_ctx_tiles, n_tiles=n_tiles, reverse=reverse),
        grid=(n_tiles,),
        in_specs=seq_specs(0, n_ctx) + seq_specs(n_ctx_tiles, n_lat) + [
            full((CONV_TAPS, c)), full((1, c)), full(wg.shape), full(bg.shape), full((1, c))],
        out_specs=[by_tile, by_tile],
        out_shape=[jax.ShapeDtypeStruct((t, nb, c), BF16), jax.ShapeDtypeStruct((t, nb, c), BF16)],
        scratch_shapes=_scan_scratch(tl, nb, c),
        compiler_params=_params("arbitrary"),
        name="rglru_scan_rev" if reverse else "rglru_scan_fwd",
    )(xr_ctx, xr_ctx, xr_ctx, xr_lat, xr_lat, xr_lat, conv_w, conv_b.reshape(1, c), wg, bg, lam.reshape(1, c))


def _rnn_scan_reuse(xc, n_ctx, wg, bg, lam, reverse, prev):
    t, nb, c = xc.shape
    tl = _pick_tile(n_ctx, 64, 2)
    n_tiles, n_ctx_tiles = t // tl, n_ctx // tl
    tile = functools.partial(_scan_tile_index, n_ctx_tiles=n_ctx_tiles, n_tiles=n_tiles, reverse=reverse)
    full = lambda shape: pl.BlockSpec(shape, lambda i: (0,) * len(shape))
    by_tile = pl.BlockSpec((tl, nb, c), lambda i: (tile(i), 0, 0))
    return pl.pallas_call(
        functools.partial(_rnn_reuse_kernel, reverse=reverse),
        grid=(n_tiles,),
        in_specs=[by_tile, full(wg.shape), full(bg.shape), full((1, c)), by_tile],
        out_specs=by_tile,
        out_shape=jax.ShapeDtypeStruct((t, nb, c), BF16),
        scratch_shapes=_scan_scratch(tl, nb, c),
        compiler_params=_params("arbitrary"),
        name="rglru_scan_rev" if reverse else "rglru_scan_fwd",
    )(xc, wg, bg, lam.reshape(1, c), prev)


def _gate_weights(rg_w_d, rg_b_d):
    _, nblk, bw, _ = rg_w_d.shape
    per = MXU_DIM // bw
    ngrp = nblk // per
    w = rg_w_d.reshape(2, ngrp, per, bw, bw)
    eye = jnp.eye(per, dtype=rg_w_d.dtype)
    dense = w[:, :, :, :, None, :] * eye[None, None, :, None, :, None]
    dense = dense.reshape(2, ngrp, MXU_DIM, MXU_DIM)
    wg = jnp.concatenate([dense[0], dense[1]], axis=-1).astype(BF16)
    b = rg_b_d.reshape(2, ngrp, 1, MXU_DIM)
    bg = jnp.concatenate([b[0], b[1]], axis=-1)
    return wg * 0.5, bg * 0.5


Q_SCALE = (HEAD_DIM ** -0.5) * float(np.log2(np.e))


def _dot_nt(a, b):
    return lax.dot_general(a, b, (((1,), (1,)), ((), ())), preferred_element_type=F32)


def _stack_heads(q, in_a):
    zero = jnp.zeros_like(q)
    return jnp.concatenate([jnp.where(in_a, q, zero), jnp.where(in_a, zero, q)], axis=0)


def _na_kernel(q_ref, k_ref, v_ref, kc_ref, vc_ref, bias_ref, o_ref, sa_scr, sb_scr, *, rows):
    cl = kc_ref.shape[0]
    gq = NA_GROUP_ROWS * GRID_W
    gk = NA_KEY_ROWS * GRID_W
    ngrp = rows // NA_GROUP_ROWS
    in_a = lax.broadcasted_iota(jnp.int32, (1, LANES), 1) < HEAD_DIM

    def key_base(g):
        kb = jnp.clip(NA_GROUP_ROWS * g - WIN_H // 2, 0, rows - NA_KEY_ROWS)
        return pl.multiple_of(kb * GRID_W, GRID_W)

    def scores(g, s_scr):
        cls = jnp.where(g == 0, 0, jnp.where(g == ngrp - 1, 2, 1))
        q2 = _stack_heads(q_ref[pl.ds(pl.multiple_of(g * gq, gq), gq), :], in_a)
        s_scr[:, 0:gk] = _dot_nt(q2, k_ref[pl.ds(key_base(g), gk), :]) + bias_ref[cls]
        s_scr[:, gk:gk + cl] = _dot_nt(q2, kc_ref[...])

    def attend(g, s_scr):
        sc = s_scr[...]
        p = jnp.exp2(sc - jnp.max(sc, axis=-1, keepdims=True))
        den = jnp.sum(p, axis=-1, keepdims=True)
        pb = p.astype(BF16)
        o2 = (jnp.dot(pb[:, 0:gk], v_ref[pl.ds(key_base(g), gk), :], preferred_element_type=F32)
              + jnp.dot(pb[:, gk:gk + cl], vc_ref[...], preferred_element_type=F32)) / den
        o_ref[pl.ds(pl.multiple_of(g * gq, gq), gq), :] = jnp.where(in_a, o2[:gq], o2[gq:]).astype(o_ref.dtype)

    scores(0, sa_scr)

    def pair(it, carry):
        g = 2 * it
        scores(g + 1, sb_scr)
        attend(g, sa_scr)
        scores(g + 2, sa_scr)
        attend(g + 1, sb_scr)
        return carry

    lax.fori_loop(0, ngrp // 2 - 1, pair, 0)
    scores(ngrp - 1, sb_scr)
    attend(ngrp - 2, sa_scr)
    attend(ngrp - 1, sb_scr)


def _bias_table_kernel(idx_ref, t_ref, o_ref):
    c = pl.program_id(1)
    left = lax.broadcasted_iota(jnp.int32, (1, LANES), 1) < GRID_W
    per_class = NA_GROUP_ROWS * NA_KEY_ROWS
    for g in range(2):
        for a in range(NA_GROUP_ROWS):
            r0 = (g * NA_GROUP_ROWS + a) * GRID_W
            for j in range(NA_KEY_ROWS // 2):
                o_even = idx_ref[c * per_class + a * NA_KEY_ROWS + 2 * j]
                o_odd = idx_ref[c * per_class + a * NA_KEY_ROWS + 2 * j + 1]
                o_ref[r0:r0 + GRID_W, j * LANES:(j + 1) * LANES] = jnp.where(left, t_ref[g, o_even], t_ref[g, o_odd])


def _na_bias_table(rpb, rows):
    nh, n_ro, n_co = rpb.shape
    ngrp = rows // NA_GROUP_ROWS
    qc = np.arange(GRID_W)[:, None]
    kc = np.arange(GRID_W)[None, :]
    wstart = np.clip(qc - WIN_W // 2, 0, GRID_W - WIN_W)
    col_valid = (kc >= wstart) & (kc < wstart + WIN_W)
    col_onehot = ((kc - qc + WIN_W - 1)[None] == np.arange(n_co)[:, None, None])
    rpb2 = rpb.astype(F32).reshape(nh // 2, 2, n_ro, n_co)
    by_col = jnp.einsum("pgrj,jqk->pgrqk", rpb2, col_onehot.astype(np.float32),
                        precision=lax.Precision.HIGHEST) * np.float32(np.log2(np.e))
    by_col = jnp.where(col_valid, by_col, MASK_VALUE)
    by_col = jnp.concatenate([by_col, jnp.full((nh // 2, 2, 1, GRID_W, GRID_W), MASK_VALUE, F32)], axis=2)
    tables = jnp.concatenate([by_col, by_col], axis=-1)
    idx = np.full((3, NA_GROUP_ROWS, NA_KEY_ROWS), n_ro, np.int32)
    for ci, g in enumerate((0, 1, ngrp - 1)):
        kb = int(np.clip(NA_GROUP_ROWS * g - WIN_H // 2, 0, rows - NA_KEY_ROWS))
        for a in range(NA_GROUP_ROWS):
            r = NA_GROUP_ROWS * g + a
            start = int(np.clip(r - WIN_H // 2, 0, rows - WIN_H))
            for cr in range(NA_KEY_ROWS):
                if start <= kb + cr < start + WIN_H:
                    idx[ci, a, cr] = kb + cr - r + WIN_H - 1
    gq, gk = NA_GROUP_ROWS * GRID_W, NA_KEY_ROWS * GRID_W
    return pl.pallas_call(
        _bias_table_kernel,
        grid_spec=pltpu.PrefetchScalarGridSpec(
            num_scalar_prefetch=1,
            grid=(nh // 2, 3),
            in_specs=[pl.BlockSpec((None,) + tables.shape[1:], lambda p, c, ix: (p, 0, 0, 0, 0))],
            out_specs=pl.BlockSpec((None, None, 2 * gq, gk), lambda p, c, ix: (p, c, 0, 0)),
        ),
        out_shape=jax.ShapeDtypeStruct((nh // 2, 3, 2 * gq, gk), F32),
        compiler_params=_params("parallel", "parallel"),
        name="na_bias_table",
    )(jnp.asarray(idx.reshape(-1)), tables)


def _neighbourhood_attention(z_lat, z_ctx, bias, seq):
    nb, s, _ = z_lat.shape
    cl = z_ctx.shape[1]
    npair = bias.shape[0]
    d_att = npair * LANES
    cb = npair
    rows = s // GRID_W
    assert s % GRID_W == 0 and rows % (2 * NA_GROUP_ROWS) == 0 and rows >= NA_KEY_ROWS + NA_GROUP_ROWS
    s_shape = (2 * NA_GROUP_ROWS * GRID_W, NA_KEY_ROWS * GRID_W + cl)
    lat = lambda seg: pl.BlockSpec((None, s, LANES), lambda p, b: (b, 0, seg * cb + p))
    ctx = lambda seg: pl.BlockSpec((None, cl, LANES), lambda p, b: (b, 0, seg * cb + p))
    return pl.pallas_call(
        functools.partial(_na_kernel, rows=rows),
        grid=(npair, nb),
        in_specs=[lat(4), lat(1), lat(2), ctx(1), ctx(2),
                  pl.BlockSpec((None,) + bias.shape[1:], lambda p, b: (p, 0, 0, 0))],
        out_specs=pl.BlockSpec((None, s, LANES), lambda p, b: (b, 0, p)),
        out_shape=jax.ShapeDtypeStruct((nb, s, d_att), BF16),
        scratch_shapes=[pltpu.VMEM(s_shape, F32), pltpu.VMEM(s_shape, F32)],
        compiler_params=_params("parallel", "parallel"),
        name="neighbourhood_attention",
    )(z_lat, z_lat, z_lat, z_ctx, z_ctx, bias)


def _ctx_attn_kernel(q_ref, k_ref, v_ref, o_ref):
    in_a = lax.broadcasted_iota(jnp.int32, (1, LANES), 1) < HEAD_DIM
    n = q_ref.shape[0]
    sc = _dot_nt(_stack_heads(q_ref[...], in_a), k_ref[...])
    p = jnp.exp2(sc - jnp.max(sc, axis=-1, keepdims=True))
    den = jnp.sum(p, axis=-1, keepdims=True)
    o2 = jnp.dot(p.astype(BF16), v_ref[...], preferred_element_type=F32) / den
    o_ref[...] = jnp.where(in_a, o2[:n], o2[n:]).astype(o_ref.dtype)


def _context_attention(z_ctx, d_att):
    nb, cl, _ = z_ctx.shape
    cb = d_att // LANES
    ctx = lambda seg: pl.BlockSpec((None, cl, LANES), lambda p, b: (b, 0, seg * cb + p))
    return pl.pallas_call(
        _ctx_attn_kernel,
        grid=(cb, nb),
        in_specs=[ctx(4), ctx(1), ctx(2)],
        out_specs=pl.BlockSpec((None, cl, LANES), lambda p, b: (b, 0, p)),
        out_shape=jax.ShapeDtypeStruct((nb, cl, d_att), BF16),
        compiler_params=_params("parallel", "parallel"),
        name="context_attention",
    )(z_ctx, z_ctx, z_ctx)


def _merge_kernel(x_ref, g_ref, hs_ref, y_ref, na_ref, gr_ref, gn_ref, wr_ref, wn_ref, wo_ref, *rest, n_experts):
    if n_experts:
        sh2_ref, sc2_ref, rw_ref, rwl_ref, o_ref, r_ref, t_scr = rest
    else:
        o_ref, t_scr = rest
    nb, ts, d = x_ref.shape
    c = hs_ref.shape[2]
    rows2d = lambda ref: ref[...].reshape(nb * ts, ref.shape[2])
    for t in range(ts):
        hs_t = hs_ref[t].astype(F32)
        for sl in range(c // LANES):
            t_scr.at[sl][pl.ds(t, nb, stride=XR_PITCH), :] = hs_t[:, sl * LANES:(sl + 1) * LANES]
    hs = jnp.concatenate([jnp.concatenate([t_scr[sl, b * XR_PITCH:b * XR_PITCH + ts, :] for sl in range(c // LANES)],
                                          axis=-1) for b in range(nb)], axis=0)
    y_rnn = (hs * _gelu_tanh(rows2d(y_ref).astype(F32))).astype(BF16)
    t_rnn = jnp.dot(y_rnn, wr_ref[...], preferred_element_type=F32)
    t_na = jnp.dot(rows2d(na_ref), wn_ref[...], preferred_element_type=F32)
    mix = _sigmoid(rows2d(gr_ref).astype(F32)) * t_rnn + _sigmoid(rows2d(gn_ref).astype(F32)) * t_na
    out = jnp.dot(mix.astype(BF16), wo_ref[...], preferred_element_type=F32)
    x_new = x_ref[...] + g_ref[...] * out.reshape(nb, ts, d)
    o_ref[...] = x_new
    if n_experts:
        h2 = _modulated_norm(x_new, sh2_ref[...], sc2_ref[...]).reshape(nb * ts, d)
        r_ref[...] = _top2_route(h2, rw_ref, rwl_ref, n_experts).reshape(nb, ts, LANES)


def _router_operands(router):
    d, n_experts = router.shape
    wr32 = jnp.zeros((d, LANES), F32).at[:, :n_experts].set(router)
    wr = wr32.astype(BF16)
    return wr, (wr32 - wr.astype(F32)).astype(BF16)


def _merge(x3d, gate, hs_tm, t_off, z3, na3, w_rnn_o, w_na_o, w_out, route_args=None):
    nb, length, d = x3d.shape
    c = hs_tm.shape[2]
    da = na3.shape[2]
    ts = _pick_tile(length, IN_PROJ_STEPS, 16)
    assert ts + 8 <= XR_PITCH and t_off % ts == 0
    y_blk = (c + 2 * da) // c
    gr_blk = (2 * c + 3 * da) // d
    gn_blk = gr_blk + 1
    row = lambda width, blk=0: pl.BlockSpec((nb, ts, width), lambda i: (0, i, blk))
    whole = lambda a: pl.BlockSpec(a.shape, lambda i: (0,) * a.ndim, pipeline_mode=pl.Buffered(1))
    in_specs = [row(d), pl.BlockSpec(gate.shape, lambda i: (0, 0, 0)),
                pl.BlockSpec((ts, nb, c), lambda i: (i + t_off // ts, 0, 0)),
                row(c, y_blk), row(da), row(d, gr_blk), row(d, gn_blk),
                whole(w_rnn_o), whole(w_na_o), whole(w_out)]
    args = [x3d, gate, hs_tm, z3, na3, z3, z3, w_rnn_o, w_na_o, w_out]
    out_specs, out_shape, n_experts = row(d), jax.ShapeDtypeStruct((nb, length, d), F32), 0
    if route_args is not None:
        shift2, scale2, router = route_args
        n_experts = router.shape[1]
        rw, rwl = _router_operands(router)
        in_specs += [pl.BlockSpec(shift2.shape, lambda i: (0, 0, 0)), pl.BlockSpec(scale2.shape, lambda i: (0, 0, 0)),
                     whole(rw), whole(rwl)]
        args += [shift2, scale2, rw, rwl]
        out_specs = [out_specs, row(LANES)]
        out_shape = [out_shape, jax.ShapeDtypeStruct((nb, length, LANES), F32)]
    return pl.pallas_call(
        functools.partial(_merge_kernel, n_experts=n_experts),
        grid=(length // ts,),
        in_specs=in_specs,
        out_specs=out_specs,
        out_shape=out_shape,
        scratch_shapes=[pltpu.VMEM((c // LANES, nb * XR_PITCH, LANES), F32)],
        compiler_params=_params("parallel"),
        name="merge_out_proj",
    )(*args)


def _ffn_kernel(x_ref, sh_ref, sc_ref, g_ref, w1_ref, w3_ref, w2_ref, o_ref, *, tf):
    x = x_ref[...]
    h = _modulated_norm(x, sh_ref[...], sc_ref[...]).astype(BF16)
    acc = None
    for j in range(w1_ref.shape[1] // tf):
        cols = slice(j * tf, (j + 1) * tf)
        a = jnp.dot(h, w1_ref[:, cols], preferred_element_type=F32)
        b = jnp.dot(h, w3_ref[:, cols], preferred_element_type=F32)
        part = jnp.dot((_silu(a) * b).astype(BF16), w2_ref[cols, :], preferred_element_type=F32)
        acc = part if acc is None else acc + part
    o_ref[...] = x + g_ref[...] * acc


def _dense_ffn(x2d, shift, scale, gate, w1, w3, w2, rows_per_mod):
    m, d = x2d.shape
    dff = w1.shape[1]
    tm = _pick_tile(rows_per_mod, 512, 8)
    tf = _pick_tile(dff, 1536, LANES)
    tiles_per_mod = rows_per_mod // tm
    mod = pl.BlockSpec((None, 1, d), lambda i: (i // tiles_per_mod, 0, 0))
    whole = lambda w: pl.BlockSpec(w.shape, lambda i: (0, 0), pipeline_mode=pl.Buffered(1))
    return pl.pallas_call(
        functools.partial(_ffn_kernel, tf=tf),
        grid=(m // tm,),
        in_specs=[pl.BlockSpec((tm, d), lambda i: (i, 0)), mod, mod, mod, whole(w1), whole(w3), whole(w2)],
        out_specs=pl.BlockSpec((tm, d), lambda i: (i, 0)),
        out_shape=jax.ShapeDtypeStruct((m, d), F32),
        compiler_params=_params("parallel"),
        name="dense_swiglu",
    )(x2d, shift, scale, gate, w1, w3, w2)


SUBLANES = 8


def _to_token_tiles(ref, x):
    n = x.shape[0]
    for s in range(SUBLANES):
        ref[pl.ds(s, n, stride=SUBLANES), :] = x[:, s * LANES:(s + 1) * LANES]


def _from_token_tiles(ref, n, s):
    return ref[pl.ds(s, n, stride=SUBLANES), :]


def _route_kernel(x_ref, sh_ref, sc_ref, wr_ref, wrl_ref, r_ref, *, n_experts):
    h = _modulated_norm(x_ref[...], sh_ref[...], sc_ref[...])
    r_ref[...] = _top2_route(h, wr_ref, wrl_ref, n_experts)


def _top2_route(h, wr_ref, wrl_ref, n_experts):
    h_hi = h.astype(BF16)
    h_lo = (h - h_hi.astype(F32)).astype(BF16)
    logits = (jnp.dot(h_hi, wr_ref[...], preferred_element_type=F32)
              + jnp.dot(h_lo, wr_ref[...], preferred_element_type=F32)
              + jnp.dot(h_hi, wrl_ref[...], preferred_element_type=F32))
    lane = lax.broadcasted_iota(jnp.int32, logits.shape, 1).astype(F32)
    neg = -jnp.inf
    lg = jnp.where(lane < n_experts, logits, neg)
    m1 = jnp.max(lg, axis=-1, keepdims=True)
    i1 = jnp.min(jnp.where(lg == m1, lane, float(LANES)), axis=-1, keepdims=True)
    lg2 = jnp.where(lane == i1, neg, lg)
    m2 = jnp.max(lg2, axis=-1, keepdims=True)
    i2 = jnp.min(jnp.where(lg2 == m2, lane, float(LANES)), axis=-1, keepdims=True)
    e = jnp.exp(m2 - m1)
    w1 = 1.0 / (1.0 + e)
    w2 = e / (1.0 + e)
    return jnp.where(lane == 0, i1, jnp.where(lane == 1, i2, jnp.where(lane == 2, w1, jnp.where(lane == 3, w2, 0.0))))


def _route(x2d, shift, scale, router, rows_per_mod):
    m, d = x2d.shape
    n_experts = router.shape[1]
    wr, wrl = _router_operands(router)
    tm = _pick_tile(rows_per_mod, 512, 8)
    tiles_per_mod = rows_per_mod // tm
    mod = pl.BlockSpec((None, 1, d), lambda i: (i // tiles_per_mod, 0, 0))
    rspec = pl.BlockSpec((d, LANES), lambda i: (0, 0))
    return pl.pallas_call(
        functools.partial(_route_kernel, n_experts=n_experts),
        grid=(m // tm,),
        in_specs=[pl.BlockSpec((tm, d), lambda i: (i, 0)), mod, mod, rspec, rspec],
        out_specs=pl.BlockSpec((tm, LANES), lambda i: (i, 0)),
        out_shape=jax.ShapeDtypeStruct((m, LANES), F32),
        compiler_params=_params("parallel"),
        name="moe_route",
    )(x2d, shift, scale, wr, wrl)


def _tile_rows(tile):
    start = tile * SUBLANES
    return pl.ds(start if isinstance(start, int) else pl.multiple_of(start, SUBLANES), SUBLANES)


def _tile_copy(src_ref, src_tile, dst_ref, dst_tile, sem):
    return pltpu.make_async_copy(src_ref.at[_tile_rows(src_tile), :], dst_ref.at[_tile_rows(dst_tile), :], sem)


def _dispatch_kernel(pos_ref, pad_ref, x_ref, sh_ref, sc_ref, xs_hbm, hbuf, zbuf, sem, zsem):
    i = pl.program_id(0)
    n = pl.num_programs(0)
    tm = x_ref.shape[0]
    m = n * tm
    pads_per_step = pad_ref.shape[0] // n
    slot = i % 2

    def wait_slot(sl):
        rows = TOP_K * tm * SUBLANES
        pltpu.make_async_copy(xs_hbm.at[pl.ds(0, rows), :], xs_hbm.at[pl.ds(0, rows), :], sem.at[sl]).wait()

    @pl.when(i >= 2)
    def _():
        wait_slot(slot)

    _to_token_tiles(hbuf.at[slot], _modulated_norm(x_ref[...], sh_ref[...], sc_ref[...]))
    for k in range(TOP_K):
        base = k * m + i * tm

        def put(j, carry):
            _tile_copy(hbuf.at[slot], j, xs_hbm, pos_ref[base + j], sem.at[slot]).start()
            return carry

        lax.fori_loop(0, tm, put, 0, unroll=8)

    @pl.when(i == 0)
    def _():
        zbuf[...] = jnp.zeros_like(zbuf)

    def clear(q, carry):
        _tile_copy(zbuf, 0, xs_hbm, pad_ref[i * pads_per_step + q], zsem).start()
        return carry

    lax.fori_loop(0, pads_per_step, clear, 0, unroll=8)

    @pl.when(i == n - 1)
    def _():
        wait_slot(slot)

        @pl.when(n >= 2)
        def _():
            wait_slot(1 - slot)

        rows = pad_ref.shape[0] * SUBLANES
        pltpu.make_async_copy(xs_hbm.at[pl.ds(0, rows), :], xs_hbm.at[pl.ds(0, rows), :], zsem).wait()


def _dispatch(x2d, shift, scale, pos, pad_rows, n_rows, rows_per_mod):
    m, d = x2d.shape
    assert d == SUBLANES * LANES
    tm = _pick_tile(rows_per_mod, 512, 8)
    tiles_per_mod = rows_per_mod // tm
    assert pad_rows.shape[0] % (m // tm) == 0
    mod = pl.BlockSpec((None, 1, d), lambda i, ps, pd: (i // tiles_per_mod, 0, 0))
    return pl.pallas_call(
        _dispatch_kernel,
        grid_spec=pltpu.PrefetchScalarGridSpec(
            num_scalar_prefetch=2,
            grid=(m // tm,),
            in_specs=[pl.BlockSpec((tm, d), lambda i, ps, pd: (i, 0)), mod, mod],
            out_specs=pl.BlockSpec(memory_space=pl.ANY),
            scratch_shapes=[pltpu.VMEM((2, tm * SUBLANES, LANES), F32), pltpu.VMEM((SUBLANES, LANES), F32),
                            pltpu.SemaphoreType.DMA((2,)), pltpu.SemaphoreType.DMA(())],
        ),
        out_shape=jax.ShapeDtypeStruct((n_rows * SUBLANES, LANES), F32),
        compiler_params=_params("arbitrary"),
        name="moe_dispatch",
    )(pos, pad_rows, x2d, shift, scale)


EXPERT_TILE_ROWS = 1024


def _expert_ffn_kernel(te_ref, nv_ref, x_ref, w1_ref, w3_ref, w2_ref, o_ref, h_scr, acc_scr):
    t = pl.program_id(0)
    f = pl.program_id(1)
    nv = nv_ref[0]
    tm = h_scr.shape[0]

    @pl.when(t < nv)
    def _():
        @pl.when(f == 0)
        def _():
            for s in range(SUBLANES):
                h_scr[:, s * LANES:(s + 1) * LANES] = _from_token_tiles(x_ref, tm, s).astype(BF16)
            acc_scr[...] = jnp.zeros_like(acc_scr)

        h = h_scr[...]
        a = jnp.dot(h, w1_ref[...].astype(BF16), preferred_element_type=F32)
        b = jnp.dot(h, w3_ref[...].astype(BF16), preferred_element_type=F32)
        acc_scr[...] += jnp.dot((_silu(a) * b).astype(BF16), w2_ref[...].astype(BF16),
                                preferred_element_type=F32)

        @pl.when(f == pl.num_programs(1) - 1)
        def _():
            _to_token_tiles(o_ref, acc_scr[...])

    @pl.when(jnp.logical_and(t >= nv, f == pl.num_programs(1) - 1))
    def _():
        o_ref[...] = jnp.zeros_like(o_ref)


def _expert_ffn(xs, tile_expert, n_valid, w1, w3, w2):
    tm = EXPERT_TILE_ROWS
    n = xs.shape[0] // SUBLANES
    d = w1.shape[1]
    dfe = w1.shape[2]
    tf = _pick_tile(dfe, 512, LANES)
    nf = dfe // tf

    def fsel(t, f, nv):
        return jnp.where(t < nv[0], f, nf - 1)

    return pl.pallas_call(
        _expert_ffn_kernel,
        grid_spec=pltpu.PrefetchScalarGridSpec(
            num_scalar_prefetch=2,
            grid=(n // tm, nf),
            in_specs=[pl.BlockSpec((tm * SUBLANES, LANES), lambda t, f, te, nv: (t, 0)),
                      pl.BlockSpec((None, d, tf), lambda t, f, te, nv: (te[t], 0, fsel(t, f, nv))),
                      pl.BlockSpec((None, d, tf), lambda t, f, te, nv: (te[t], 0, fsel(t, f, nv))),
                      pl.BlockSpec((None, tf, d), lambda t, f, te, nv: (te[t], fsel(t, f, nv), 0))],
            out_specs=pl.BlockSpec((tm * SUBLANES, LANES), lambda t, f, te, nv: (t, 0)),
            scratch_shapes=[pltpu.VMEM((tm, d), BF16), pltpu.VMEM((tm, d), F32)],
        ),
        out_shape=jax.ShapeDtypeStruct((n * SUBLANES, LANES), F32),
        compiler_params=_params("arbitrary", "arbitrary"),
        name="expert_swiglu",
    )(tile_expert, n_valid, xs, w1, w3, w2)


def _combine_kernel(pos_ref, x_ref, g_ref, r_ref, ys_hbm, o_ref, ybuf, sem):
    i = pl.program_id(0)
    n = pl.num_programs(0)
    tm = x_ref.shape[0]
    m = n * tm

    def issue_rows(tile, slot):
        for k in range(TOP_K):
            base = k * m + tile * tm

            def body(j, carry):
                _tile_copy(ys_hbm, pos_ref[base + j], ybuf.at[slot, k], j, sem.at[slot]).start()
                return carry

            lax.fori_loop(0, tm, body, 0, unroll=8)

    slot = i % 2

    @pl.when(i == 0)
    def _():
        issue_rows(0, 0)

    pltpu.make_async_copy(ybuf.at[1 - slot], ybuf.at[slot], sem.at[slot]).wait()

    @pl.when(i + 1 < n)
    def _():
        issue_rows(i + 1, 1 - slot)

    r = r_ref[...]
    w1, w2 = r[:, 2:3], r[:, 3:4]
    for s in range(SUBLANES):
        lanes = slice(s * LANES, (s + 1) * LANES)
        mix = (w1 * _from_token_tiles(ybuf.at[slot, 0], tm, s) + w2 * _from_token_tiles(ybuf.at[slot, 1], tm, s))
        o_ref[:, lanes] = x_ref[:, lanes] + g_ref[:, lanes] * mix


def _combine(x2d, gate, route, ys, pos, rows_per_mod):
    m, d = x2d.shape
    tm = _pick_tile(rows_per_mod, 512, 8)
    tiles_per_mod = rows_per_mod // tm
    return pl.pallas_call(
        _combine_kernel,
        grid_spec=pltpu.PrefetchScalarGridSpec(
            num_scalar_prefetch=1,
            grid=(m // tm,),
            in_specs=[pl.BlockSpec((tm, d), lambda i, ps: (i, 0)),
                      pl.BlockSpec((None, 1, d), lambda i, ps: (i // tiles_per_mod, 0, 0)),
                      pl.BlockSpec((tm, LANES), lambda i, ps: (i, 0)),
                      pl.BlockSpec(memory_space=pl.ANY)],
            out_specs=pl.BlockSpec((tm, d), lambda i, ps: (i, 0)),
            scratch_shapes=[pltpu.VMEM((2, TOP_K, tm * SUBLANES, LANES), F32), pltpu.SemaphoreType.DMA((2,))],
        ),
        out_shape=jax.ShapeDtypeStruct((m, d), F32),
        compiler_params=_params("arbitrary"),
        name="moe_combine",
    )(pos, x2d, gate, route, ys)


def _moe_ffn(x2d, shift, scale, gate, router, w1, w3, w2, rows_per_mod, route=None):
    m, d = x2d.shape
    n_experts = router.shape[1]
    if route is None:
        route = _route(x2d, shift, scale, router, rows_per_mod)
    tm = EXPERT_TILE_ROWS
    expert = route[:, :TOP_K].astype(jnp.int32).T.reshape(-1)
    onehot = (expert[:, None] == jnp.arange(n_experts)[None, :]).astype(jnp.int32)
    csum = jnp.cumsum(onehot, axis=0)
    rank = jnp.sum(onehot * (csum - 1), axis=1)
    counts = csum[-1]
    tiles = (counts + tm - 1) // tm
    tile_end = jnp.cumsum(tiles)
    first_row = (tile_end - tiles) * tm
    pos = (jnp.sum(onehot * first_row[None, :], axis=1) + rank).astype(jnp.int32)
    n_tiles = (TOP_K * m) // tm + n_experts
    n_rows = n_tiles * tm
    gap_len = jnp.concatenate([tiles * tm - counts, (n_rows - tile_end[-1:] * tm)])
    gap_row = jnp.concatenate([first_row + counts, tile_end[-1:] * tm])
    gap_end = jnp.cumsum(gap_len)
    q = jnp.arange(n_rows - TOP_K * m)
    in_gap = jnp.logical_and(q[:, None] >= (gap_end - gap_len)[None, :], q[:, None] < gap_end[None, :])
    pad_rows = jnp.sum(jnp.where(in_gap, (gap_row - (gap_end - gap_len))[None, :] + q[:, None], 0),
                       axis=1).astype(jnp.int32)
    tile_expert = jnp.minimum(jnp.sum((jnp.arange(n_tiles)[:, None] >= tile_end[None, :]).astype(jnp.int32), axis=1),
                              n_experts - 1).astype(jnp.int32)
    n_valid = tile_end[-1:].astype(jnp.int32)
    xs = _dispatch(x2d, shift, scale, pos, pad_rows, n_rows, rows_per_mod)
    ys = _expert_ffn(xs, tile_expert, n_valid, w1, w3, w2)
    return _combine(x2d, gate, route, ys, pos, rows_per_mod)


def kernel(x, c, ctx, c_ctx, w_mod, b_mod, w_in, conv_w, conv_b, rg_lambda, rg_w, rg_b, q_gain, k_gain, rpb,
           w_rnn_o, w_na_o, w_out, ffn_w1, ffn_w3, ffn_w2, router, moe_w1, moe_w3, moe_w2):
    nb, seq, d = x.shape
    cl = ctx.shape[1]
    depth = w_mod.shape[0]
    c_rnn = conv_w.shape[2]
    d_att = rpb.shape[1] * HEAD_DIM
    ctx_cols = c_rnn + 2 * d_att
    rows = seq // GRID_W

    n_cond = -(-(nb + 1) // 8) * 8
    cond = jnp.zeros((n_cond, d), F32).at[:nb].set(c).at[nb].set(c_ctx)
    mods = _adaln(cond, w_mod, b_mod)

    x2 = x.reshape(nb * seq, d)
    xc2 = ctx.reshape(nb * cl, d)
    for l in range(depth):
        ctx_out = l < depth - 1
        lat = [mods[l, :nb, k * d:(k + 1) * d].reshape(nb, 1, d) for k in range(N_MOD)]
        cmod = [mods[l, nb:nb + 1, k * d:(k + 1) * d].reshape(1, 1, d) for k in range(N_MOD)]
        w_in_l = w_in[l].astype(BF16)
        tile_heads = lambda g: jnp.tile(g.astype(F32), c_rnn // HEAD_DIM).reshape(1, c_rnn)
        head_gains = jnp.stack([tile_heads(k_gain[l]), tile_heads(q_gain[l]) * Q_SCALE])

        z_lat3, xr_lat = _norm_matmul(x2.reshape(nb, seq, d), lat[0], lat[1], w_in_l, head_gains, c_rnn, (1, 4))
        if ctx_out:
            z_ctx3, xr_ctx = _norm_matmul(xc2.reshape(nb, cl, d), cmod[0], cmod[1], w_in_l, head_gains, c_rnn, (1, 4))
        else:
            z_ctx3, xr_ctx = _norm_matmul(xc2.reshape(nb, cl, d), cmod[0], cmod[1], w_in_l[:, :ctx_cols],
                                          head_gains, c_rnn, (1,))

        wg_f, bg_f = _gate_weights(rg_w[l, 0], rg_b[l, 0])
        wg_r, bg_r = _gate_weights(rg_w[l, 1], rg_b[l, 1])
        hs_r, xc = _rnn_scan_conv(xr_ctx, xr_lat, conv_w[l], conv_b[l], wg_r, bg_r, rg_lambda[l, 1], True)
        hs = _rnn_scan_reuse(xc, cl, wg_f, bg_f, rg_lambda[l, 0], False, hs_r)

        bias = _na_bias_table(rpb[l], rows)
        na_lat = _neighbourhood_attention(z_lat3, z_ctx3, bias, seq)

        wr, wn, wo = w_rnn_o[l].astype(BF16), w_na_o[l].astype(BF16), w_out[l].astype(BF16)
        route_lat = None
        if l % 2 == 0:
            x2 = _merge(x2.reshape(nb, seq, d), lat[2], hs, cl, z_lat3, na_lat, wr, wn, wo).reshape(nb * seq, d)
        else:
            x3, route3 = _merge(x2.reshape(nb, seq, d), lat[2], hs, cl, z_lat3, na_lat, wr, wn, wo,
                                route_args=(lat[3], lat[4], router[l // 2]))
            x2, route_lat = x3.reshape(nb * seq, d), route3.reshape(nb * seq, LANES)
        if ctx_out:
            na_ctx = _context_attention(z_ctx3, d_att)
            xc2 = _merge(xc2.reshape(nb, cl, d), cmod[2], hs, 0, z_ctx3, na_ctx, wr, wn, wo).reshape(nb * cl, d)

        j = l // 2
        if l % 2 == 0:
            w1, w3, w2 = ffn_w1[j].astype(BF16), ffn_w3[j].astype(BF16), ffn_w2[j].astype(BF16)
            x2 = _dense_ffn(x2, lat[3], lat[4], lat[5], w1, w3, w2, seq)
            if ctx_out:
                xc2 = _dense_ffn(xc2, cmod[3], cmod[4], cmod[5], w1, w3, w2, nb * cl)
        else:
            w1, w3, w2 = moe_w1[j], moe_w3[j], moe_w2[j]
            x2 = _moe_ffn(x2, lat[3], lat[4], lat[5], router[j], w1, w3, w2, seq, route=route_lat)
            if ctx_out:
                xc2 = _moe_ffn(xc2, cmod[3], cmod[4], cmod[5], router[j], w1, w3, w2, nb * cl)
    return x2.reshape(nb, seq, d)
```
